```python
import jax, jax.numpy as jnp
from jax import lax
import numpy as np

D_MODEL = 1024
BATCH = 2
SEQ = 16384
DEPTH = 4

GRID_W = 64
CTX_LEN = 256
D_CONV = D_MODEL // 2
CONV_K = 3
D_SSM = D_MODEL // 2
SSM_GROUP = 16
N_GROUPS = D_SSM // SSM_GROUP
STATE = 64
N_DIR = 2
D_FF = -(-8 * D_MODEL // (3 * 256)) * 256
D_IN_PROJ = 3 * D_CONV + D_SSM + 2 * D_MODEL
SPLITS = [D_CONV, 2 * D_CONV, 3 * D_CONV, 3 * D_CONV + D_SSM, 3 * D_CONV + D_SSM + D_MODEL]
U_LO, U_HI = 3 * D_CONV, 3 * D_CONV + D_SSM
RMS_EPS = 1e-6
DT_MIN, DT_MAX = 1e-3, 1e-1

kernel_name = "hybrid_shortconv_s5_adaln_prefix_block"


def rmsnorm(x, g):
    xf = x.astype(jnp.float32)
    y = xf * lax.rsqrt(jnp.mean(xf * xf, axis=-1, keepdims=True) + RMS_EPS)
    return (y * g.astype(jnp.float32)).astype(x.dtype)


def modulate(h, shift, scale):
    return h * (1 + scale) + shift


def conv_centered(z, w):
    pad = CONV_K // 2
    n = z.shape[-2]
    zp = jnp.pad(z, [(0, 0)] * (z.ndim - 2) + [(pad, pad), (0, 0)])
    return sum(lax.slice_in_dim(zp, k, k + n, axis=z.ndim - 2) * w[k] for k in range(CONV_K))


def zoh(lam_re, lam_im, log_dt, b_re, b_im):
    lam_re = lam_re.astype(jnp.float32)
    lam_im = lam_im.astype(jnp.float32)
    dt = jnp.exp(log_dt.astype(jnp.float32))[:, None]
    mag = jnp.exp(lam_re * dt)
    a_re = mag * jnp.cos(lam_im * dt)
    a_im = mag * jnp.sin(lam_im * dt)
    nr, ni = a_re - 1.0, a_im
    den = lam_re * lam_re + lam_im * lam_im
    f_re = (nr * lam_re + ni * lam_im) / den
    f_im = (ni * lam_re - nr * lam_im) / den
    br, bi = b_re.astype(jnp.float32), b_im.astype(jnp.float32)
    bb_re = f_re[..., None] * br - f_im[..., None] * bi
    bb_im = f_re[..., None] * bi + f_im[..., None] * br
    return a_re, a_im, bb_re, bb_im


def _linrec_combine(e1, e2):
    a1r, a1i, b1r, b1i = e1
    a2r, a2i, b2r, b2i = e2
    return (a2r * a1r - a2i * a1i,
            a2r * a1i + a2i * a1r,
            a2r * b1r - a2i * b1i + b2r,
            a2r * b1i + a2i * b1r + b2i)


def ssm_states(u, lam_re, lam_im, log_dt, b_re, b_im, h0):
    bsz, n = u.shape[0], u.shape[1]
    ug = u.astype(jnp.float32).reshape(bsz, n, N_GROUPS, SSM_GROUP)
    states = []
    for d in range(N_DIR):
        reverse = d == 1
        a_re, a_im, bb_re, bb_im = zoh(lam_re[d], lam_im[d], log_dt[d], b_re[d], b_im[d])
        bu_re = jnp.einsum('bngh,gph->bngp', ug, bb_re)
        bu_im = jnp.einsum('bngh,gph->bngp', ug, bb_im)
        if h0 is not None:
            hr, hi = h0[d]
            first = n - 1 if reverse else 0
            bu_re = bu_re.at[:, first].add(a_re * hr - a_im * hi)
            bu_im = bu_im.at[:, first].add(a_re * hi + a_im * hr)
        ar = jnp.broadcast_to(a_re, bu_re.shape)
        ai = jnp.broadcast_to(a_im, bu_re.shape)
        _, _, s_re, s_im = lax.associative_scan(
            _linrec_combine, (ar, ai, bu_re, bu_im), reverse=reverse, axis=1)
        states.append((s_re, s_im))
    return states


def final_states(states):
    (fr, fi), (br, bi) = states
    return [(fr[:, -1], fi[:, -1]), (br[:, 0], bi[:, 0])]


def ssm_readout(states, u, c_re, c_im, d_skip):
    bsz, n = u.shape[0], u.shape[1]
    y = u.astype(jnp.float32) * d_skip.astype(jnp.float32)
    for d, (s_re, s_im) in enumerate(states):
        yd = (jnp.einsum('bngp,ghp->bngh', s_re, c_re[d].astype(jnp.float32))
              - jnp.einsum('bngp,ghp->bngh', s_im, c_im[d].astype(jnp.float32)))
        y = y + yd.reshape(bsz, n, D_SSM)
    return y.astype(u.dtype)


def mixer(h, lp, rows, h0):
    z = h @ lp['w_in'] + lp['b_in']
    g_b, g_c, x_in, u, gate_a, gate_b = jnp.split(z, SPLITS, axis=-1)
    v = g_c * x_in
    if rows is None:
        v = conv_centered(v, lp['conv_w'])
    else:
        bsz, n, ch = v.shape
        v = conv_centered(v.reshape(bsz, rows, GRID_W, ch), lp['conv_w']).reshape(bsz, n, ch)
    y_a = (g_b * v) @ lp['w_out_a']
    states = ssm_states(u, lp['lam_re'], lp['lam_im'], lp['log_dt'], lp['b_re'], lp['b_im'], h0)
    s = jax.nn.gelu(ssm_readout(states, u, lp['c_re'], lp['c_im'], lp['d_skip']))
    glu_v, glu_g = jnp.split(s @ lp['w_glu'] + lp['b_glu'], 2, axis=-1)
    y_b = glu_v * jax.nn.sigmoid(glu_g)
    merged = jax.nn.sigmoid(gate_a) * y_a + jax.nn.sigmoid(gate_b) * y_b
    return merged @ lp['w_o'], states


def swiglu(h, w_in, w_out):
    g, u = jnp.split(h @ w_in, 2, axis=-1)
    return (jax.nn.silu(g) * u) @ w_out


def setup_inputs(seed: int = 0) -> dict:
    key = jax.random.key(seed)
    ks = jax.random.split(key, 32)
    f32 = jnp.float32
    nrm = lambda k, shape, s: jax.random.normal(k, shape, f32) * s
    n_idx = jnp.arange(STATE, dtype=f32)
    lam_re = -0.5 + nrm(ks[10], (DEPTH, N_DIR, N_GROUPS, STATE), 0.01)
    lam_im = jnp.pi * n_idx + nrm(ks[11], (DEPTH, N_DIR, N_GROUPS, STATE), 0.01)
    log_dt = jax.random.uniform(ks[12], (DEPTH, N_DIR, N_GROUPS), f32,
                                np.log(DT_MIN).astype(np.float32), np.log(DT_MAX).astype(np.float32))
    return {
        "x": nrm(ks[0], (BATCH, SEQ, D_MODEL), 1.0),
        "c": nrm(ks[1], (BATCH, D_MODEL), 1.0),
        "ctx": nrm(ks[2], (BATCH, CTX_LEN, D_MODEL), 1.0),
        "c_ctx": nrm(ks[3], (D_MODEL,), 1.0),
        "w_mod": nrm(ks[4], (DEPTH, D_MODEL, 6 * D_MODEL), 0.5 * D_MODEL ** -0.5),
        "b_mod": nrm(ks[5], (DEPTH, 6 * D_MODEL), 0.01),
        "norm1_g": 1.0 + nrm(ks[6], (DEPTH, D_MODEL), 0.02),
        "norm2_g": 1.0 + nrm(ks[7], (DEPTH, D_MODEL), 0.02),
        "w_in": nrm(ks[8], (DEPTH, D_MODEL, D_IN_PROJ), D_MODEL ** -0.5),
        "b_in": nrm(ks[9], (DEPTH, D_IN_PROJ), 0.01),
        "conv_w": nrm(ks[13], (DEPTH, CONV_K, D_CONV), CONV_K ** -0.5),
        "w_out_a": nrm(ks[14], (DEPTH, D_CONV, D_MODEL), D_CONV ** -0.5),
        "lam_re": lam_re,
        "lam_im": lam_im,
        "log_dt": log_dt,
        "b_re": nrm(ks[15], (DEPTH, N_DIR, N_GROUPS, STATE, SSM_GROUP), (2 * SSM_GROUP) ** -0.5),
        "b_im": nrm(ks[16], (DEPTH, N_DIR, N_GROUPS, STATE, SSM_GROUP), (2 * SSM_GROUP) ** -0.5),
        "c_re": nrm(ks[17], (DEPTH, N_DIR, N_GROUPS, SSM_GROUP, STATE), (2 * STATE) ** -0.5),
        "c_im": nrm(ks[18], (DEPTH, N_DIR, N_GROUPS, SSM_GROUP, STATE), (2 * STATE) ** -0.5),
        "d_skip": nrm(ks[19], (DEPTH, D_SSM), 1.0),
        "w_glu": nrm(ks[20], (DEPTH, D_SSM, 2 * D_MODEL), D_SSM ** -0.5),
        "b_glu": nrm(ks[21], (DEPTH, 2 * D_MODEL), 0.01),
        "w_o": nrm(ks[22], (DEPTH, D_MODEL, D_MODEL), D_MODEL ** -0.5),
        "w_ff_in": nrm(ks[23], (DEPTH, D_MODEL, 2 * D_FF), D_MODEL ** -0.5),
        "w_ff_out": nrm(ks[24], (DEPTH, D_FF, D_MODEL), D_FF ** -0.5),
        "final_g": 1.0 + nrm(ks[25], (D_MODEL,), 0.02),
    }


def reference(x, c, ctx, c_ctx, w_mod, b_mod, norm1_g, norm2_g, w_in, b_in, conv_w, w_out_a,
              lam_re, lam_im, log_dt, b_re, b_im, c_re, c_im, d_skip, w_glu, b_glu, w_o,
              w_ff_in, w_ff_out, final_g):
    rows = x.shape[1] // GRID_W
    silu_c = jax.nn.silu(c)
    silu_cc = jax.nn.silu(c_ctx)
    for l in range(DEPTH):
        last = l == DEPTH - 1
        lp = {'w_in': w_in[l], 'b_in': b_in[l], 'conv_w': conv_w[l], 'w_out_a': w_out_a[l],
              'lam_re': lam_re[l], 'lam_im': lam_im[l], 'log_dt': log_dt[l],
              'b_re': b_re[l], 'b_im': b_im[l], 'c_re': c_re[l], 'c_im': c_im[l],
              'd_skip': d_skip[l], 'w_glu': w_glu[l], 'b_glu': b_glu[l], 'w_o': w_o[l]}
        sh1, sc1, g1, sh2, sc2, g2 = [m[:, None, :] for m in
                                      jnp.split(silu_c @ w_mod[l] + b_mod[l], 6, axis=-1)]
        n_ctx_mod = 2 if last else 6
        modc = jnp.split(silu_cc @ w_mod[l][:, :n_ctx_mod * D_MODEL]
                         + b_mod[l][:n_ctx_mod * D_MODEL], n_ctx_mod, axis=-1)
        hc = modulate(rmsnorm(ctx, norm1_g[l]), modc[0], modc[1])
        if last:
            u_ctx = hc @ w_in[l][:, U_LO:U_HI] + b_in[l][U_LO:U_HI]
            ctx_states = ssm_states(u_ctx, lp['lam_re'], lp['lam_im'], lp['log_dt'],
                                    lp['b_re'], lp['b_im'], None)
        else:
            out_c, ctx_states = mixer(hc, lp, None, None)
        h0 = final_states(ctx_states)
        h = modulate(rmsnorm(x, norm1_g[l]), sh1, sc1)
        out, _ = mixer(h, lp, rows, h0)
        x = x + g1 * out
        x = x + g2 * swiglu(modulate(rmsnorm(x, norm2_g[l]), sh2, sc2), w_ff_in[l], w_ff_out[l])
        if not last:
            ctx = ctx + modc[2] * out_c
            ctx = ctx + modc[5] * swiglu(modulate(rmsnorm(ctx, norm2_g[l]), modc[3], modc[4]),
                                         w_ff_in[l], w_ff_out[l])
    return rmsnorm(x, final_g)
```

```python
import functools

import numpy as np
import jax
import jax.numpy as jnp
from jax import lax
from jax.experimental import pallas as pl
from jax.experimental.pallas import tpu as pltpu

GRID_W = 64
CONV_K = 3
SSM_GROUP = 16
STATE = 64
N_DIR = 2
RMS_EPS = 1e-6

TOK_TILE = 256
SSM_CHUNK = 32
SSM_GROUP_BLOCK = 4
SUBLANES = 8
VMEM_LIMIT_BYTES = 56 * 1024 * 1024

F32 = jnp.float32
BF16 = jnp.bfloat16


def _const_spec(shape):
    nd = len(shape)
    return pl.BlockSpec(shape, lambda *_: (0,) * nd, pipeline_mode=pl.Buffered(1))


def _norm_mod(x, g, shift, scale):
    ms = jnp.mean(x * x, axis=-1, keepdims=True)
    y = x * lax.rsqrt(ms + RMS_EPS) * g
    return y * (1.0 + scale) + shift


def _mod_vec(mod_ref, row, k, d):
    return mod_ref[pl.ds(row, 1), k * d:(k + 1) * d]


def _mod_kernel(c_ref, w_ref, b_ref, o_ref):
    c = c_ref[...]
    s = c * jax.nn.sigmoid(c)
    o_ref[...] = jnp.dot(s.astype(BF16), w_ref[...].astype(BF16),
                         preferred_element_type=F32) + b_ref[...]


def _mod_all(cvec, w_mod, b_mod):
    depth, d, n = w_mod.shape
    tn = 1536
    return pl.pallas_call(
        _mod_kernel,
        grid=(depth, n // tn),
        in_specs=[pl.BlockSpec((8, d), lambda l, j: (0, 0)),
                  pl.BlockSpec((None, d, tn), lambda l, j: (l, 0, j)),
                  pl.BlockSpec((None, 1, tn), lambda l, j: (l, 0, j))],
        out_specs=pl.BlockSpec((None, 8, tn), lambda l, j: (l, 0, j)),
        out_shape=jax.ShapeDtypeStruct((depth, 8, n), F32),
        compiler_params=pltpu.CompilerParams(
            dimension_semantics=("arbitrary", "arbitrary"), vmem_limit_bytes=VMEM_LIMIT_BYTES),
        name="adaln_mod",
    )(cvec, w_mod, b_mod.reshape(depth, 1, n))


def _uproj_kernel(ctx_tiles, x_ref, mod_ref, g_ref, w_ref, b_ref, o_ref):
    d = x_ref.shape[-1]
    row = jnp.where(pl.program_id(1) < ctx_tiles, 2, pl.program_id(0))
    h = _norm_mod(x_ref[...], g_ref[...], _mod_vec(mod_ref, row, 0, d), _mod_vec(mod_ref, row, 1, d))
    u = jnp.dot(h.astype(BF16), w_ref[...], preferred_element_type=F32) + b_ref[...]
    o_ref[...] = u.astype(o_ref.dtype)


def _uproj(xs, mod, g, w_u, b_u, ctx_tiles):
    bsz, s, d = xs.shape
    n = w_u.shape[1]
    return pl.pallas_call(
        functools.partial(_uproj_kernel, ctx_tiles),
        grid=(bsz, s // TOK_TILE),
        in_specs=[pl.BlockSpec((None, TOK_TILE, d), lambda b, i: (b, i, 0)),
                  _const_spec(mod.shape), _const_spec(g.shape),
                  _const_spec(w_u.shape), _const_spec(b_u.shape)],
        out_specs=pl.BlockSpec((None, TOK_TILE, n), lambda b, i: (b, i, 0)),
        out_shape=jax.ShapeDtypeStruct((bsz, s, n), BF16),
        compiler_params=pltpu.CompilerParams(
            dimension_semantics=("parallel", "parallel"), vmem_limit_bytes=VMEM_LIMIT_BYTES),
        name="ssm_uproj",
    )(xs, mod, g, w_u, b_u)


def _cmul(a_re, a_im, b_re, b_im):
    return a_re * b_re - a_im * b_im, a_re * b_im + a_im * b_re


def _ssm_kernel(bsz, ctx_chunks, u_ref, m_ref, w_ref, v_ref, a_ref, y_ref, x_scr, hf_scr, hb_scr):
    gb, rows, _ = u_ref.shape
    sub = SUBLANES
    steps = rows // sub
    half = STATE * N_DIR

    for g in range(gb):
        x_scr[g] = jnp.dot(u_ref[g], w_ref[g], preferred_element_type=F32)

    is_fwd = lax.broadcasted_iota(jnp.int32, (sub, half), 1) < STATE
    sl = lax.broadcasted_iota(jnp.int32, (sub, half), 0)
    coef = lambda g, r: jnp.broadcast_to(a_ref[g, r:r + 1, :], (sub, half))
    a_re = [coef(g, 0) for g in range(gb)]
    a_im = [coef(g, 1) for g in range(gb)]
    blk = lambda j: pl.ds(pl.multiple_of(j * sub, sub), sub)

    def scan_step(j, carry):
        jb = steps - 1 - j
        new = []
        for g in range(gb):
            h_re, h_im = carry[g]
            x_re = jnp.where(is_fwd, x_scr[g, blk(j), :half], x_scr[g, blk(jb), :half])
            x_im = jnp.where(is_fwd, x_scr[g, blk(j), half:], x_scr[g, blk(jb), half:])
            hf_scr[g, blk(j), :half] = h_re
            hf_scr[g, blk(j), half:] = h_im
            hb_scr[g, blk(jb), :half] = h_re
            hb_scr[g, blk(jb), half:] = h_im
            p_re, p_im = _cmul(a_re[g], a_im[g], h_re, h_im)
            new.append((p_re + x_re, p_im + x_im))
        return tuple(new)

    zero = jnp.zeros((sub, half), F32)
    ends = lax.fori_loop(0, steps, scan_step, tuple((zero, zero) for _ in range(gb)))

    def shift(t):
        return jnp.where(is_fwd, jnp.where(sl < bsz, 0.0, pltpu.roll(t, bsz, 0)),
                         jnp.where(sl >= sub - bsz, 0.0, pltpu.roll(t, sub - bsz, 0)))

    carries = []
    for g in range(gb):
        e_re, e_im = ends[g]
        s_re, s_im = coef(g, 2), coef(g, 3)
        c_re, c_im = zero, zero
        for _ in range(sub // bsz - 1):
            p_re, p_im = _cmul(s_re, s_im, c_re, c_im)
            c_re, c_im = shift(e_re + p_re), shift(e_im + p_im)
        carries.append((c_re, c_im))

    def fix_step(j, carry):
        jb = steps - 1 - j
        new = []
        for g in range(gb):
            d_re, d_im = carry[g]
            hf_scr[g, blk(j), :half] += d_re
            hf_scr[g, blk(j), half:] += d_im
            hb_scr[g, blk(jb), :half] += d_re
            hb_scr[g, blk(jb), half:] += d_im
            new.append(_cmul(a_re[g], a_im[g], d_re, d_im))
        return tuple(new)

    lax.fori_loop(0, steps, fix_step, tuple(carries))

    sl2 = lax.broadcasted_iota(jnp.int32, (sub, 2 * half), 0)
    lane = lax.broadcasted_iota(jnp.int32, (rows, 2 * half), 1)
    take_fwd = (lane % half) < STATE
    for g in range(gb):
        for j in range(ctx_chunks):
            tail = hb_scr[g, (steps - ctx_chunks + j) * sub:(steps - ctx_chunks + j + 1) * sub, :]
            head = hb_scr[g, j * sub:(j + 1) * sub, :]
            hb_scr[g, j * sub:(j + 1) * sub, :] = jnp.where(sl2 < bsz, pltpu.roll(tail, bsz, 0), head)
        h_in = jnp.where(take_fwd, hf_scr[g], hb_scr[g]).astype(BF16)
        y = (jnp.dot(u_ref[g], m_ref[g], preferred_element_type=F32)
             + jnp.dot(h_in, v_ref[g], preferred_element_type=F32))
        y_ref[g] = y.astype(y_ref.dtype)


def _ssm(u_g, m, w, v, a, bsz, ctx_chunks):
    ng, rows, f = u_g.shape
    gb = SSM_GROUP_BLOCK
    ns = w.shape[-1]
    blk = lambda shape: pl.BlockSpec((gb,) + shape, lambda j: (j, 0, 0))
    return pl.pallas_call(
        functools.partial(_ssm_kernel, bsz, ctx_chunks),
        grid=(ng // gb,),
        in_specs=[blk((rows, f)), blk((f, f)), blk((f, ns)), blk((ns, f)), blk((4, ns // 2))],
        out_specs=blk((rows, f)),
        out_shape=jax.ShapeDtypeStruct((ng, rows, f), BF16),
        scratch_shapes=[pltpu.VMEM((gb, rows, ns), F32)] * 3,
        compiler_params=pltpu.CompilerParams(
            dimension_semantics=("parallel",), vmem_limit_bytes=VMEM_LIMIT_BYTES),
        name="ssm_chunked",
    )(u_g, m, w, v, a)


def _cpow(z_re, z_im, n):
    out = None
    while n:
        if n & 1:
            out = (z_re, z_im) if out is None else _cmul(out[0], out[1], z_re, z_im)
        n >>= 1
        if n:
            z_re, z_im = _cmul(z_re, z_im, z_re, z_im)
    return out


def _ssm_operators(lam_re, lam_im, log_dt, b_re, b_im, c_re, c_im, d_skip, seg_steps):
    L = SSM_CHUNK
    hi = lax.Precision.HIGHEST
    lam_re = lam_re.astype(F32)
    lam_im = lam_im.astype(F32)
    dt = jnp.exp(log_dt.astype(F32))[..., None]
    mag = jnp.exp(lam_re * dt)
    a_re = mag * jnp.cos(lam_im * dt)
    a_im = mag * jnp.sin(lam_im * dt)
    nr, ni = a_re - 1.0, a_im
    den = lam_re * lam_re + lam_im * lam_im
    f_re = (nr * lam_re + ni * lam_im) / den
    f_im = (ni * lam_re - nr * lam_im) / den
    br, bi = b_re.astype(F32), b_im.astype(F32)
    bb_re = f_re[..., None] * br - f_im[..., None] * bi
    bb_im = f_re[..., None] * bi + f_im[..., None] * br
    k = jnp.arange(L + 1, dtype=F32)[:, None]
    pmag = jnp.exp(lam_re[..., None, :] * dt[..., None, :] * k)
    parg = lam_im[..., None, :] * dt[..., None, :] * k
    p_re = pmag * jnp.cos(parg)
    p_im = pmag * jnp.sin(parg)
    q_re = p_re[..., None] * bb_re[..., None, :, :] - p_im[..., None] * bb_im[..., None, :, :]
    q_im = p_re[..., None] * bb_im[..., None, :, :] + p_im[..., None] * bb_re[..., None, :, :]
    cr, ci = c_re.astype(F32), c_im.astype(F32)
    kern = (jnp.einsum('dzgop,dzgkpi->dzgkio', cr, q_re[..., :L, :, :], precision=hi)
            - jnp.einsum('dzgop,dzgkpi->dzgkio', ci, q_im[..., :L, :, :], precision=hi))
    depth, _, ng = kern.shape[:3]
    H = SSM_GROUP
    kf, kb = kern[:, 0], kern[:, 1]
    dsk = d_skip.astype(F32).reshape(depth, ng, H)
    centre = kf[:, :, 0] + kb[:, :, 0] + dsk[..., None] * jnp.eye(H, dtype=F32)
    kall = jnp.concatenate([kb[:, :, :0:-1], centre[:, :, None], kf[:, :, 1:]], axis=2)
    pad = jnp.concatenate([kall, jnp.zeros_like(kall[:, :, :1])], axis=2)
    tiled = jnp.tile(pad, (1, 1, L, 1, 1))[:, :, :L * (2 * L - 1)]
    toep = tiled.reshape(depth, ng, L, 2 * L - 1, H, H)[:, :, :, L - 1:]
    m = toep.transpose(0, 1, 2, 4, 3, 5).reshape(depth, ng, L * H, L * H)
    def w_part(q, d, rev):
        qq = q[:, d, :, :L]
        if rev:
            qq = qq[:, :, ::-1]
        return qq.transpose(0, 1, 2, 4, 3).reshape(depth, ng, L * H, STATE)
    w = jnp.concatenate([w_part(q_re, 0, True), w_part(q_re, 1, False),
                         w_part(q_im, 0, True), w_part(q_im, 1, False)], axis=-1)
    def v_parts(d, powers):
        pr = p_re[:, d][:, :, powers]
        pi = p_im[:, d][:, :, powers]
        crd, cid = cr[:, d][:, :, None], ci[:, d][:, :, None]
        g_re = crd * pr[..., None, :] - cid * pi[..., None, :]
        g_im = crd * pi[..., None, :] + cid * pr[..., None, :]
        to_rows = lambda z: z.transpose(0, 1, 4, 2, 3).reshape(depth, ng, STATE, L * H)
        return to_rows(g_re), to_rows(-g_im)
    vf_re, vf_im = v_parts(0, np.arange(1, L + 1))
    vb_re, vb_im = v_parts(1, np.arange(L, 0, -1))
    v = jnp.concatenate([vf_re, vb_re, vf_im, vb_im], axis=2)
    ac_re = jnp.concatenate([p_re[:, 0, :, L], p_re[:, 1, :, L]], axis=-1)
    ac_im = jnp.concatenate([p_im[:, 0, :, L], p_im[:, 1, :, L]], axis=-1)
    as_re, as_im = _cpow(ac_re, ac_im, seg_steps)
    a = jnp.stack([ac_re, ac_im, as_re, as_im], axis=2)
    return m.astype(BF16), w.astype(BF16), v.astype(BF16), a


def _gelu_tanh(x):
    return 0.5 * x * (1.0 + jnp.tanh(np.sqrt(2.0 / np.pi).astype(np.float32)
                                     * (x + np.float32(0.044715) * (x * x * x))))


def _mixer_kernel(ctx_tiles, ctx_len, x_ref, ys_ref, mod_ref, g_ref, win_ref, bin_ref, cw_ref,
                  woa_ref, wglu_ref, bglu_ref, wo_ref, o_ref):
    tm, d = x_ref.shape
    dc = woa_ref.shape[0]
    is_ctx = pl.program_id(1) < ctx_tiles
    row = jnp.where(is_ctx, 2, pl.program_id(0))
    x = x_ref[...]
    h = _norm_mod(x, g_ref[...], _mod_vec(mod_ref, row, 0, d), _mod_vec(mod_ref, row, 1, d))
    z = jnp.dot(h.astype(BF16), win_ref[...], preferred_element_type=F32) + bin_ref[...]
    g_b, g_c, x_in = z[:, :dc], z[:, dc:2 * dc], z[:, 2 * dc:3 * dc]
    gate_a, gate_b = z[:, 3 * dc:3 * dc + d], z[:, 3 * dc + d:]
    v = g_c * x_in
    t = lax.broadcasted_iota(jnp.int32, (tm, 1), 0)
    seg = jnp.where(is_ctx, ctx_len - 1, GRID_W - 1)
    pos = t & seg
    v_prev = jnp.where(pos == 0, 0.0, pltpu.roll(v, 1, 0))
    v_next = jnp.where(pos == seg, 0.0, pltpu.roll(v, tm - 1, 0))
    cv = cw_ref[0:1, :] * v_prev + cw_ref[1:2, :] * v + cw_ref[2:3, :] * v_next
    y_a = jnp.dot((g_b * cv).astype(BF16), woa_ref[...], preferred_element_type=F32)
    s = _gelu_tanh(ys_ref[...].astype(F32))
    gl = jnp.dot(s.astype(BF16), wglu_ref[...], preferred_element_type=F32) + bglu_ref[...]
    y_b = gl[:, :d] * jax.nn.sigmoid(gl[:, d:])
    merged = jax.nn.sigmoid(gate_a) * y_a + jax.nn.sigmoid(gate_b) * y_b
    out = jnp.dot(merged.astype(BF16), wo_ref[...], preferred_element_type=F32)
    o_ref[...] = x + _mod_vec(mod_ref, row, 2, d) * out


def _mixer(xs, ys, mod, g, w_in, b_in, conv_w, w_out_a, w_glu, b_glu, w_o, ctx_tiles, ctx_len):
    bsz, s, d = xs.shape
    tok = lambda n: pl.BlockSpec((None, TOK_TILE, n), lambda b, i: (b, i, 0))
    consts = (mod, g, w_in, b_in, conv_w, w_out_a, w_glu, b_glu, w_o)
    return pl.pallas_call(
        functools.partial(_mixer_kernel, ctx_tiles, ctx_len),
        grid=(bsz, s // TOK_TILE),
        in_specs=[tok(d), tok(ys.shape[-1])] + [_const_spec(c.shape) for c in consts],
        out_specs=tok(d),
        out_shape=jax.ShapeDtypeStruct((bsz, s, d), F32),
        compiler_params=pltpu.CompilerParams(
            dimension_semantics=("parallel", "parallel"), vmem_limit_bytes=VMEM_LIMIT_BYTES),
        name="mixer",
    )(xs, ys, *consts)


def _ffn_kernel(ctx_tiles, tile_offset, final, x_ref, mod_ref, g_ref, win_ref, wout_ref, fg_ref, o_ref):
    d = x_ref.shape[-1]
    dff = wout_ref.shape[0]
    row = jnp.where(pl.program_id(1) + tile_offset < ctx_tiles, 2, pl.program_id(0))
    x = x_ref[...]
    h = _norm_mod(x, g_ref[...], _mod_vec(mod_ref, row, 3, d), _mod_vec(mod_ref, row, 4, d))
    z = jnp.dot(h.astype(BF16), win_ref[...], preferred_element_type=F32)
    gate, up = z[:, :dff], z[:, dff:]
    act = gate * jax.nn.sigmoid(gate) * up
    out = jnp.dot(act.astype(BF16), wout_ref[...], preferred_element_type=F32)
    y = x + _mod_vec(mod_ref, row, 5, d) * out
    if final:
        ms = jnp.mean(y * y, axis=-1, keepdims=True)
        y = y * lax.rsqrt(ms + RMS_EPS) * fg_ref[...]
    o_ref[...] = y


def _ffn(xs, mod, g, w_ff_in, w_ff_out, final_g, ctx_tiles, final):
    bsz, s, d = xs.shape
    tile_offset = ctx_tiles if final else 0
    s_out = s - tile_offset * TOK_TILE
    consts = (mod, g, w_ff_in, w_ff_out, final_g)
    return pl.pallas_call(
        functools.partial(_ffn_kernel, ctx_tiles, tile_offset, final),
        grid=(bsz, s_out // TOK_TILE),
        in_specs=[pl.BlockSpec((None, TOK_TILE, d), lambda b, i: (b, i + tile_offset, 0))]
        + [_const_spec(c.shape) for c in consts],
        out_specs=pl.BlockSpec((None, TOK_TILE, d), lambda b, i: (b, i, 0)),
        out_shape=jax.ShapeDtypeStruct((bsz, s_out, d), F32),
        compiler_params=pltpu.CompilerParams(
            dimension_semantics=("parallel", "parallel"), vmem_limit_bytes=VMEM_LIMIT_BYTES),
        name="ffn_final" if final else "ffn",
    )(xs, *consts)


def _to_group_rows(u, n_groups, ctx_len):
    bsz, s, _ = u.shape
    L = SSM_CHUNK
    nseg = SUBLANES // bsz
    u = jnp.concatenate([u, u[:, :ctx_len]], axis=1)
    steps = (s + ctx_len) // (L * nseg)
    u = u.reshape(bsz, nseg, steps, L, n_groups, SSM_GROUP).transpose(4, 2, 1, 0, 3, 5)
    return u.reshape(n_groups, steps * nseg * bsz, L * SSM_GROUP)


def _from_group_rows(y, bsz, s):
    n_groups, rows, _ = y.shape
    L = SSM_CHUNK
    nseg = SUBLANES // bsz
    steps = rows // SUBLANES
    y = y.reshape(n_groups, steps, nseg, bsz, L, SSM_GROUP).transpose(3, 2, 1, 4, 0, 5)
    return y.reshape(bsz, nseg * steps * L, n_groups * SSM_GROUP)[:, :s]


def kernel(x, c, ctx, c_ctx, w_mod, b_mod, norm1_g, norm2_g, w_in, b_in, conv_w, w_out_a, lam_re, lam_im,
           log_dt, b_re, b_im, c_re, c_im, d_skip, w_glu, b_glu, w_o, w_ff_in, w_ff_out, final_g):
    bsz, seq, d = x.shape
    depth = w_mod.shape[0]
    ctx_len = ctx.shape[1]
    d_conv = conv_w.shape[-1]
    d_ssm = d_skip.shape[-1]
    n_groups = d_ssm // SSM_GROUP
    u_lo, u_hi = 3 * d_conv, 3 * d_conv + d_ssm
    assert ctx_len == TOK_TILE and seq % TOK_TILE == 0 and TOK_TILE % GRID_W == 0
    assert TOK_TILE % SSM_CHUNK == 0 and n_groups % SSM_GROUP_BLOCK == 0 and SUBLANES % bsz == 0
    n_seg = SUBLANES // bsz
    assert (seq + 2 * ctx_len) % (SSM_CHUNK * n_seg) == 0
    seg_steps = (seq + 2 * ctx_len) // (SSM_CHUNK * n_seg)
    assert ctx_len // SSM_CHUNK <= seg_steps
    ctx_tiles = ctx_len // TOK_TILE
    ctx_chunks = ctx_len // SSM_CHUNK

    cvec = jnp.zeros((8, d), F32).at[:bsz].set(c.astype(F32)).at[2].set(c_ctx.astype(F32))
    mod = _mod_all(cvec, w_mod, b_mod)

    m_op, w_op, v_op, a_op = _ssm_operators(lam_re, lam_im, log_dt, b_re, b_im, c_re, c_im, d_skip,
                                            seg_steps)

    w_u = w_in[:, :, u_lo:u_hi].astype(BF16)
    b_u = b_in[:, None, u_lo:u_hi]
    w_rest = jnp.concatenate([w_in[:, :, :u_lo], w_in[:, :, u_hi:]], axis=-1).astype(BF16)
    b_rest = jnp.concatenate([b_in[:, None, :u_lo], b_in[:, None, u_hi:]], axis=-1)
    w_out_a, w_glu, w_o = w_out_a.astype(BF16), w_glu.astype(BF16), w_o.astype(BF16)
    w_ff_in, w_ff_out = w_ff_in.astype(BF16), w_ff_out.astype(BF16)
    fg = final_g.reshape(1, d)

    xs = jnp.concatenate([ctx, x], axis=1)
    for l in range(depth):
        g1, g2 = norm1_g[l].reshape(1, d), norm2_g[l].reshape(1, d)
        u = _uproj(xs, mod[l], g1, w_u[l], b_u[l], ctx_tiles)
        y = _ssm(_to_group_rows(u, n_groups, ctx_len), m_op[l], w_op[l], v_op[l], a_op[l], bsz, ctx_chunks)
        ys = _from_group_rows(y, bsz, ctx_len + seq)
        xs = _mixer(xs, ys, mod[l], g1, w_rest[l], b_rest[l], conv_w[l], w_out_a[l], w_glu[l],
                    b_glu[l, None], w_o[l], ctx_tiles, ctx_len)
        xs = _ffn(xs, mod[l], g2, w_ff_in[l], w_ff_out[l], fg, ctx_tiles, l == depth - 1)
    return xs
```

```python
import functools

import numpy as np
import jax
import jax.numpy as jnp
from jax import lax
from jax.experimental import pallas as pl
from jax.experimental.pallas import tpu as pltpu

GRID_W = 64
CONV_K = 3
SSM_GROUP = 16
STATE = 64
N_DIR = 2
RMS_EPS = 1e-6

TOK_TILE = 256
SSM_CHUNK = 32
SSM_GROUP_BLOCK = 4
SUBLANES = 8
VMEM_LIMIT_BYTES = 56 * 1024 * 1024

F32 = jnp.float32
BF16 = jnp.bfloat16


def _const_spec(shape):
    nd = len(shape)
    return pl.BlockSpec(shape, lambda *_: (0,) * nd, pipeline_mode=pl.Buffered(1))


def _norm_mod(x, g, shift, scale):
    ms = jnp.mean(x * x, axis=-1, keepdims=True)
    y = x * lax.rsqrt(ms + RMS_EPS) * g
    return y * (1.0 + scale) + shift


def _mod_vec(mod_ref, row, k, d):
    return mod_ref[pl.ds(row, 1), k * d:(k + 1) * d]


def _mod_kernel(c_ref, w_ref, b_ref, o_ref):
    c = c_ref[...]
    s = c * jax.nn.sigmoid(c)
    o_ref[...] = jnp.dot(s.astype(BF16), w_ref[...].astype(BF16),
                         preferred_element_type=F32) + b_ref[...]


def _mod_all(cvec, w_mod, b_mod):
    depth, d, n = w_mod.shape
    tn = 1536
    return pl.pallas_call(
        _mod_kernel,
        grid=(depth, n // tn),
        in_specs=[pl.BlockSpec((8, d), lambda l, j: (0, 0)),
                  pl.BlockSpec((None, d, tn), lambda l, j: (l, 0, j)),
                  pl.BlockSpec((None, 1, tn), lambda l, j: (l, 0, j))],
        out_specs=pl.BlockSpec((None, 8, tn), lambda l, j: (l, 0, j)),
        out_shape=jax.ShapeDtypeStruct((depth, 8, n), F32),
        compiler_params=pltpu.CompilerParams(
            dimension_semantics=("arbitrary", "arbitrary"), vmem_limit_bytes=VMEM_LIMIT_BYTES),
        name="adaln_mod",
    )(cvec, w_mod, b_mod.reshape(depth, 1, n))


def _uproj_kernel(ctx_tiles, x_ref, mod_ref, g_ref, w_ref, b_ref, o_ref):
    d = x_ref.shape[-1]
    row = jnp.where(pl.program_id(1) < ctx_tiles, 2, pl.program_id(0))
    h = _norm_mod(x_ref[...], g_ref[...], _mod_vec(mod_ref, row, 0, d), _mod_vec(mod_ref, row, 1, d))
    u = jnp.dot(h.astype(BF16), w_ref[...], preferred_element_type=F32) + b_ref[...]
    o_ref[...] = u.astype(o_ref.dtype)


def _uproj(xs, mod, g, w_u, b_u, ctx_tiles):
    bsz, s, d = xs.shape
    n = w_u.shape[1]
    return pl.pallas_call(
        functools.partial(_uproj_kernel, ctx_tiles),
        grid=(bsz, s // TOK_TILE),
        in_specs=[pl.BlockSpec((None, TOK_TILE, d), lambda b, i: (b, i, 0)),
                  _const_spec(mod.shape), _const_spec(g.shape),
                  _const_spec(w_u.shape), _const_spec(b_u.shape)],
        out_specs=pl.BlockSpec((None, TOK_TILE, n), lambda b, i: (b, i, 0)),
        out_shape=jax.ShapeDtypeStruct((bsz, s, n), BF16),
        compiler_params=pltpu.CompilerParams(
            dimension_semantics=("parallel", "parallel"), vmem_limit_bytes=VMEM_LIMIT_BYTES),
        name="ssm_uproj",
    )(xs, mod, g, w_u, b_u)


def _cmul(a_re, a_im, b_re, b_im):
    return a_re * b_re - a_im * b_im, a_re * b_im + a_im * b_re


def _ssm_kernel(bsz, ctx_chunks, u_ref, m_ref, w_ref, v_ref, a_ref, y_ref, x_scr, hf_scr, hb_scr):
    gb, rows, _ = u_ref.shape
    sub = SUBLANES
    steps = rows // sub
    half = STATE * N_DIR

    for g in range(gb):
        x_scr[g] = jnp.dot(u_ref[g], w_ref[g], preferred_element_type=F32)

    is_fwd = lax.broadcasted_iota(jnp.int32, (sub, half), 1) < STATE
    sl = lax.broadcasted_iota(jnp.int32, (sub, half), 0)
    coef = lambda g, r: jnp.broadcast_to(a_ref[g, r:r + 1, :], (sub, half))
    a_re = [coef(g, 0) for g in range(gb)]
    a_im = [coef(g, 1) for g in range(gb)]
    blk = lambda j: pl.ds(pl.multiple_of(j * sub, sub), sub)

    def scan_step(j, carry):
        jb = steps - 1 - j
        new = []
        for g in range(gb):
            h_re, h_im = carry[g]
            x_re = jnp.where(is_fwd, x_scr[g, blk(j), :half], x_scr[g, blk(jb), :half])
            x_im = jnp.where(is_fwd, x_scr[g, blk(j), half:], x_scr[g, blk(jb), half:])
            hf_scr[g, blk(j), :half] = h_re
            hf_scr[g, blk(j), half:] = h_im
            hb_scr[g, blk(jb), :half] = h_re
            hb_scr[g, blk(jb), half:] = h_im
            p_re, p_im = _cmul(a_re[g], a_im[g], h_re, h_im)
            new.append((p_re + x_re, p_im + x_im))
        return tuple(new)

    zero = jnp.zeros((sub, half), F32)
    ends = lax.fori_loop(0, steps, scan_step, tuple((zero, zero) for _ in range(gb)))

    def shift(t):
        return jnp.where(is_fwd, jnp.where(sl < bsz, 0.0, pltpu.roll(t, bsz, 0)),
                         jnp.where(sl >= sub - bsz, 0.0, pltpu.roll(t, sub - bsz, 0)))

    carries = []
    for g in range(gb):
        e_re, e_im = ends[g]
        s_re, s_im = coef(g, 2), coef(g, 3)
        c_re, c_im = zero, zero
        for _ in range(sub // bsz - 1):
            p_re, p_im = _cmul(s_re, s_im, c_re, c_im)
            c_re, c_im = shift(e_re + p_re), shift(e_im + p_im)
        carries.append((c_re, c_im))

    def fix_step(j, carry):
        jb = steps - 1 - j
        new = []
        for g in range(gb):
            d_re, d_im = carry[g]
            hf_scr[g, blk(j), :half] += d_re
            hf_scr[g, blk(j), half:] += d_im
            hb_scr[g, blk(jb), :half] += d_re
            hb_scr[g, blk(jb), half:] += d_im
            new.append(_cmul(a_re[g], a_im[g], d_re, d_im))
        return tuple(new)

    lax.fori_loop(0, steps, fix_step, tuple(carries))

    sl2 = lax.broadcasted_iota(jnp.int32, (sub, 2 * half), 0)
    lane = lax.broadcasted_iota(jnp.int32, (rows, 2 * half), 1)
    take_fwd = (lane % half) < STATE
    for g in range(gb):
        for j in range(ctx_chunks):
            tail = hb_scr[g, (steps - ctx_chunks + j) * sub:(steps - ctx_chunks + j + 1) * sub, :]
            head = hb_scr[g, j * sub:(j + 1) * sub, :]
            hb_scr[g, j * sub:(j + 1) * sub, :] = jnp.where(sl2 < bsz, pltpu.roll(tail, bsz, 0), head)
        h_in = jnp.where(take_fwd, hf_scr[g], hb_scr[g]).astype(BF16)
        y = (jnp.dot(u_ref[g], m_ref[g], preferred_element_type=F32)
             + jnp.dot(h_in, v_ref[g], preferred_element_type=F32))
        y_ref[g] = y.astype(y_ref.dtype)


def _ssm(u_g, m, w, v, a, bsz, ctx_chunks):
    ng, rows, f = u_g.shape
    gb = SSM_GROUP_BLOCK
    ns = w.shape[-1]
    blk = lambda shape: pl.BlockSpec((gb,) + shape, lambda j: (j, 0, 0))
    return pl.pallas_call(
        functools.partial(_ssm_kernel, bsz, ctx_chunks),
        grid=(ng // gb,),
        in_specs=[blk((rows, f)), blk((f, f)), blk((f, ns)), blk((ns, f)), blk((4, ns // 2))],
        out_specs=blk((rows, f)),
        out_shape=jax.ShapeDtypeStruct((ng, rows, f), BF16),
        scratch_shapes=[pltpu.VMEM((gb, rows, ns), F32)] * 3,
        compiler_params=pltpu.CompilerParams(
            dimension_semantics=("parallel",), vmem_limit_bytes=VMEM_LIMIT_BYTES),
        name="ssm_chunked",
    )(u_g, m, w, v, a)


def _toeplitz_kernel(k_ref, m_ref):
    gb, h, width = k_ref.shape
    f = m_ref.shape[-1]
    for g in range(gb):
        k = k_ref[g]
        for t in range(f // h):
            off = f - (t + 1) * h
            win = k if off == 0 else pltpu.roll(k, width - off, 1)
            m_ref[g, t * h:(t + 1) * h, :] = win[:, :f].astype(m_ref.dtype)


def _toeplitz_expand(kcat):
    depth, ng, h, width = kcat.shape
    f = width // 2
    gb = 8
    return pl.pallas_call(
        _toeplitz_kernel,
        grid=(depth, ng // gb),
        in_specs=[pl.BlockSpec((None, gb, h, width), lambda l, j: (l, j, 0, 0))],
        out_specs=pl.BlockSpec((None, gb, f, f), lambda l, j: (l, j, 0, 0)),
        out_shape=jax.ShapeDtypeStruct((depth, ng, f, f), BF16),
        compiler_params=pltpu.CompilerParams(
            dimension_semantics=("parallel", "parallel"), vmem_limit_bytes=VMEM_LIMIT_BYTES),
        name="ssm_toeplitz",
    )(kcat)


def _cpow(z_re, z_im, n):
    out = None
    while n:
        if n & 1:
            out = (z_re, z_im) if out is None else _cmul(out[0], out[1], z_re, z_im)
        n >>= 1
        if n:
            z_re, z_im = _cmul(z_re, z_im, z_re, z_im)
    return out


def _ssm_operators(lam_re, lam_im, log_dt, b_re, b_im, c_re, c_im, d_skip, seg_steps):
    L = SSM_CHUNK
    hi = lax.Precision.HIGHEST
    lam_re = lam_re.astype(F32)
    lam_im = lam_im.astype(F32)
    dt = jnp.exp(log_dt.astype(F32))[..., None]
    mag = jnp.exp(lam_re * dt)
    a_re = mag * jnp.cos(lam_im * dt)
    a_im = mag * jnp.sin(lam_im * dt)
    nr, ni = a_re - 1.0, a_im
    den = lam_re * lam_re + lam_im * lam_im
    f_re = (nr * lam_re + ni * lam_im) / den
    f_im = (ni * lam_re - nr * lam_im) / den
    br, bi = b_re.astype(F32), b_im.astype(F32)
    bb_re = f_re[..., None] * br - f_im[..., None] * bi
    bb_im = f_re[..., None] * bi + f_im[..., None] * br
    k = jnp.arange(L + 1, dtype=F32)[:, None]
    pmag = jnp.exp(lam_re[..., None, :] * dt[..., None, :] * k)
    parg = lam_im[..., None, :] * dt[..., None, :] * k
    p_re = pmag * jnp.cos(parg)
    p_im = pmag * jnp.sin(parg)
    q_re = p_re[..., None] * bb_re[..., None, :, :] - p_im[..., None] * bb_im[..., None, :, :]
    q_im = p_re[..., None] * bb_im[..., None, :, :] + p_im[..., None] * bb_re[..., None, :, :]
    cr, ci = c_re.astype(F32), c_im.astype(F32)
    kern = (jnp.einsum('dzgop,dzgkpi->dzgkio', cr, q_re[..., :L, :, :], precision=hi)
            - jnp.einsum('dzgop,dzgkpi->dzgkio', ci, q_im[..., :L, :, :], precision=hi))
    depth, _, ng = kern.shape[:3]
    H = SSM_GROUP
    kf, kb = kern[:, 0], kern[:, 1]
    dsk = d_skip.astype(F32).reshape(depth, ng, H)
    centre = kf[:, :, 0] + kb[:, :, 0] + dsk[..., None] * jnp.eye(H, dtype=F32)
    kall = jnp.concatenate([kb[:, :, :0:-1], centre[:, :, None], kf[:, :, 1:]], axis=2)
    kcat = kall.transpose(0, 1, 3, 2, 4).reshape(depth, ng, H, (2 * L - 1) * H)
    m = _toeplitz_expand(jnp.pad(kcat, ((0, 0), (0, 0), (0, 0), (0, H))))
    def w_part(q, d, rev):
        qq = q[:, d, :, :L]
        if rev:
            qq = qq[:, :, ::-1]
        return qq.transpose(0, 1, 2, 4, 3).reshape(depth, ng, L * H, STATE)
    w = jnp.concatenate([w_part(q_re, 0, True), w_part(q_re, 1, False),
                         w_part(q_im, 0, True), w_part(q_im, 1, False)], axis=-1)
    def v_parts(d, powers):
        pr = p_re[:, d][:, :, powers]
        pi = p_im[:, d][:, :, powers]
        crd, cid = cr[:, d][:, :, None], ci[:, d][:, :, None]
        g_re = crd * pr[..., None, :] - cid * pi[..., None, :]
        g_im = crd * pi[..., None, :] + cid * pr[..., None, :]
        to_rows = lambda z: z.transpose(0, 1, 4, 2, 3).reshape(depth, ng, STATE, L * H)
        return to_rows(g_re), to_rows(-g_im)
    vf_re, vf_im = v_parts(0, np.arange(1, L + 1))
    vb_re, vb_im = v_parts(1, np.arange(L, 0, -1))
    v = jnp.concatenate([vf_re, vb_re, vf_im, vb_im], axis=2)
    ac_re = jnp.concatenate([p_re[:, 0, :, L], p_re[:, 1, :, L]], axis=-1)
    ac_im = jnp.concatenate([p_im[:, 0, :, L], p_im[:, 1, :, L]], axis=-1)
    as_re, as_im = _cpow(ac_re, ac_im, seg_steps)
    a = jnp.stack([ac_re, ac_im, as_re, as_im], axis=2)
    return m.astype(BF16), w.astype(BF16), v.astype(BF16), a


def _gelu_tanh(x):
    return 0.5 * x * (1.0 + jnp.tanh(np.sqrt(2.0 / np.pi).astype(np.float32)
                                     * (x + np.float32(0.044715) * (x * x * x))))


def _mixer_kernel(ctx_tiles, ctx_len, x_ref, ys_ref, mod_ref, g_ref, win_ref, bin_ref, cw_ref,
                  woa_ref, wglu_ref, bglu_ref, wo_ref, o_ref):
    tm, d = x_ref.shape
    dc = woa_ref.shape[0]
    is_ctx = pl.program_id(1) < ctx_tiles
    row = jnp.where(is_ctx, 2, pl.program_id(0))
    x = x_ref[...]
    h = _norm_mod(x, g_ref[...], _mod_vec(mod_ref, row, 0, d), _mod_vec(mod_ref, row, 1, d))
    z = jnp.dot(h.astype(BF16), win_ref[...], preferred_element_type=F32) + bin_ref[...]
    g_b, g_c, x_in = z[:, :dc], z[:, dc:2 * dc], z[:, 2 * dc:3 * dc]
    gate_a, gate_b = z[:, 3 * dc:3 * dc + d], z[:, 3 * dc + d:]
    v = g_c * x_in
    t = lax.broadcasted_iota(jnp.int32, (tm, 1), 0)
    seg = jnp.where(is_ctx, ctx_len - 1, GRID_W - 1)
    pos = t & seg
    v_prev = jnp.where(pos == 0, 0.0, pltpu.roll(v, 1, 0))
    v_next = jnp.where(pos == seg, 0.0, pltpu.roll(v, tm - 1, 0))
    cv = cw_ref[0:1, :] * v_prev + cw_ref[1:2, :] * v + cw_ref[2:3, :] * v_next
    y_a = jnp.dot((g_b * cv).astype(BF16), woa_ref[...], preferred_element_type=F32)
    s = _gelu_tanh(ys_ref[...].astype(F32))
    gl = jnp.dot(s.astype(BF16), wglu_ref[...], preferred_element_type=F32) + bglu_ref[...]
    y_b = gl[:, :d] * jax.nn.sigmoid(gl[:, d:])
    merged = jax.nn.sigmoid(gate_a) * y_a + jax.nn.sigmoid(gate_b) * y_b
    out = jnp.dot(merged.astype(BF16), wo_ref[...], preferred_element_type=F32)
    o_ref[...] = x + _mod_vec(mod_ref, row, 2, d) * out


def _mixer(xs, ys, mod, g, w_in, b_in, conv_w, w_out_a, w_glu, b_glu, w_o, ctx_tiles, ctx_len):
    bsz, s, d = xs.shape
    tok = lambda n: pl.BlockSpec((None, TOK_TILE, n), lambda b, i: (b, i, 0))
    consts = (mod, g, w_in, b_in, conv_w, w_out_a, w_glu, b_glu, w_o)
    return pl.pallas_call(
        functools.partial(_mixer_kernel, ctx_tiles, ctx_len),
        grid=(bsz, s // TOK_TILE),
        in_specs=[tok(d), tok(ys.shape[-1])] + [_const_spec(c.shape) for c in consts],
        out_specs=tok(d),
        out_shape=jax.ShapeDtypeStruct((bsz, s, d), F32),
        compiler_params=pltpu.CompilerParams(
            dimension_semantics=("parallel", "parallel"), vmem_limit_bytes=VMEM_LIMIT_BYTES),
        name="mixer",
    )(xs, ys, *consts)


def _ffn_kernel(ctx_tiles, tile_offset, final, x_ref, mod_ref, g_ref, win_ref, wout_ref, fg_ref, o_ref):
    d = x_ref.shape[-1]
    dff = wout_ref.shape[0]
    row = jnp.where(pl.program_id(1) + tile_offset < ctx_tiles, 2, pl.program_id(0))
    x = x_ref[...]
    h = _norm_mod(x, g_ref[...], _mod_vec(mod_ref, row, 3, d), _mod_vec(mod_ref, row, 4, d))
    z = jnp.dot(h.astype(BF16), win_ref[...], preferred_element_type=F32)
    gate, up = z[:, :dff], z[:, dff:]
    act = gate * jax.nn.sigmoid(gate) * up
    out = jnp.dot(act.astype(BF16), wout_ref[...], preferred_element_type=F32)
    y = x + _mod_vec(mod_ref, row, 5, d) * out
    if final:
        ms = jnp.mean(y * y, axis=-1, keepdims=True)
        y = y * lax.rsqrt(ms + RMS_EPS) * fg_ref[...]
    o_ref[...] = y


def _ffn(xs, mod, g, w_ff_in, w_ff_out, final_g, ctx_tiles, final):
    bsz, s, d = xs.shape
    tile_offset = ctx_tiles if final else 0
    s_out = s - tile_offset * TOK_TILE
    consts = (mod, g, w_ff_in, w_ff_out, final_g)
    return pl.pallas_call(
        functools.partial(_ffn_kernel, ctx_tiles, tile_offset, final),
        grid=(bsz, s_out // TOK_TILE),
        in_specs=[pl.BlockSpec((None, TOK_TILE, d), lambda b, i: (b, i + tile_offset, 0))]
        + [_const_spec(c.shape) for c in consts],
        out_specs=pl.BlockSpec((None, TOK_TILE, d), lambda b, i: (b, i, 0)),
        out_shape=jax.ShapeDtypeStruct((bsz, s_out, d), F32),
        compiler_params=pltpu.CompilerParams(
            dimension_semantics=("parallel", "parallel"), vmem_limit_bytes=VMEM_LIMIT_BYTES),
        name="ffn_final" if final else "ffn",
    )(xs, *consts)


def _to_group_rows(u, n_groups, ctx_len):
    bsz, s, _ = u.shape
    L = SSM_CHUNK
    nseg = SUBLANES // bsz
    u = jnp.concatenate([u, u[:, :ctx_len]], axis=1)
    steps = (s + ctx_len) // (L * nseg)
    u = u.reshape(bsz, nseg, steps, L, n_groups, SSM_GROUP).transpose(4, 2, 1, 0, 3, 5)
    return u.reshape(n_groups, steps * nseg * bsz, L * SSM_GROUP)


def _from_group_rows(y, bsz, s):
    n_groups, rows, _ = y.shape
    L = SSM_CHUNK
    nseg = SUBLANES // bsz
    steps = rows // SUBLANES
    y = y.reshape(n_groups, steps, nseg, bsz, L, SSM_GROUP).transpose(3, 2, 1, 4, 0, 5)
    return y.reshape(bsz, nseg * steps * L, n_groups * SSM_GROUP)[:, :s]


def kernel(x, c, ctx, c_ctx, w_mod, b_mod, norm1_g, norm2_g, w_in, b_in, conv_w, w_out_a, lam_re, lam_im,
           log_dt, b_re, b_im, c_re, c_im, d_skip, w_glu, b_glu, w_o, w_ff_in, w_ff_out, final_g):
    bsz, seq, d = x.shape
    depth = w_mod.shape[0]
    ctx_len = ctx.shape[1]
    d_conv = conv_w.shape[-1]
    d_ssm = d_skip.shape[-1]
    n_groups = d_ssm // SSM_GROUP
    u_lo, u_hi = 3 * d_conv, 3 * d_conv + d_ssm
    assert ctx_len == TOK_TILE and seq % TOK_TILE == 0 and TOK_TILE % GRID_W == 0
    assert TOK_TILE % SSM_CHUNK == 0 and n_groups % SSM_GROUP_BLOCK == 0 and SUBLANES % bsz == 0
    n_seg = SUBLANES // bsz
    assert (seq + 2 * ctx_len) % (SSM_CHUNK * n_seg) == 0
    seg_steps = (seq + 2 * ctx_len) // (SSM_CHUNK * n_seg)
    assert ctx_len // SSM_CHUNK <= seg_steps
    ctx_tiles = ctx_len // TOK_TILE
    ctx_chunks = ctx_len // SSM_CHUNK

    cvec = jnp.zeros((8, d), F32).at[:bsz].set(c.astype(F32)).at[2].set(c_ctx.astype(F32))
    mod = _mod_all(cvec, w_mod, b_mod)

    m_op, w_op, v_op, a_op = _ssm_operators(lam_re, lam_im, log_dt, b_re, b_im, c_re, c_im, d_skip,
                                            seg_steps)

    w_u = w_in[:, :, u_lo:u_hi].astype(BF16)
    b_u = b_in[:, None, u_lo:u_hi]
    w_rest = jnp.concatenate([w_in[:, :, :u_lo], w_in[:, :, u_hi:]], axis=-1).astype(BF16)
    b_rest = jnp.concatenate([b_in[:, None, :u_lo], b_in[:, None, u_hi:]], axis=-1)
    w_out_a, w_glu, w_o = w_out_a.astype(BF16), w_glu.astype(BF16), w_o.astype(BF16)
    w_ff_in, w_ff_out = w_ff_in.astype(BF16), w_ff_out.astype(BF16)
    fg = final_g.reshape(1, d)

    xs = jnp.concatenate([ctx, x], axis=1)
    for l in range(depth):
        g1, g2 = norm1_g[l].reshape(1, d), norm2_g[l].reshape(1, d)
        u = _uproj(xs, mod[l], g1, w_u[l], b_u[l], ctx_tiles)
        y = _ssm(_to_group_rows(u, n_groups, ctx_len), m_op[l], w_op[l], v_op[l], a_op[l], bsz, ctx_chunks)
        ys = _from_group_rows(y, bsz, ctx_len + seq)
        xs = _mixer(xs, ys, mod[l], g1, w_rest[l], b_rest[l], conv_w[l], w_out_a[l], w_glu[l],
                    b_glu[l, None], w_o[l], ctx_tiles, ctx_len)
        xs = _ffn(xs, mod[l], g2, w_ff_in[l], w_ff_out[l], fg, ctx_tiles, l == depth - 1)
    return xs
```

```python
import functools

import numpy as np
import jax
import jax.numpy as jnp
from jax import lax
from jax.experimental import pallas as pl
from jax.experimental.pallas import tpu as pltpu

GRID_W = 64
CONV_K = 3
SSM_GROUP = 16
STATE = 64
N_DIR = 2
RMS_EPS = 1e-6

TOK_TILE = 256
SSM_CHUNK = 32
CHUNK_PITCH = 40
SUBLANES = 8
LANES = 128
SSM_LANE_GROUPS = LANES // SSM_GROUP
SSM_SCAN_GROUPS = 4
VMEM_LIMIT_BYTES = 56 * 1024 * 1024

F32 = jnp.float32
BF16 = jnp.bfloat16


def _const_spec(shape):
    nd = len(shape)
    return pl.BlockSpec(shape, lambda *_: (0,) * nd, pipeline_mode=pl.Buffered(1))


def _norm_mod(x, g, shift, scale):
    ms = jnp.mean(x * x, axis=-1, keepdims=True)
    y = x * lax.rsqrt(ms + RMS_EPS) * g
    return y * (1.0 + scale) + shift


def _mod_vec(mod_ref, row, k, d):
    return mod_ref[pl.ds(row, 1), k * d:(k + 1) * d]


def _chunk_tile_spec(n, ctx_tiles):
    cpt = TOK_TILE // SSM_CHUNK

    def index(b, i, steps_per_seg_tiles):
        k = jnp.maximum(i - ctx_tiles, 0)
        return (b, k % steps_per_seg_tiles, k // steps_per_seg_tiles, 0, 0)

    return cpt, index


def _mod_kernel(c_ref, w_ref, b_ref, o_ref):
    c = c_ref[...]
    s = c * jax.nn.sigmoid(c)
    o_ref[...] = jnp.dot(s.astype(BF16), w_ref[...].astype(BF16),
                         preferred_element_type=F32) + b_ref[...]


def _mod_all(cvec, w_mod, b_mod):
    depth, d, n = w_mod.shape
    tn = 1536
    return pl.pallas_call(
        _mod_kernel,
        grid=(depth, n // tn),
        in_specs=[pl.BlockSpec((8, d), lambda l, j: (0, 0)),
                  pl.BlockSpec((None, d, tn), lambda l, j: (l, 0, j)),
                  pl.BlockSpec((None, 1, tn), lambda l, j: (l, 0, j))],
        out_specs=pl.BlockSpec((None, 8, tn), lambda l, j: (l, 0, j)),
        out_shape=jax.ShapeDtypeStruct((depth, 8, n), F32),
        compiler_params=pltpu.CompilerParams(
            dimension_semantics=("arbitrary", "arbitrary"), vmem_limit_bytes=VMEM_LIMIT_BYTES),
        name="adaln_mod",
    )(cvec, w_mod, b_mod.reshape(depth, 1, n))


def _emit_ssm_inputs(x, row, is_ctx, mod_ref, g_ref, w_ref, b_ref, ul_ref, uc_ref):
    d = x.shape[-1]
    h = _norm_mod(x, g_ref[...], _mod_vec(mod_ref, row, 0, d), _mod_vec(mod_ref, row, 1, d))
    u = jnp.dot(h.astype(BF16), w_ref[...], preferred_element_type=F32) + b_ref[...]
    cpt, pitch, n = ul_ref.shape
    ul_ref[:, :SSM_CHUNK, :] = u.reshape(cpt, SSM_CHUNK, n)
    ul_ref[:, SSM_CHUNK:, :] = jnp.zeros((cpt, pitch - SSM_CHUNK, n), F32)

    @pl.when(is_ctx)
    def _():
        uc_ref[...] = u


def _uproj_kernel(ctx_tiles, x_ref, mod_ref, g_ref, w_ref, b_ref, ul_ref, uc_ref):
    is_ctx = pl.program_id(1) < ctx_tiles
    row = jnp.where(is_ctx, 2, pl.program_id(0))
    _emit_ssm_inputs(x_ref[...], row, is_ctx, mod_ref, g_ref, w_ref, b_ref, ul_ref, uc_ref)


def _ssm_input_specs(bsz, s, n, ctx_tiles):
    ctx_len = ctx_tiles * TOK_TILE
    steps = (s - ctx_len) // (SSM_CHUNK * SUBLANES)
    cpt, index = _chunk_tile_spec(n, ctx_tiles)
    specs = [pl.BlockSpec((None, cpt, None, CHUNK_PITCH, n), lambda b, i: index(b, i, steps // cpt)),
             pl.BlockSpec((None, ctx_len, n), lambda b, i: (b, 0, 0))]
    shapes = [jax.ShapeDtypeStruct((bsz, steps, SUBLANES, CHUNK_PITCH, n), F32),
              jax.ShapeDtypeStruct((bsz, ctx_len, n), F32)]
    return specs, shapes


def _uproj(xs, mod, g, w_u, b_u, ctx_tiles):
    bsz, s, d = xs.shape
    out_specs, out_shapes = _ssm_input_specs(bsz, s, w_u.shape[1], ctx_tiles)
    return pl.pallas_call(
        functools.partial(_uproj_kernel, ctx_tiles),
        grid=(bsz, s // TOK_TILE),
        in_specs=[pl.BlockSpec((None, TOK_TILE, d), lambda b, i: (b, i, 0)),
                  _const_spec(mod.shape), _const_spec(g.shape),
                  _const_spec(w_u.shape), _const_spec(b_u.shape)],
        out_specs=out_specs,
        out_shape=out_shapes,
        compiler_params=pltpu.CompilerParams(
            dimension_semantics=("arbitrary", "arbitrary"), vmem_limit_bytes=VMEM_LIMIT_BYTES),
        name="ssm_uproj",
    )(xs, mod, g, w_u, b_u)


def _cmul(a_re, a_im, b_re, b_im):
    return a_re * b_re - a_im * b_im, a_re * b_im + a_im * b_re


def _ssm_kernel(ctx_chunks, ul_ref, uc_ref, mt_ref, wt_ref, vt_ref, a_ref, yl_ref, yc_ref,
                ut_scr, yt_scr, x_scr, hf_scr, hb_scr):
    L = SSM_CHUNK
    H = SSM_GROUP
    sub = SUBLANES
    n_lat = ul_ref.shape[0] // CHUNK_PITCH
    n_all = ut_scr.shape[-1]
    steps = n_lat // sub
    half = STATE * N_DIR
    gl = SSM_LANE_GROUPS
    gs = SSM_SCAN_GROUPS

    pad = jnp.zeros((n_all - n_lat - ctx_chunks, LANES), F32)

    def fill(t, carry):
        lat = ul_ref[pl.ds(t, n_lat, stride=CHUNK_PITCH), :]
        cx = uc_ref[pl.ds(t, ctx_chunks, stride=L), :]
        at = jnp.concatenate([lat, cx, pad], axis=0).T
        for g in range(gl):
            ut_scr[g, pl.ds(pl.multiple_of(t * H, H), H), :] = at[g * H:(g + 1) * H, :].astype(BF16)
        return carry

    lax.fori_loop(0, L, fill, 0)

    is_fwd = lax.broadcasted_iota(jnp.int32, (sub, half), 1) < STATE
    is_fwd1 = is_fwd[:1]
    sl = lax.broadcasted_iota(jnp.int32, (sub, half), 0)
    blk = lambda j: pl.ds(pl.multiple_of(j * sub, sub), sub)
    zero = jnp.zeros((sub, half), F32)
    lane2 = lax.broadcasted_iota(jnp.int32, (n_all, 2 * half), 1)
    take_fwd = (lane2 % half) < STATE
    tail0 = n_lat + ctx_chunks
    assert tail0 % sub == 0 and ctx_chunks == sub

    for g0 in range(0, gl, gs):
        coef = lambda g, r, rows=sub: jnp.broadcast_to(a_ref[g0 + g, r:r + 1, :], (rows, half))
        a_re = [coef(g, 0) for g in range(gs)]
        a_im = [coef(g, 1) for g in range(gs)]

        for g in range(gs):
            xt = jnp.dot(wt_ref[g0 + g], ut_scr[g0 + g], preferred_element_type=F32)
            x_scr[g] = xt.T
            hf_scr[g, tail0:, :] = jnp.zeros((n_all - tail0, 2 * half), F32)
            hb_scr[g, tail0:, :] = jnp.zeros((n_all - tail0, 2 * half), F32)

        h0 = []
        for g in range(gs):
            xc = x_scr[g, n_lat:n_lat + ctx_chunks, :]
            ar, ai = a_re[g][:1], a_im[g][:1]
            h_re = h_im = jnp.zeros((1, half), F32)
            hf_rows, hb_rows = [], [None] * ctx_chunks
            for k in range(ctx_chunks):
                kb = ctx_chunks - 1 - k
                hf_rows.append((h_re, h_im))
                hb_rows[kb] = (h_re, h_im)
                x_re = jnp.where(is_fwd1, xc[k:k + 1, :half], xc[kb:kb + 1, :half])
                x_im = jnp.where(is_fwd1, xc[k:k + 1, half:], xc[kb:kb + 1, half:])
                p_re, p_im = _cmul(ar, ai, h_re, h_im)
                h_re, h_im = p_re + x_re, p_im + x_im
            cat = lambda rows, part: jnp.concatenate([r[part] for r in rows], axis=0)
            hf_scr[g, n_lat:n_lat + ctx_chunks, :half] = cat(hf_rows, 0)
            hf_scr[g, n_lat:n_lat + ctx_chunks, half:] = cat(hf_rows, 1)
            hb_scr[g, n_lat:n_lat + ctx_chunks, :half] = cat(hb_rows, 0)
            hb_scr[g, n_lat:n_lat + ctx_chunks, half:] = cat(hb_rows, 1)
            h0.append((jnp.broadcast_to(h_re, (sub, half)), jnp.broadcast_to(h_im, (sub, half))))

        def scan_step(j, carry):
            jb = steps - 1 - j
            new = []
            for g in range(gs):
                h_re, h_im = carry[g]
                x_re = jnp.where(is_fwd, x_scr[g, blk(j), :half], x_scr[g, blk(jb), :half])
                x_im = jnp.where(is_fwd, x_scr[g, blk(j), half:], x_scr[g, blk(jb), half:])
                hf_scr[g, blk(j), :half] = h_re
                hf_scr[g, blk(j), half:] = h_im
                hb_scr[g, blk(jb), :half] = h_re
                hb_scr[g, blk(jb), half:] = h_im
                p_re, p_im = _cmul(a_re[g], a_im[g], h_re, h_im)
                new.append((p_re + x_re, p_im + x_im))
            return tuple(new)

        ends = lax.fori_loop(0, steps, scan_step, tuple((zero, zero) for _ in range(gs)))

        carries = []
        for g in range(gs):
            def shift(t, h):
                return jnp.where(is_fwd, jnp.where(sl < 1, h, pltpu.roll(t, 1, 0)),
                                 jnp.where(sl >= sub - 1, h, pltpu.roll(t, sub - 1, 0)))
            e_re, e_im = ends[g]
            s_re, s_im = coef(g, 2), coef(g, 3)
            c_re, c_im = shift(zero, h0[g][0]), shift(zero, h0[g][1])
            for _ in range(sub - 1):
                p_re, p_im = _cmul(s_re, s_im, c_re, c_im)
                c_re, c_im = shift(e_re + p_re, h0[g][0]), shift(e_im + p_im, h0[g][1])
            carries.append((c_re, c_im))

        def fix_step(j, carry):
            jb = steps - 1 - j
            new = []
            for g in range(gs):
                d_re, d_im = carry[g]
                hf_scr[g, blk(j), :half] += d_re
                hf_scr[g, blk(j), half:] += d_im
                hb_scr[g, blk(jb), :half] += d_re
                hb_scr[g, blk(jb), half:] += d_im
                new.append(_cmul(a_re[g], a_im[g], d_re, d_im))
            return tuple(new)

        lax.fori_loop(0, steps, fix_step, tuple(carries))

        for g in range(gs):
            h_in = jnp.where(take_fwd, hf_scr[g], hb_scr[g]).T.astype(BF16)
            yt = (jnp.dot(mt_ref[g0 + g], ut_scr[g0 + g], preferred_element_type=F32)
                  + jnp.dot(vt_ref[g0 + g], h_in, preferred_element_type=F32))
            for t in range(L):
                yt_scr[t, (g0 + g) * H:(g0 + g + 1) * H, :] = yt[t * H:(t + 1) * H, :].astype(BF16)

    def drain(t, carry):
        y = yt_scr[t].astype(F32).T
        yl_ref[pl.ds(t, n_lat, stride=CHUNK_PITCH), :] = y[:n_lat]
        yc_ref[pl.ds(t, ctx_chunks, stride=L), :] = y[n_lat:n_lat + ctx_chunks]
        return carry

    lax.fori_loop(0, L, drain, 0)
    for t in range(L, CHUNK_PITCH):
        yl_ref[pl.ds(t, n_lat, stride=CHUNK_PITCH), :] = jnp.zeros((n_lat, LANES), F32)


def _ssm(u_lat, u_ctx, mt, wt, vt, a):
    bsz, steps, nseg, pitch, n = u_lat.shape
    L = SSM_CHUNK
    ctx_len = u_ctx.shape[1]
    ctx_chunks = ctx_len // L
    n_lat = steps * nseg
    n_all = -(-(n_lat + ctx_chunks) // LANES) * LANES
    gl = SSM_LANE_GROUPS
    f = L * SSM_GROUP
    ns = wt.shape[1]
    lat_spec = pl.BlockSpec((None, n_lat * pitch, LANES), lambda b, j: (b, 0, j), pipeline_mode=pl.Buffered(1))
    ctx_spec = pl.BlockSpec((None, ctx_len, LANES), lambda b, j: (b, 0, j))
    grp = lambda shape: pl.BlockSpec((gl,) + shape, lambda b, j: (j, 0, 0), pipeline_mode=pl.Buffered(1))
    y_lat, y_ctx = pl.pallas_call(
        functools.partial(_ssm_kernel, ctx_chunks),
        grid=(bsz, n // LANES),
        in_specs=[lat_spec, ctx_spec, grp((f, f)), grp((ns, f)), grp((f, ns)), grp((4, ns // 2))],
        out_specs=[lat_spec, ctx_spec],
        out_shape=[jax.ShapeDtypeStruct((bsz, n_lat * pitch, n), F32),
                   jax.ShapeDtypeStruct((bsz, ctx_len, n), F32)],
        scratch_shapes=[pltpu.VMEM((gl, f, n_all), BF16), pltpu.VMEM((L, LANES, n_all), BF16)]
        + [pltpu.VMEM((SSM_SCAN_GROUPS, n_all, ns), F32)] * 3,
        compiler_params=pltpu.CompilerParams(
            dimension_semantics=("arbitrary", "arbitrary"), vmem_limit_bytes=VMEM_LIMIT_BYTES),
        name="ssm_chunked",
    )(u_lat.reshape(bsz, n_lat * pitch, n), u_ctx, mt, wt, vt, a)
    return y_lat.reshape(u_lat.shape), y_ctx


def _toeplitz_kernel(k_ref, m_ref):
    gb, h, width = k_ref.shape
    f = m_ref.shape[-1]
    for g in range(gb):
        k = k_ref[g]
        for t in range(f // h):
            off = f - (t + 1) * h
            win = k if off == 0 else pltpu.roll(k, width - off, 1)
            m_ref[g, t * h:(t + 1) * h, :] = win[:, :f].astype(m_ref.dtype)


def _toeplitz_expand(kcat):
    depth, ng, h, width = kcat.shape
    f = width // 2
    gb = 8
    return pl.pallas_call(
        _toeplitz_kernel,
        grid=(depth, ng // gb),
        in_specs=[pl.BlockSpec((None, gb, h, width), lambda l, j: (l, j, 0, 0))],
        out_specs=pl.BlockSpec((None, gb, f, f), lambda l, j: (l, j, 0, 0)),
        out_shape=jax.ShapeDtypeStruct((depth, ng, f, f), BF16),
        compiler_params=pltpu.CompilerParams(
            dimension_semantics=("parallel", "parallel"), vmem_limit_bytes=VMEM_LIMIT_BYTES),
        name="ssm_toeplitz",
    )(kcat)


def _cpow(z_re, z_im, n):
    out = None
    while n:
        if n & 1:
            out = (z_re, z_im) if out is None else _cmul(out[0], out[1], z_re, z_im)
        n >>= 1
        if n:
            z_re, z_im = _cmul(z_re, z_im, z_re, z_im)
    return out


def _ssm_operators(lam_re, lam_im, log_dt, b_re, b_im, c_re, c_im, d_skip, seg_steps):
    L = SSM_CHUNK
    hi = lax.Precision.HIGHEST
    lam_re = lam_re.astype(F32)
    lam_im = lam_im.astype(F32)
    dt = jnp.exp(log_dt.astype(F32))[..., None]
    mag = jnp.exp(lam_re * dt)
    a_re = mag * jnp.cos(lam_im * dt)
    a_im = mag * jnp.sin(lam_im * dt)
    nr, ni = a_re - 1.0, a_im
    den = lam_re * lam_re + lam_im * lam_im
    f_re = (nr * lam_re + ni * lam_im) / den
    f_im = (ni * lam_re - nr * lam_im) / den
    br, bi = b_re.astype(F32), b_im.astype(F32)
    bb_re = f_re[..., None] * br - f_im[..., None] * bi
    bb_im = f_re[..., None] * bi + f_im[..., None] * br
    k = jnp.arange(L + 1, dtype=F32)[:, None]
    pmag = jnp.exp(lam_re[..., None, :] * dt[..., None, :] * k)
    parg = lam_im[..., None, :] * dt[..., None, :] * k
    p_re = pmag * jnp.cos(parg)
    p_im = pmag * jnp.sin(parg)
    q_re = p_re[..., None] * bb_re[..., None, :, :] - p_im[..., None] * bb_im[..., None, :, :]
    q_im = p_re[..., None] * bb_im[..., None, :, :] + p_im[..., None] * bb_re[..., None, :, :]
    cr, ci = c_re.astype(F32), c_im.astype(F32)
    kern = (jnp.einsum('dzgop,dzgkpi->dzgkoi', cr, q_re[..., :L, :, :], precision=hi)
            - jnp.einsum('dzgop,dzgkpi->dzgkoi', ci, q_im[..., :L, :, :], precision=hi))
    depth, _, ng = kern.shape[:3]
    H = SSM_GROUP
    kf, kb = kern[:, 0], kern[:, 1]
    dsk = d_skip.astype(F32).reshape(depth, ng, H)
    centre = kf[:, :, 0] + kb[:, :, 0] + dsk[..., None] * jnp.eye(H, dtype=F32)
    kall = jnp.concatenate([kf[:, :, :0:-1], centre[:, :, None], kb[:, :, 1:]], axis=2)
    kcat = kall.transpose(0, 1, 3, 2, 4).reshape(depth, ng, H, (2 * L - 1) * H)
    mt = _toeplitz_expand(jnp.pad(kcat, ((0, 0), (0, 0), (0, 0), (0, H))))
    def wt_part(q, d, rev):
        qq = q[:, d, :, :L]
        if rev:
            qq = qq[:, :, ::-1]
        return qq.transpose(0, 1, 3, 2, 4).reshape(depth, ng, STATE, L * H)
    wt = jnp.concatenate([wt_part(q_re, 0, True), wt_part(q_re, 1, False),
                          wt_part(q_im, 0, True), wt_part(q_im, 1, False)], axis=2)
    def vt_parts(d, powers):
        pr = p_re[:, d][:, :, powers]
        pi = p_im[:, d][:, :, powers]
        crd, cid = cr[:, d][:, :, None], ci[:, d][:, :, None]
        g_re = crd * pr[..., None, :] - cid * pi[..., None, :]
        g_im = crd * pi[..., None, :] + cid * pr[..., None, :]
        to_cols = lambda z: z.reshape(depth, ng, L * H, STATE)
        return to_cols(g_re), to_cols(-g_im)
    vf_re, vf_im = vt_parts(0, np.arange(1, L + 1))
    vb_re, vb_im = vt_parts(1, np.arange(L, 0, -1))
    vt = jnp.concatenate([vf_re, vb_re, vf_im, vb_im], axis=-1)
    ac_re = jnp.concatenate([p_re[:, 0, :, L], p_re[:, 1, :, L]], axis=-1)
    ac_im = jnp.concatenate([p_im[:, 0, :, L], p_im[:, 1, :, L]], axis=-1)
    as_re, as_im = _cpow(ac_re, ac_im, seg_steps)
    a = jnp.stack([ac_re, ac_im, as_re, as_im], axis=2)
    return mt, wt.astype(BF16), vt.astype(BF16), a


def _gelu_tanh(x):
    return 0.5 * x * (1.0 + jnp.tanh(np.sqrt(2.0 / np.pi).astype(np.float32)
                                     * (x + np.float32(0.044715) * (x * x * x))))


def _mixer_kernel(ctx_tiles, ctx_len, x_ref, ysl_ref, ysc_ref, mod_ref, g_ref, win_ref, bin_ref, cw_ref,
                  woa_ref, wglu_ref, bglu_ref, wo_ref, o_ref):
    tm, d = x_ref.shape
    dc = woa_ref.shape[0]
    is_ctx = pl.program_id(1) < ctx_tiles
    row = jnp.where(is_ctx, 2, pl.program_id(0))
    x = x_ref[...]
    h = _norm_mod(x, g_ref[...], _mod_vec(mod_ref, row, 0, d), _mod_vec(mod_ref, row, 1, d))
    z = jnp.dot(h.astype(BF16), win_ref[...], preferred_element_type=F32) + bin_ref[...]
    g_b, g_c, x_in = z[:, :dc], z[:, dc:2 * dc], z[:, 2 * dc:3 * dc]
    gate_a, gate_b = z[:, 3 * dc:3 * dc + d], z[:, 3 * dc + d:]
    v = g_c * x_in
    t = lax.broadcasted_iota(jnp.int32, (tm, 1), 0)
    seg = jnp.where(is_ctx, ctx_len - 1, GRID_W - 1)
    pos = t & seg
    v_prev = jnp.where(pos == 0, 0.0, pltpu.roll(v, 1, 0))
    v_next = jnp.where(pos == seg, 0.0, pltpu.roll(v, tm - 1, 0))
    cv = cw_ref[0:1, :] * v_prev + cw_ref[1:2, :] * v + cw_ref[2:3, :] * v_next
    y_a = jnp.dot((g_b * cv).astype(BF16), woa_ref[...], preferred_element_type=F32)
    ys = jnp.where(is_ctx, ysc_ref[...], ysl_ref[:, :SSM_CHUNK, :].reshape(tm, ysc_ref.shape[-1]))
    s = _gelu_tanh(ys)
    gl = jnp.dot(s.astype(BF16), wglu_ref[...], preferred_element_type=F32) + bglu_ref[...]
    y_b = gl[:, :d] * jax.nn.sigmoid(gl[:, d:])
    merged = jax.nn.sigmoid(gate_a) * y_a + jax.nn.sigmoid(gate_b) * y_b
    out = jnp.dot(merged.astype(BF16), wo_ref[...], preferred_element_type=F32)
    o_ref[...] = x + _mod_vec(mod_ref, row, 2, d) * out


def _mixer(xs, ys_lat, ys_ctx, mod, g, w_in, b_in, conv_w, w_out_a, w_glu, b_glu, w_o, ctx_tiles, ctx_len):
    bsz, s, d = xs.shape
    n = ys_ctx.shape[-1]
    steps = ys_lat.shape[1]
    cpt, index = _chunk_tile_spec(n, ctx_tiles)
    tok = lambda w: pl.BlockSpec((None, TOK_TILE, w), lambda b, i: (b, i, 0))
    consts = (mod, g, w_in, b_in, conv_w, w_out_a, w_glu, b_glu, w_o)
    return pl.pallas_call(
        functools.partial(_mixer_kernel, ctx_tiles, ctx_len),
        grid=(bsz, s // TOK_TILE),
        in_specs=[tok(d),
                  pl.BlockSpec((None, cpt, None, CHUNK_PITCH, n), lambda b, i: index(b, i, steps // cpt)),
                  pl.BlockSpec((None, ctx_len, n), lambda b, i: (b, 0, 0))]
        + [_const_spec(c.shape) for c in consts],
        out_specs=tok(d),
        out_shape=jax.ShapeDtypeStruct((bsz, s, d), F32),
        compiler_params=pltpu.CompilerParams(
            dimension_semantics=("parallel", "parallel"), vmem_limit_bytes=VMEM_LIMIT_BYTES),
        name="mixer",
    )(xs, ys_lat, ys_ctx, *consts)


def _ffn_kernel(ctx_tiles, tile_offset, final, x_ref, mod_ref, g_ref, win_ref, wout_ref, *rest):
    d = x_ref.shape[-1]
    dff = wout_ref.shape[0]
    is_ctx = pl.program_id(1) + tile_offset < ctx_tiles
    row = jnp.where(is_ctx, 2, pl.program_id(0))
    x = x_ref[...]
    h = _norm_mod(x, g_ref[...], _mod_vec(mod_ref, row, 3, d), _mod_vec(mod_ref, row, 4, d))
    z = jnp.dot(h.astype(BF16), win_ref[...], preferred_element_type=F32)
    gate, up = z[:, :dff], z[:, dff:]
    act = gate * jax.nn.sigmoid(gate) * up
    out = jnp.dot(act.astype(BF16), wout_ref[...], preferred_element_type=F32)
    y = x + _mod_vec(mod_ref, row, 5, d) * out
    if final:
        fg_ref, o_ref = rest
        ms = jnp.mean(y * y, axis=-1, keepdims=True)
        o_ref[...] = y * lax.rsqrt(ms + RMS_EPS) * fg_ref[...]
    else:
        modn_ref, gn_ref, wu_ref, bu_ref, o_ref, ul_ref, uc_ref = rest
        o_ref[...] = y
        _emit_ssm_inputs(y, row, is_ctx, modn_ref, gn_ref, wu_ref, bu_ref, ul_ref, uc_ref)


def _ffn(xs, mod, g, w_ff_in, w_ff_out, ctx_tiles, final_g=None, nxt=None):
    bsz, s, d = xs.shape
    final = final_g is not None
    tile_offset = ctx_tiles if final else 0
    s_out = s - tile_offset * TOK_TILE
    consts = (mod, g, w_ff_in, w_ff_out) + ((final_g,) if final else tuple(nxt))
    out_specs = [pl.BlockSpec((None, TOK_TILE, d), lambda b, i: (b, i, 0))]
    out_shapes = [jax.ShapeDtypeStruct((bsz, s_out, d), F32)]
    if not final:
        u_specs, u_shapes = _ssm_input_specs(bsz, s, nxt[2].shape[1], ctx_tiles)
        out_specs, out_shapes = out_specs + u_specs, out_shapes + u_shapes
    return pl.pallas_call(
        functools.partial(_ffn_kernel, ctx_tiles, tile_offset, final),
        grid=(bsz, s_out // TOK_TILE),
        in_specs=[pl.BlockSpec((None, TOK_TILE, d), lambda b, i: (b, i + tile_offset, 0))]
        + [_const_spec(c.shape) for c in consts],
        out_specs=out_specs,
        out_shape=out_shapes,
        compiler_params=pltpu.CompilerParams(
            dimension_semantics=("arbitrary", "arbitrary"), vmem_limit_bytes=VMEM_LIMIT_BYTES),
        name="ffn_final" if final else "ffn",
    )(xs, *consts)


def kernel(x, c, ctx, c_ctx, w_mod, b_mod, norm1_g, norm2_g, w_in, b_in, conv_w, w_out_a, lam_re, lam_im,
           log_dt, b_re, b_im, c_re, c_im, d_skip, w_glu, b_glu, w_o, w_ff_in, w_ff_out, final_g):
    bsz, seq, d = x.shape
    depth = w_mod.shape[0]
    ctx_len = ctx.shape[1]
    d_conv = conv_w.shape[-1]
    d_ssm = d_skip.shape[-1]
    u_lo, u_hi = 3 * d_conv, 3 * d_conv + d_ssm
    assert ctx_len == TOK_TILE and seq % TOK_TILE == 0 and TOK_TILE % GRID_W == 0
    assert ctx_len == SSM_CHUNK * SUBLANES and d_ssm % LANES == 0 and bsz <= 2
    assert seq % (SUBLANES * TOK_TILE) == 0
    seg_steps = seq // (SSM_CHUNK * SUBLANES)
    ctx_tiles = ctx_len // TOK_TILE

    cvec = jnp.zeros((8, d), F32).at[:bsz].set(c.astype(F32)).at[2].set(c_ctx.astype(F32))
    mod = _mod_all(cvec, w_mod, b_mod)

    mt_op, wt_op, vt_op, a_op = _ssm_operators(lam_re, lam_im, log_dt, b_re, b_im, c_re, c_im, d_skip,
                                               seg_steps)

    w_u = w_in[:, :, u_lo:u_hi].astype(BF16)
    b_u = b_in[:, None, u_lo:u_hi]
    w_rest = jnp.concatenate([w_in[:, :, :u_lo], w_in[:, :, u_hi:]], axis=-1).astype(BF16)
    b_rest = jnp.concatenate([b_in[:, None, :u_lo], b_in[:, None, u_hi:]], axis=-1)
    w_out_a, w_glu, w_o = w_out_a.astype(BF16), w_glu.astype(BF16), w_o.astype(BF16)
    w_ff_in, w_ff_out = w_ff_in.astype(BF16), w_ff_out.astype(BF16)
    fg = final_g.reshape(1, d)

    xs = jnp.concatenate([ctx, x], axis=1)
    ssm_in = lambda l: (mod[l], norm1_g[l].reshape(1, d), w_u[l], b_u[l])
    u_lat, u_ctx = _uproj(xs, *ssm_in(0), ctx_tiles)
    for l in range(depth):
        g1, g2 = norm1_g[l].reshape(1, d), norm2_g[l].reshape(1, d)
        ys_lat, ys_ctx = _ssm(u_lat, u_ctx, mt_op[l], wt_op[l], vt_op[l], a_op[l])
        xs = _mixer(xs, ys_lat, ys_ctx, mod[l], g1, w_rest[l], b_rest[l], conv_w[l], w_out_a[l], w_glu[l],
                    b_glu[l, None], w_o[l], ctx_tiles, ctx_len)
        if l + 1 < depth:
            xs, u_lat, u_ctx = _ffn(xs, mod[l], g2, w_ff_in[l], w_ff_out[l], ctx_tiles, nxt=ssm_in(l + 1))
        else:
            xs, = _ffn(xs, mod[l], g2, w_ff_in[l], w_ff_out[l], ctx_tiles, final_g=fg)
    return xs
```

```python
import functools

import numpy as np
import jax
import jax.numpy as jnp
from jax import lax
from jax.experimental import pallas as pl
from jax.experimental.pallas import tpu as pltpu

GRID_W = 64
CONV_K = 3
SSM_GROUP = 16
STATE = 64
N_DIR = 2
RMS_EPS = 1e-6

TOK_TILE = 256
SSM_CHUNK = 32
CHUNK_PITCH = 40
SUBLANES = 8
LANES = 128
SSM_LANE_GROUPS = LANES // SSM_GROUP
SSM_SCAN_GROUPS = 4
VMEM_LIMIT_BYTES = 56 * 1024 * 1024

F32 = jnp.float32
BF16 = jnp.bfloat16


def _const_spec(shape):
    nd = len(shape)
    return pl.BlockSpec(shape, lambda *_: (0,) * nd, pipeline_mode=pl.Buffered(1))


def _norm_mod(x, g, shift, scale):
    ms = jnp.mean(x * x, axis=-1, keepdims=True)
    y = x * lax.rsqrt(ms + RMS_EPS) * g
    return y * (1.0 + scale) + shift


def _mod_vec(mod_ref, row, k, d):
    return mod_ref[pl.ds(row, 1), k * d:(k + 1) * d]


def _chunk_tile_spec(n, ctx_tiles):
    cpt = TOK_TILE // SSM_CHUNK

    def index(b, i, steps_per_seg_tiles):
        k = jnp.maximum(i - ctx_tiles, 0)
        return (b, k % steps_per_seg_tiles, k // steps_per_seg_tiles, 0, 0)

    return cpt, index


def _mod_kernel(c_ref, w_ref, b_ref, o_ref):
    c = c_ref[...]
    s = c * jax.nn.sigmoid(c)
    o_ref[...] = jnp.dot(s.astype(BF16), w_ref[...].astype(BF16),
                         preferred_element_type=F32) + b_ref[...]


def _mod_all(cvec, w_mod, b_mod):
    depth, d, n = w_mod.shape
    tn = 1536
    return pl.pallas_call(
        _mod_kernel,
        grid=(depth, n // tn),
        in_specs=[pl.BlockSpec((8, d), lambda l, j: (0, 0)),
                  pl.BlockSpec((None, d, tn), lambda l, j: (l, 0, j)),
                  pl.BlockSpec((None, 1, tn), lambda l, j: (l, 0, j))],
        out_specs=pl.BlockSpec((None, 8, tn), lambda l, j: (l, 0, j)),
        out_shape=jax.ShapeDtypeStruct((depth, 8, n), F32),
        compiler_params=pltpu.CompilerParams(
            dimension_semantics=("arbitrary", "arbitrary"), vmem_limit_bytes=VMEM_LIMIT_BYTES),
        name="adaln_mod",
    )(cvec, w_mod, b_mod.reshape(depth, 1, n))


def _emit_ssm_inputs(x, row, is_ctx, mod_ref, g_ref, w_ref, b_ref, ul_ref, uc_ref):
    d = x.shape[-1]
    h = _norm_mod(x, g_ref[...], _mod_vec(mod_ref, row, 0, d), _mod_vec(mod_ref, row, 1, d))
    u = jnp.dot(h.astype(BF16), w_ref[...], preferred_element_type=F32) + b_ref[...]
    cpt, pitch, n = ul_ref.shape
    ul_ref[:, :SSM_CHUNK, :] = u.reshape(cpt, SSM_CHUNK, n)
    ul_ref[:, SSM_CHUNK:, :] = jnp.zeros((cpt, pitch - SSM_CHUNK, n), F32)

    @pl.when(is_ctx)
    def _():
        uc_ref[...] = u


def _uproj_kernel(ctx_tiles, x_ref, mod_ref, g_ref, w_ref, b_ref, ul_ref, uc_ref):
    is_ctx = pl.program_id(1) < ctx_tiles
    row = jnp.where(is_ctx, 2, pl.program_id(0))
    _emit_ssm_inputs(x_ref[...], row, is_ctx, mod_ref, g_ref, w_ref, b_ref, ul_ref, uc_ref)


def _ssm_input_specs(bsz, s, n, ctx_tiles):
    ctx_len = ctx_tiles * TOK_TILE
    steps = (s - ctx_len) // (SSM_CHUNK * SUBLANES)
    cpt, index = _chunk_tile_spec(n, ctx_tiles)
    specs = [pl.BlockSpec((None, cpt, None, CHUNK_PITCH, n), lambda b, i: index(b, i, steps // cpt)),
             pl.BlockSpec((None, ctx_len, n), lambda b, i: (b, 0, 0))]
    shapes = [jax.ShapeDtypeStruct((bsz, steps, SUBLANES, CHUNK_PITCH, n), F32),
              jax.ShapeDtypeStruct((bsz, ctx_len, n), F32)]
    return specs, shapes


def _uproj(xs, mod, g, w_u, b_u, ctx_tiles):
    bsz, s, d = xs.shape
    out_specs, out_shapes = _ssm_input_specs(bsz, s, w_u.shape[1], ctx_tiles)
    return pl.pallas_call(
        functools.partial(_uproj_kernel, ctx_tiles),
        grid=(bsz, s // TOK_TILE),
        in_specs=[pl.BlockSpec((None, TOK_TILE, d), lambda b, i: (b, i, 0)),
                  _const_spec(mod.shape), _const_spec(g.shape),
                  _const_spec(w_u.shape), _const_spec(b_u.shape)],
        out_specs=out_specs,
        out_shape=out_shapes,
        compiler_params=pltpu.CompilerParams(
            dimension_semantics=("arbitrary", "arbitrary"), vmem_limit_bytes=VMEM_LIMIT_BYTES),
        name="ssm_uproj",
    )(xs, mod, g, w_u, b_u)


def _cmul(a_re, a_im, b_re, b_im):
    return a_re * b_re - a_im * b_im, a_re * b_im + a_im * b_re


def _ssm_kernel(ctx_chunks, ul_ref, uc_ref, mt_ref, wt_ref, vt_ref, a_ref, yl_ref, yc_ref,
                ut_scr, yt_scr, x_scr, hf_scr, hb_scr):
    L = SSM_CHUNK
    H = SSM_GROUP
    sub = SUBLANES
    n_lat = ul_ref.shape[0] // CHUNK_PITCH
    n_all = ut_scr.shape[-1]
    steps = n_lat // sub
    half = STATE * N_DIR
    gl = SSM_LANE_GROUPS
    gs = SSM_SCAN_GROUPS

    pad = jnp.zeros((n_all - n_lat - ctx_chunks, LANES), F32)

    def fill(t, carry):
        lat = ul_ref[pl.ds(t, n_lat, stride=CHUNK_PITCH), :]
        cx = uc_ref[pl.ds(t, ctx_chunks, stride=L), :]
        at = jnp.concatenate([lat, cx, pad], axis=0).T
        for g in range(gl):
            ut_scr[g, pl.ds(pl.multiple_of(t * H, H), H), :] = at[g * H:(g + 1) * H, :].astype(BF16)
        return carry

    lax.fori_loop(0, L, fill, 0)

    is_fwd = lax.broadcasted_iota(jnp.int32, (sub, half), 1) < STATE
    is_fwd1 = is_fwd[:1]
    sl = lax.broadcasted_iota(jnp.int32, (sub, half), 0)
    blk = lambda j: pl.ds(pl.multiple_of(j * sub, sub), sub)
    zero = jnp.zeros((sub, half), F32)
    lane2 = lax.broadcasted_iota(jnp.int32, (n_all, 2 * half), 1)
    take_fwd = (lane2 % half) < STATE
    tail0 = n_lat + ctx_chunks
    assert tail0 % sub == 0 and ctx_chunks == sub

    for g0 in range(0, gl, gs):
        coef = lambda g, r, rows=sub: jnp.broadcast_to(a_ref[g0 + g, r:r + 1, :], (rows, half))
        a_re = [coef(g, 0) for g in range(gs)]
        a_im = [coef(g, 1) for g in range(gs)]

        for g in range(gs):
            xt = jnp.dot(wt_ref[g0 + g], ut_scr[g0 + g], preferred_element_type=F32)
            x_scr[g] = xt.T
            hf_scr[g, tail0:, :] = jnp.zeros((n_all - tail0, 2 * half), F32)
            hb_scr[g, tail0:, :] = jnp.zeros((n_all - tail0, 2 * half), F32)

        h0 = []
        for g in range(gs):
            xc = x_scr[g, n_lat:n_lat + ctx_chunks, :]
            ar, ai = a_re[g][:1], a_im[g][:1]
            h_re = h_im = jnp.zeros((1, half), F32)
            hf_rows, hb_rows = [], [None] * ctx_chunks
            for k in range(ctx_chunks):
                kb = ctx_chunks - 1 - k
                hf_rows.append((h_re, h_im))
                hb_rows[kb] = (h_re, h_im)
                x_re = jnp.where(is_fwd1, xc[k:k + 1, :half], xc[kb:kb + 1, :half])
                x_im = jnp.where(is_fwd1, xc[k:k + 1, half:], xc[kb:kb + 1, half:])
                p_re, p_im = _cmul(ar, ai, h_re, h_im)
                h_re, h_im = p_re + x_re, p_im + x_im
            cat = lambda rows, part: jnp.concatenate([r[part] for r in rows], axis=0)
            hf_scr[g, n_lat:n_lat + ctx_chunks, :half] = cat(hf_rows, 0)
            hf_scr[g, n_lat:n_lat + ctx_chunks, half:] = cat(hf_rows, 1)
            hb_scr[g, n_lat:n_lat + ctx_chunks, :half] = cat(hb_rows, 0)
            hb_scr[g, n_lat:n_lat + ctx_chunks, half:] = cat(hb_rows, 1)
            h0.append((jnp.broadcast_to(h_re, (sub, half)), jnp.broadcast_to(h_im, (sub, half))))

        def scan_step(j, carry):
            jb = steps - 1 - j
            new = []
            for g in range(gs):
                h_re, h_im = carry[g]
                x_re = jnp.where(is_fwd, x_scr[g, blk(j), :half], x_scr[g, blk(jb), :half])
                x_im = jnp.where(is_fwd, x_scr[g, blk(j), half:], x_scr[g, blk(jb), half:])
                hf_scr[g, blk(j), :half] = h_re
                hf_scr[g, blk(j), half:] = h_im
                hb_scr[g, blk(jb), :half] = h_re
                hb_scr[g, blk(jb), half:] = h_im
                p_re, p_im = _cmul(a_re[g], a_im[g], h_re, h_im)
                new.append((p_re + x_re, p_im + x_im))
            return tuple(new)

        ends = lax.fori_loop(0, steps, scan_step, tuple((zero, zero) for _ in range(gs)))

        carries = []
        for g in range(gs):
            def shift(t, h):
                return jnp.where(is_fwd, jnp.where(sl < 1, h, pltpu.roll(t, 1, 0)),
                                 jnp.where(sl >= sub - 1, h, pltpu.roll(t, sub - 1, 0)))
            e_re, e_im = ends[g]
            s_re, s_im = coef(g, 2), coef(g, 3)
            c_re, c_im = shift(zero, h0[g][0]), shift(zero, h0[g][1])
            for _ in range(sub - 1):
                p_re, p_im = _cmul(s_re, s_im, c_re, c_im)
                c_re, c_im = shift(e_re + p_re, h0[g][0]), shift(e_im + p_im, h0[g][1])
            carries.append((c_re, c_im))

        def fix_step(j, carry):
            jb = steps - 1 - j
            new = []
            for g in range(gs):
                d_re, d_im = carry[g]
                hf_scr[g, blk(j), :half] += d_re
                hf_scr[g, blk(j), half:] += d_im
                hb_scr[g, blk(jb), :half] += d_re
                hb_scr[g, blk(jb), half:] += d_im
                new.append(_cmul(a_re[g], a_im[g], d_re, d_im))
            return tuple(new)

        lax.fori_loop(0, steps, fix_step, tuple(carries))

        for g in range(gs):
            h_in = jnp.where(take_fwd, hf_scr[g], hb_scr[g]).T.astype(BF16)
            yt = (jnp.dot(mt_ref[g0 + g], ut_scr[g0 + g], preferred_element_type=F32)
                  + jnp.dot(vt_ref[g0 + g], h_in, preferred_element_type=F32))
            for t in range(L):
                yt_scr[t, (g0 + g) * H:(g0 + g + 1) * H, :] = yt[t * H:(t + 1) * H, :].astype(BF16)

    def drain(t, carry):
        y = yt_scr[t].astype(F32).T
        yl_ref[pl.ds(t, n_lat, stride=CHUNK_PITCH), :] = y[:n_lat]
        yc_ref[pl.ds(t, ctx_chunks, stride=L), :] = y[n_lat:n_lat + ctx_chunks]
        return carry

    lax.fori_loop(0, L, drain, 0)
    for t in range(L, CHUNK_PITCH):
        yl_ref[pl.ds(t, n_lat, stride=CHUNK_PITCH), :] = jnp.zeros((n_lat, LANES), F32)


def _ssm(u_lat, u_ctx, mt, wt, vt, a):
    bsz, steps, nseg, pitch, n = u_lat.shape
    L = SSM_CHUNK
    ctx_len = u_ctx.shape[1]
    ctx_chunks = ctx_len // L
    n_lat = steps * nseg
    n_all = -(-(n_lat + ctx_chunks) // LANES) * LANES
    gl = SSM_LANE_GROUPS
    f = L * SSM_GROUP
    ns = wt.shape[1]
    lat_spec = pl.BlockSpec((None, n_lat * pitch, LANES), lambda b, j: (b, 0, j), pipeline_mode=pl.Buffered(1))
    ctx_spec = pl.BlockSpec((None, ctx_len, LANES), lambda b, j: (b, 0, j))
    grp = lambda shape: pl.BlockSpec((gl,) + shape, lambda b, j: (j, 0, 0), pipeline_mode=pl.Buffered(1))
    y_lat, y_ctx = pl.pallas_call(
        functools.partial(_ssm_kernel, ctx_chunks),
        grid=(bsz, n // LANES),
        in_specs=[lat_spec, ctx_spec, grp((f, f)), grp((ns, f)), grp((f, ns)), grp((4, ns // 2))],
        out_specs=[lat_spec, ctx_spec],
        out_shape=[jax.ShapeDtypeStruct((bsz, n_lat * pitch, n), F32),
                   jax.ShapeDtypeStruct((bsz, ctx_len, n), F32)],
        scratch_shapes=[pltpu.VMEM((gl, f, n_all), BF16), pltpu.VMEM((L, LANES, n_all), BF16)]
        + [pltpu.VMEM((SSM_SCAN_GROUPS, n_all, ns), F32)] * 3,
        compiler_params=pltpu.CompilerParams(
            dimension_semantics=("arbitrary", "arbitrary"), vmem_limit_bytes=VMEM_LIMIT_BYTES),
        name="ssm_chunked",
    )(u_lat.reshape(bsz, n_lat * pitch, n), u_ctx, mt, wt, vt, a)
    return y_lat.reshape(u_lat.shape), y_ctx


def _cpow(z_re, z_im, n):
    out = None
    while n:
        if n & 1:
            out = (z_re, z_im) if out is None else _cmul(out[0], out[1], z_re, z_im)
        n >>= 1
        if n:
            z_re, z_im = _cmul(z_re, z_im, z_re, z_im)
    return out


def _ops_kernel(seg_steps, pw_ref, pv_ref, pa_ref, bbt_ref, c_ref, csel_ref, dcol_ref,
                mt_ref, wt_ref, vt_ref, a_ref, w_scr):
    gb = mt_ref.shape[0]
    f = mt_ref.shape[-1]
    H = SSM_GROUP
    L = f // H
    half = STATE * N_DIR
    lane = lax.broadcasted_iota(jnp.int32, (H, 2 * f), 1)
    row = lax.broadcasted_iota(jnp.int32, (H, 2 * f), 0)
    centre = lane == (L - 1) * H + row
    zeros = jnp.zeros((H, f), F32)
    for g in range(gb):
        bt_re, bt_im = bbt_ref[g, 0], bbt_ref[g, 1]
        c_re, c_im = c_ref[g, 0], c_ref[g, 1]
        for t in range(L):
            rows = slice(t * H, (t + 1) * H)
            w_re, w_im = _cmul(pw_ref[g, 0, t:t + 1, :], pw_ref[g, 1, t:t + 1, :], bt_re, bt_im)
            w_scr[rows, :half] = w_re
            w_scr[rows, half:] = w_im
            g_re, g_im = _cmul(pv_ref[g, 0, t:t + 1, :], pv_ref[g, 1, t:t + 1, :], c_re, c_im)
            vt_ref[g, rows, :half] = g_re.astype(vt_ref.dtype)
            vt_ref[g, rows, half:] = (-g_im).astype(vt_ref.dtype)
        wt = w_scr[...].T
        wt_ref[g] = wt.astype(wt_ref.dtype)
        kk = jnp.dot(csel_ref[g], wt, preferred_element_type=F32, precision=lax.Precision.HIGHEST)
        kf = jnp.concatenate([kk[:H], zeros], axis=1)
        kb = pltpu.roll(jnp.concatenate([kk[H:], zeros], axis=1), (L - 1) * H, 1)
        dmat = jnp.concatenate([dcol_ref[g]] * (2 * f // LANES), axis=1)
        k = kf + kb + jnp.where(centre, dmat, 0.0)
        for t in range(L):
            off = f - (t + 1) * H
            win = k if off == 0 else pltpu.roll(k, 2 * f - off, 1)
            mt_ref[g, t * H:(t + 1) * H, :] = win[:, :f].astype(mt_ref.dtype)
        ac_re, ac_im = pa_ref[g, 0:1, :], pa_ref[g, 1:2, :]
        as_re, as_im = _cpow(ac_re, ac_im, seg_steps)
        a_ref[g, 0:1, :] = ac_re
        a_ref[g, 1:2, :] = ac_im
        a_ref[g, 2:3, :] = as_re
        a_ref[g, 3:4, :] = as_im


def _ssm_operators(lam_re, lam_im, log_dt, b_re, b_im, c_re, c_im, d_skip, seg_steps):
    L = SSM_CHUNK
    H = SSM_GROUP
    lam_re = lam_re.astype(F32)
    lam_im = lam_im.astype(F32)
    depth, _, ng, _ = lam_re.shape
    dt = jnp.exp(log_dt.astype(F32))[..., None]
    mag = jnp.exp(lam_re * dt)
    a_re = mag * jnp.cos(lam_im * dt)
    a_im = mag * jnp.sin(lam_im * dt)
    nr, ni = a_re - 1.0, a_im
    den = lam_re * lam_re + lam_im * lam_im
    f_re = (nr * lam_re + ni * lam_im) / den
    f_im = (ni * lam_re - nr * lam_im) / den
    br, bi = b_re.astype(F32), b_im.astype(F32)
    bb_re = f_re[..., None] * br - f_im[..., None] * bi
    bb_im = f_re[..., None] * bi + f_im[..., None] * br
    k = jnp.arange(L + 1, dtype=F32)[:, None]
    pmag = jnp.exp(lam_re[..., None, :] * dt[..., None, :] * k)
    parg = lam_im[..., None, :] * dt[..., None, :] * k
    p_re = pmag * jnp.cos(parg)
    p_im = pmag * jnp.sin(parg)
    both = lambda fwd, bwd: jnp.concatenate([fwd, bwd], axis=-1)
    reim = lambda re, im: jnp.stack([re, im], axis=2)
    pw = reim(both(p_re[:, 0, :, L - 1::-1][:, :, :L], p_re[:, 1, :, :L]),
              both(p_im[:, 0, :, L - 1::-1][:, :, :L], p_im[:, 1, :, :L]))
    pv = reim(both(p_re[:, 0, :, 1:], p_re[:, 1, :, :0:-1]), both(p_im[:, 0, :, 1:], p_im[:, 1, :, :0:-1]))
    pa = jnp.stack([both(p_re[:, 0, :, L], p_re[:, 1, :, L]), both(p_im[:, 0, :, L], p_im[:, 1, :, L])], axis=2)
    sw = lambda z: jnp.swapaxes(z, -1, -2)
    bbt = reim(both(sw(bb_re[:, 0]), sw(bb_re[:, 1])), both(sw(bb_im[:, 0]), sw(bb_im[:, 1])))
    cr, ci = c_re.astype(F32), c_im.astype(F32)
    cc = reim(both(cr[:, 0], cr[:, 1]), both(ci[:, 0], ci[:, 1]))
    z = jnp.zeros_like(cr[:, 0])
    csel = jnp.concatenate([jnp.concatenate([cr[:, 0], z, -ci[:, 0], z], axis=-1),
                            jnp.concatenate([z, cr[:, 1], z, -ci[:, 1]], axis=-1)], axis=-2)
    dcol = jnp.broadcast_to(d_skip.astype(F32).reshape(depth, ng, H, 1), (depth, ng, H, LANES))
    f = L * H
    ns = 2 * N_DIR * STATE
    gb = 8
    blk = lambda *shape: pl.BlockSpec((None, gb) + shape, lambda l, j: (l, j) + (0,) * len(shape))
    return pl.pallas_call(
        functools.partial(_ops_kernel, seg_steps),
        grid=(depth, ng // gb),
        in_specs=[blk(2, L, ns // 2), blk(2, L, ns // 2), blk(2, ns // 2), blk(2, H, ns // 2),
                  blk(2, H, ns // 2), blk(2 * H, ns), blk(H, LANES)],
        out_specs=[blk(f, f), blk(ns, f), blk(f, ns), blk(4, ns // 2)],
        out_shape=[jax.ShapeDtypeStruct((depth, ng, f, f), BF16), jax.ShapeDtypeStruct((depth, ng, ns, f), BF16),
                   jax.ShapeDtypeStruct((depth, ng, f, ns), BF16), jax.ShapeDtypeStruct((depth, ng, 4, ns // 2), F32)],
        scratch_shapes=[pltpu.VMEM((f, ns), F32)],
        compiler_params=pltpu.CompilerParams(
            dimension_semantics=("parallel", "parallel"), vmem_limit_bytes=VMEM_LIMIT_BYTES),
        name="ssm_operators",
    )(pw, pv, pa, bbt, cc, csel, dcol)


def _gelu_tanh(x):
    return 0.5 * x * (1.0 + jnp.tanh(np.sqrt(2.0 / np.pi).astype(np.float32)
                                     * (x + np.float32(0.044715) * (x * x * x))))


def _mixer_kernel(ctx_tiles, ctx_len, x_ref, ysl_ref, ysc_ref, mod_ref, g_ref, win_ref, bin_ref, cw_ref,
                  woa_ref, wglu_ref, bglu_ref, wo_ref, o_ref):
    tm, d = x_ref.shape
    dc = woa_ref.shape[0]
    is_ctx = pl.program_id(1) < ctx_tiles
    row = jnp.where(is_ctx, 2, pl.program_id(0))
    x = x_ref[...]
    h = _norm_mod(x, g_ref[...], _mod_vec(mod_ref, row, 0, d), _mod_vec(mod_ref, row, 1, d))
    z = jnp.dot(h.astype(BF16), win_ref[...], preferred_element_type=F32) + bin_ref[...]
    g_b, g_c, x_in = z[:, :dc], z[:, dc:2 * dc], z[:, 2 * dc:3 * dc]
    gate_a, gate_b = z[:, 3 * dc:3 * dc + d], z[:, 3 * dc + d:]
    v = g_c * x_in
    t = lax.broadcasted_iota(jnp.int32, (tm, 1), 0)
    seg = jnp.where(is_ctx, ctx_len - 1, GRID_W - 1)
    pos = t & seg
    v_prev = jnp.where(pos == 0, 0.0, pltpu.roll(v, 1, 0))
    v_next = jnp.where(pos == seg, 0.0, pltpu.roll(v, tm - 1, 0))
    cv = cw_ref[0:1, :] * v_prev + cw_ref[1:2, :] * v + cw_ref[2:3, :] * v_next
    y_a = jnp.dot((g_b * cv).astype(BF16), woa_ref[...], preferred_element_type=F32)
    ys = jnp.where(is_ctx, ysc_ref[...], ysl_ref[:, :SSM_CHUNK, :].reshape(tm, ysc_ref.shape[-1]))
    s = _gelu_tanh(ys)
    gl = jnp.dot(s.astype(BF16), wglu_ref[...], preferred_element_type=F32) + bglu_ref[...]
    y_b = gl[:, :d] * jax.nn.sigmoid(gl[:, d:])
    merged = jax.nn.sigmoid(gate_a) * y_a + jax.nn.sigmoid(gate_b) * y_b
    out = jnp.dot(merged.astype(BF16), wo_ref[...], preferred_element_type=F32)
    o_ref[...] = x + _mod_vec(mod_ref, row, 2, d) * out


def _mixer(xs, ys_lat, ys_ctx, mod, g, w_in, b_in, conv_w, w_out_a, w_glu, b_glu, w_o, ctx_tiles, ctx_len):
    bsz, s, d = xs.shape
    n = ys_ctx.shape[-1]
    steps = ys_lat.shape[1]
    cpt, index = _chunk_tile_spec(n, ctx_tiles)
    tok = lambda w: pl.BlockSpec((None, TOK_TILE, w), lambda b, i: (b, i, 0))
    consts = (mod, g, w_in, b_in, conv_w, w_out_a, w_glu, b_glu, w_o)
    return pl.pallas_call(
        functools.partial(_mixer_kernel, ctx_tiles, ctx_len),
        grid=(bsz, s // TOK_TILE),
        in_specs=[tok(d),
                  pl.BlockSpec((None, cpt, None, CHUNK_PITCH, n), lambda b, i: index(b, i, steps // cpt)),
                  pl.BlockSpec((None, ctx_len, n), lambda b, i: (b, 0, 0))]
        + [_const_spec(c.shape) for c in consts],
        out_specs=tok(d),
        out_shape=jax.ShapeDtypeStruct((bsz, s, d), F32),
        compiler_params=pltpu.CompilerParams(
            dimension_semantics=("parallel", "parallel"), vmem_limit_bytes=VMEM_LIMIT_BYTES),
        name="mixer",
    )(xs, ys_lat, ys_ctx, *consts)


def _ffn_kernel(ctx_tiles, tile_offset, final, x_ref, mod_ref, g_ref, win_ref, wout_ref, *rest):
    d = x_ref.shape[-1]
    dff = wout_ref.shape[0]
    is_ctx = pl.program_id(1) + tile_offset < ctx_tiles
    row = jnp.where(is_ctx, 2, pl.program_id(0))
    x = x_ref[...]
    h = _norm_mod(x, g_ref[...], _mod_vec(mod_ref, row, 3, d), _mod_vec(mod_ref, row, 4, d))
    z = jnp.dot(h.astype(BF16), win_ref[...], preferred_element_type=F32)
    gate, up = z[:, :dff], z[:, dff:]
    act = gate * jax.nn.sigmoid(gate) * up
    out = jnp.dot(act.astype(BF16), wout_ref[...], preferred_element_type=F32)
    y = x + _mod_vec(mod_ref, row, 5, d) * out
    if final:
        fg_ref, o_ref = rest
        ms = jnp.mean(y * y, axis=-1, keepdims=True)
        o_ref[...] = y * lax.rsqrt(ms + RMS_EPS) * fg_ref[...]
    else:
        modn_ref, gn_ref, wu_ref, bu_ref, o_ref, ul_ref, uc_ref = rest
        o_ref[...] = y
        _emit_ssm_inputs(y, row, is_ctx, modn_ref, gn_ref, wu_ref, bu_ref, ul_ref, uc_ref)


def _ffn(xs, mod, g, w_ff_in, w_ff_out, ctx_tiles, final_g=None, nxt=None):
    bsz, s, d = xs.shape
    final = final_g is not None
    tile_offset = ctx_tiles if final else 0
    s_out = s - tile_offset * TOK_TILE
    consts = (mod, g, w_ff_in, w_ff_out) + ((final_g,) if final else tuple(nxt))
    out_specs = [pl.BlockSpec((None, TOK_TILE, d), lambda b, i: (b, i, 0))]
    out_shapes = [jax.ShapeDtypeStruct((bsz, s_out, d), F32)]
    if not final:
        u_specs, u_shapes = _ssm_input_specs(bsz, s, nxt[2].shape[1], ctx_tiles)
        out_specs, out_shapes = out_specs + u_specs, out_shapes + u_shapes
    return pl.pallas_call(
        functools.partial(_ffn_kernel, ctx_tiles, tile_offset, final),
        grid=(bsz, s_out // TOK_TILE),
        in_specs=[pl.BlockSpec((None, TOK_TILE, d), lambda b, i: (b, i + tile_offset, 0))]
        + [_const_spec(c.shape) for c in consts],
        out_specs=out_specs,
        out_shape=out_shapes,
        compiler_params=pltpu.CompilerParams(
            dimension_semantics=("arbitrary", "arbitrary"), vmem_limit_bytes=VMEM_LIMIT_BYTES),
        name="ffn_final" if final else "ffn",
    )(xs, *consts)


def kernel(x, c, ctx, c_ctx, w_mod, b_mod, norm1_g, norm2_g, w_in, b_in, conv_w, w_out_a, lam_re, lam_im,
           log_dt, b_re, b_im, c_re, c_im, d_skip, w_glu, b_glu, w_o, w_ff_in, w_ff_out, final_g):
    bsz, seq, d = x.shape
    depth = w_mod.shape[0]
    ctx_len = ctx.shape[1]
    d_conv = conv_w.shape[-1]
    d_ssm = d_skip.shape[-1]
    u_lo, u_hi = 3 * d_conv, 3 * d_conv + d_ssm
    assert ctx_len == TOK_TILE and seq % TOK_TILE == 0 and TOK_TILE % GRID_W == 0
    assert ctx_len == SSM_CHUNK * SUBLANES and d_ssm % LANES == 0 and bsz <= 2
    assert seq % (SUBLANES * TOK_TILE) == 0
    seg_steps = seq // (SSM_CHUNK * SUBLANES)
    ctx_tiles = ctx_len // TOK_TILE

    cvec = jnp.zeros((8, d), F32).at[:bsz].set(c.astype(F32)).at[2].set(c_ctx.astype(F32))
    mod = _mod_all(cvec, w_mod, b_mod)

    mt_op, wt_op, vt_op, a_op = _ssm_operators(lam_re, lam_im, log_dt, b_re, b_im, c_re, c_im, d_skip,
                                               seg_steps)

    w_u = w_in[:, :, u_lo:u_hi].astype(BF16)
    b_u = b_in[:, None, u_lo:u_hi]
    w_rest = jnp.concatenate([w_in[:, :, :u_lo], w_in[:, :, u_hi:]], axis=-1).astype(BF16)
    b_rest = jnp.concatenate([b_in[:, None, :u_lo], b_in[:, None, u_hi:]], axis=-1)
    w_out_a, w_glu, w_o = w_out_a.astype(BF16), w_glu.astype(BF16), w_o.astype(BF16)
    w_ff_in, w_ff_out = w_ff_in.astype(BF16), w_ff_out.astype(BF16)
    fg = final_g.reshape(1, d)

    xs = jnp.concatenate([ctx, x], axis=1)
    ssm_in = lambda l: (mod[l], norm1_g[l].reshape(1, d), w_u[l], b_u[l])
    u_lat, u_ctx = _uproj(xs, *ssm_in(0), ctx_tiles)
    for l in range(depth):
        g1, g2 = norm1_g[l].reshape(1, d), norm2_g[l].reshape(1, d)
        ys_lat, ys_ctx = _ssm(u_lat, u_ctx, mt_op[l], wt_op[l], vt_op[l], a_op[l])
        xs = _mixer(xs, ys_lat, ys_ctx, mod[l], g1, w_rest[l], b_rest[l], conv_w[l], w_out_a[l], w_glu[l],
                    b_glu[l, None], w_o[l], ctx_tiles, ctx_len)
        if l + 1 < depth:
            xs, u_lat, u_ctx = _ffn(xs, mod[l], g2, w_ff_in[l], w_ff_out[l], ctx_tiles, nxt=ssm_in(l + 1))
        else:
            xs, = _ffn(xs, mod[l], g2, w_ff_in[l], w_ff_out[l], ctx_tiles, final_g=fg)
    return xs
```

```python
import functools

import numpy as np
import jax
import jax.numpy as jnp
from jax import lax
from jax.experimental import pallas as pl
from jax.experimental.pallas import tpu as pltpu

GRID_W = 64
CONV_K = 3
SSM_GROUP = 16
STATE = 64
N_DIR = 2
RMS_EPS = 1e-6

TOK_TILE = 512
SSM_CHUNK = 32
CHUNK_PITCH = 40
SUBLANES = 8
LANES = 128
SSM_LANE_GROUPS = LANES // SSM_GROUP
SSM_SCAN_GROUPS = 4
VMEM_LIMIT_BYTES = 56 * 1024 * 1024

F32 = jnp.float32
BF16 = jnp.bfloat16


def _const_spec(shape):
    nd = len(shape)
    return pl.BlockSpec(shape, lambda *_: (0,) * nd, pipeline_mode=pl.Buffered(1))


def _norm_mod(x, g, shift, scale):
    ms = jnp.mean(x * x, axis=-1, keepdims=True)
    y = x * lax.rsqrt(ms + RMS_EPS) * g
    return y * (1.0 + scale) + shift


def _mod_vec(mod_ref, row, k, d):
    return mod_ref[pl.ds(row, 1), k * d:(k + 1) * d]


def _chunk_tile_spec(n, ctx_tiles):
    cpt = TOK_TILE // SSM_CHUNK

    def index(b, i, steps_per_seg_tiles):
        k = jnp.maximum(i - ctx_tiles, 0)
        return (b, k % steps_per_seg_tiles, k // steps_per_seg_tiles, 0, 0)

    return cpt, index


def _mod_kernel(c_ref, w_ref, b_ref, o_ref):
    c = c_ref[...]
    s = c * jax.nn.sigmoid(c)
    o_ref[...] = jnp.dot(s.astype(BF16), w_ref[...].astype(BF16),
                         preferred_element_type=F32) + b_ref[...]


def _mod_all(cvec, w_mod, b_mod):
    depth, d, n = w_mod.shape
    tn = 1536
    return pl.pallas_call(
        _mod_kernel,
        grid=(depth, n // tn),
        in_specs=[pl.BlockSpec((8, d), lambda l, j: (0, 0)),
                  pl.BlockSpec((None, d, tn), lambda l, j: (l, 0, j)),
                  pl.BlockSpec((None, 1, tn), lambda l, j: (l, 0, j))],
        out_specs=pl.BlockSpec((None, 8, tn), lambda l, j: (l, 0, j)),
        out_shape=jax.ShapeDtypeStruct((depth, 8, n), F32),
        compiler_params=pltpu.CompilerParams(
            dimension_semantics=("arbitrary", "arbitrary"), vmem_limit_bytes=VMEM_LIMIT_BYTES),
        name="adaln_mod",
    )(cvec, w_mod, b_mod.reshape(depth, 1, n))


def _emit_ssm_inputs(x, row, is_ctx, mod_ref, g_ref, w_ref, b_ref, ul_ref, uc_ref):
    d = x.shape[-1]
    h = _norm_mod(x, g_ref[...], _mod_vec(mod_ref, row, 0, d), _mod_vec(mod_ref, row, 1, d))
    u = jnp.dot(h.astype(BF16), w_ref[...], preferred_element_type=F32) + b_ref[...]
    cpt, pitch, n = ul_ref.shape
    ul_ref[:, :SSM_CHUNK, :] = u.reshape(cpt, SSM_CHUNK, n)
    ul_ref[:, SSM_CHUNK:, :] = jnp.zeros((cpt, pitch - SSM_CHUNK, n), F32)

    @pl.when(is_ctx)
    def _():
        uc_ref[...] = u[:uc_ref.shape[0]]


def _uproj_kernel(ctx_tiles, x_ref, mod_ref, g_ref, w_ref, b_ref, ul_ref, uc_ref):
    is_ctx = pl.program_id(1) < ctx_tiles
    row = jnp.where(is_ctx, 2, pl.program_id(0))
    _emit_ssm_inputs(x_ref[...], row, is_ctx, mod_ref, g_ref, w_ref, b_ref, ul_ref, uc_ref)


def _ssm_input_specs(bsz, s, n, ctx_tiles, ctx_len):
    steps = (s - ctx_tiles * TOK_TILE) // (SSM_CHUNK * SUBLANES)
    cpt, index = _chunk_tile_spec(n, ctx_tiles)
    specs = [pl.BlockSpec((None, cpt, None, CHUNK_PITCH, n), lambda b, i: index(b, i, steps // cpt)),
             pl.BlockSpec((None, ctx_len, n), lambda b, i: (b, 0, 0))]
    shapes = [jax.ShapeDtypeStruct((bsz, steps, SUBLANES, CHUNK_PITCH, n), F32),
              jax.ShapeDtypeStruct((bsz, ctx_len, n), F32)]
    return specs, shapes


def _uproj(xs, mod, g, w_u, b_u, ctx_tiles, ctx_len):
    bsz, s, d = xs.shape
    out_specs, out_shapes = _ssm_input_specs(bsz, s, w_u.shape[1], ctx_tiles, ctx_len)
    return pl.pallas_call(
        functools.partial(_uproj_kernel, ctx_tiles),
        grid=(bsz, s // TOK_TILE),
        in_specs=[pl.BlockSpec((None, TOK_TILE, d), lambda b, i: (b, i, 0)),
                  _const_spec(mod.shape), _const_spec(g.shape),
                  _const_spec(w_u.shape), _const_spec(b_u.shape)],
        out_specs=out_specs,
        out_shape=out_shapes,
        compiler_params=pltpu.CompilerParams(
            dimension_semantics=("arbitrary", "arbitrary"), vmem_limit_bytes=VMEM_LIMIT_BYTES),
        name="ssm_uproj",
    )(xs, mod, g, w_u, b_u)


def _cmul(a_re, a_im, b_re, b_im):
    return a_re * b_re - a_im * b_im, a_re * b_im + a_im * b_re


def _ssm_kernel(ctx_chunks, ul_ref, uc_ref, mt_ref, wt_ref, vt_ref, a_ref, yl_ref, yc_ref,
                ut_scr, yt_scr, x_scr, hf_scr, hb_scr):
    L = SSM_CHUNK
    H = SSM_GROUP
    sub = SUBLANES
    n_lat = ul_ref.shape[0] // CHUNK_PITCH
    n_all = ut_scr.shape[-1]
    steps = n_lat // sub
    half = STATE * N_DIR
    gl = SSM_LANE_GROUPS
    gs = SSM_SCAN_GROUPS

    pad = jnp.zeros((n_all - n_lat - ctx_chunks, LANES), F32)

    def fill(t, carry):
        lat = ul_ref[pl.ds(t, n_lat, stride=CHUNK_PITCH), :]
        cx = uc_ref[pl.ds(t, ctx_chunks, stride=L), :]
        at = jnp.concatenate([lat, cx, pad], axis=0).T
        for g in range(gl):
            ut_scr[g, pl.ds(pl.multiple_of(t * H, H), H), :] = at[g * H:(g + 1) * H, :].astype(BF16)
        return carry

    lax.fori_loop(0, L, fill, 0, unroll=4)

    is_fwd = lax.broadcasted_iota(jnp.int32, (sub, half), 1) < STATE
    is_fwd1 = is_fwd[:1]
    sl = lax.broadcasted_iota(jnp.int32, (sub, half), 0)
    blk = lambda j: pl.ds(pl.multiple_of(j * sub, sub), sub)
    zero = jnp.zeros((sub, half), F32)
    lane2 = lax.broadcasted_iota(jnp.int32, (n_all, 2 * half), 1)
    take_fwd = (lane2 % half) < STATE
    tail0 = n_lat + ctx_chunks
    assert tail0 % sub == 0 and ctx_chunks == sub

    for g0 in range(0, gl, gs):
        coef = lambda g, r, rows=sub: jnp.broadcast_to(a_ref[g0 + g, r:r + 1, :], (rows, half))
        a_re = [coef(g, 0) for g in range(gs)]
        a_im = [coef(g, 1) for g in range(gs)]

        for g in range(gs):
            xt = jnp.dot(wt_ref[g0 + g], ut_scr[g0 + g], preferred_element_type=F32)
            x_scr[g] = xt.T
            hf_scr[g, tail0:, :] = jnp.zeros((n_all - tail0, 2 * half), F32)
            hb_scr[g, tail0:, :] = jnp.zeros((n_all - tail0, 2 * half), F32)

        h0 = []
        for g in range(gs):
            xc = x_scr[g, n_lat:n_lat + ctx_chunks, :]
            ar, ai = a_re[g][:1], a_im[g][:1]
            h_re = h_im = jnp.zeros((1, half), F32)
            hf_rows, hb_rows = [], [None] * ctx_chunks
            for k in range(ctx_chunks):
                kb = ctx_chunks - 1 - k
                hf_rows.append((h_re, h_im))
                hb_rows[kb] = (h_re, h_im)
                x_re = jnp.where(is_fwd1, xc[k:k + 1, :half], xc[kb:kb + 1, :half])
                x_im = jnp.where(is_fwd1, xc[k:k + 1, half:], xc[kb:kb + 1, half:])
                p_re, p_im = _cmul(ar, ai, h_re, h_im)
                h_re, h_im = p_re + x_re, p_im + x_im
            cat = lambda rows, part: jnp.concatenate([r[part] for r in rows], axis=0)
            hf_scr[g, n_lat:n_lat + ctx_chunks, :half] = cat(hf_rows, 0)
            hf_scr[g, n_lat:n_lat + ctx_chunks, half:] = cat(hf_rows, 1)
            hb_scr[g, n_lat:n_lat + ctx_chunks, :half] = cat(hb_rows, 0)
            hb_scr[g, n_lat:n_lat + ctx_chunks, half:] = cat(hb_rows, 1)
            h0.append((jnp.broadcast_to(h_re, (sub, half)), jnp.broadcast_to(h_im, (sub, half))))

        def scan_step(j, carry):
            jb = steps - 1 - j
            new = []
            for g in range(gs):
                h_re, h_im = carry[g]
                x_re = jnp.where(is_fwd, x_scr[g, blk(j), :half], x_scr[g, blk(jb), :half])
                x_im = jnp.where(is_fwd, x_scr[g, blk(j), half:], x_scr[g, blk(jb), half:])
                hf_scr[g, blk(j), :half] = h_re
                hf_scr[g, blk(j), half:] = h_im
                hb_scr[g, blk(jb), :half] = h_re
                hb_scr[g, blk(jb), half:] = h_im
                p_re, p_im = _cmul(a_re[g], a_im[g], h_re, h_im)
                new.append((p_re + x_re, p_im + x_im))
            return tuple(new)

        ends = lax.fori_loop(0, steps, scan_step, tuple((zero, zero) for _ in range(gs)), unroll=2)

        carries = []
        for g in range(gs):
            def shift(t, h):
                return jnp.where(is_fwd, jnp.where(sl < 1, h, pltpu.roll(t, 1, 0)),
                                 jnp.where(sl >= sub - 1, h, pltpu.roll(t, sub - 1, 0)))
            e_re, e_im = ends[g]
            s_re, s_im = coef(g, 2), coef(g, 3)
            c_re, c_im = shift(zero, h0[g][0]), shift(zero, h0[g][1])
            for _ in range(sub - 1):
                p_re, p_im = _cmul(s_re, s_im, c_re, c_im)
                c_re, c_im = shift(e_re + p_re, h0[g][0]), shift(e_im + p_im, h0[g][1])
            carries.append((c_re, c_im))

        def fix_step(j, carry):
            jb = steps - 1 - j
            new = []
            for g in range(gs):
                d_re, d_im = carry[g]
                hf_scr[g, blk(j), :half] += d_re
                hf_scr[g, blk(j), half:] += d_im
                hb_scr[g, blk(jb), :half] += d_re
                hb_scr[g, blk(jb), half:] += d_im
                new.append(_cmul(a_re[g], a_im[g], d_re, d_im))
            return tuple(new)

        lax.fori_loop(0, steps, fix_step, tuple(carries), unroll=2)

        for g in range(gs):
            h_in = jnp.where(take_fwd, hf_scr[g], hb_scr[g]).T.astype(BF16)
            yt = (jnp.dot(mt_ref[g0 + g], ut_scr[g0 + g], preferred_element_type=F32)
                  + jnp.dot(vt_ref[g0 + g], h_in, preferred_element_type=F32))
            for t in range(L):
                yt_scr[t, (g0 + g) * H:(g0 + g + 1) * H, :] = yt[t * H:(t + 1) * H, :].astype(BF16)

    def drain(t, carry):
        y = yt_scr[t].astype(F32).T
        yl_ref[pl.ds(t, n_lat, stride=CHUNK_PITCH), :] = y[:n_lat]
        yc_ref[pl.ds(t, ctx_chunks, stride=L), :] = y[n_lat:n_lat + ctx_chunks]
        return carry

    lax.fori_loop(0, L, drain, 0, unroll=4)
    for t in range(L, CHUNK_PITCH):
        yl_ref[pl.ds(t, n_lat, stride=CHUNK_PITCH), :] = jnp.zeros((n_lat, LANES), F32)


def _ssm(u_lat, u_ctx, mt, wt, vt, a):
    bsz, steps, nseg, pitch, n = u_lat.shape
    L = SSM_CHUNK
    ctx_len = u_ctx.shape[1]
    ctx_chunks = ctx_len // L
    n_lat = steps * nseg
    n_all = -(-(n_lat + ctx_chunks) // LANES) * LANES
    gl = SSM_LANE_GROUPS
    f = L * SSM_GROUP
    ns = wt.shape[1]
    lat_spec = pl.BlockSpec((None, n_lat * pitch, LANES), lambda b, j: (b, 0, j), pipeline_mode=pl.Buffered(1))
    ctx_spec = pl.BlockSpec((None, ctx_len, LANES), lambda b, j: (b, 0, j))
    grp = lambda shape: pl.BlockSpec((gl,) + shape, lambda b, j: (j, 0, 0), pipeline_mode=pl.Buffered(1))
    y_lat, y_ctx = pl.pallas_call(
        functools.partial(_ssm_kernel, ctx_chunks),
        grid=(bsz, n // LANES),
        in_specs=[lat_spec, ctx_spec, grp((f, f)), grp((ns, f)), grp((f, ns)), grp((4, ns // 2))],
        out_specs=[lat_spec, ctx_spec],
        out_shape=[jax.ShapeDtypeStruct((bsz, n_lat * pitch, n), F32),
                   jax.ShapeDtypeStruct((bsz, ctx_len, n), F32)],
        scratch_shapes=[pltpu.VMEM((gl, f, n_all), BF16), pltpu.VMEM((L, LANES, n_all), BF16)]
        + [pltpu.VMEM((SSM_SCAN_GROUPS, n_all, ns), F32)] * 3,
        compiler_params=pltpu.CompilerParams(
            dimension_semantics=("arbitrary", "arbitrary"), vmem_limit_bytes=VMEM_LIMIT_BYTES),
        name="ssm_chunked",
    )(u_lat.reshape(bsz, n_lat * pitch, n), u_ctx, mt, wt, vt, a)
    return y_lat.reshape(u_lat.shape), y_ctx


def _cpow(z_re, z_im, n):
    out = None
    while n:
        if n & 1:
            out = (z_re, z_im) if out is None else _cmul(out[0], out[1], z_re, z_im)
        n >>= 1
        if n:
            z_re, z_im = _cmul(z_re, z_im, z_re, z_im)
    return out


def _ops_kernel(seg_steps, pw_ref, pv_ref, pa_ref, bbt_ref, c_ref, csel_ref, dcol_ref,
                mt_ref, wt_ref, vt_ref, a_ref, w_scr):
    gb = mt_ref.shape[0]
    f = mt_ref.shape[-1]
    H = SSM_GROUP
    L = f // H
    half = STATE * N_DIR
    lane = lax.broadcasted_iota(jnp.int32, (H, 2 * f), 1)
    row = lax.broadcasted_iota(jnp.int32, (H, 2 * f), 0)
    centre = lane == (L - 1) * H + row
    zeros = jnp.zeros((H, f), F32)
    for g in range(gb):
        bt_re, bt_im = bbt_ref[g, 0], bbt_ref[g, 1]
        c_re, c_im = c_ref[g, 0], c_ref[g, 1]
        for t in range(L):
            rows = slice(t * H, (t + 1) * H)
            w_re, w_im = _cmul(pw_ref[g, 0, t:t + 1, :], pw_ref[g, 1, t:t + 1, :], bt_re, bt_im)
            w_scr[rows, :half] = w_re
            w_scr[rows, half:] = w_im
            g_re, g_im = _cmul(pv_ref[g, 0, t:t + 1, :], pv_ref[g, 1, t:t + 1, :], c_re, c_im)
            vt_ref[g, rows, :half] = g_re.astype(vt_ref.dtype)
            vt_ref[g, rows, half:] = (-g_im).astype(vt_ref.dtype)
        wt = w_scr[...].T
        wt_ref[g] = wt.astype(wt_ref.dtype)
        kk = jnp.dot(csel_ref[g], wt, preferred_element_type=F32, precision=lax.Precision.HIGHEST)
        kf = jnp.concatenate([kk[:H], zeros], axis=1)
        kb = pltpu.roll(jnp.concatenate([kk[H:], zeros], axis=1), (L - 1) * H, 1)
        dmat = jnp.concatenate([dcol_ref[g]] * (2 * f // LANES), axis=1)
        k = kf + kb + jnp.where(centre, dmat, 0.0)
        for t in range(L):
            off = f - (t + 1) * H
            win = k if off == 0 else pltpu.roll(k, 2 * f - off, 1)
            mt_ref[g, t * H:(t + 1) * H, :] = win[:, :f].astype(mt_ref.dtype)
        ac_re, ac_im = pa_ref[g, 0:1, :], pa_ref[g, 1:2, :]
        as_re, as_im = _cpow(ac_re, ac_im, seg_steps)
        a_ref[g, 0:1, :] = ac_re
        a_ref[g, 1:2, :] = ac_im
        a_ref[g, 2:3, :] = as_re
        a_ref[g, 3:4, :] = as_im


def _ssm_operators(lam_re, lam_im, log_dt, b_re, b_im, c_re, c_im, d_skip, seg_steps):
    L = SSM_CHUNK
    H = SSM_GROUP
    lam_re = lam_re.astype(F32)
    lam_im = lam_im.astype(F32)
    depth, _, ng, _ = lam_re.shape
    dt = jnp.exp(log_dt.astype(F32))[..., None]
    mag = jnp.exp(lam_re * dt)
    a_re = mag * jnp.cos(lam_im * dt)
    a_im = mag * jnp.sin(lam_im * dt)
    nr, ni = a_re - 1.0, a_im
    den = lam_re * lam_re + lam_im * lam_im
    f_re = (nr * lam_re + ni * lam_im) / den
    f_im = (ni * lam_re - nr * lam_im) / den
    br, bi = b_re.astype(F32), b_im.astype(F32)
    bb_re = f_re[..., None] * br - f_im[..., None] * bi
    bb_im = f_re[..., None] * bi + f_im[..., None] * br
    k = jnp.arange(L + 1, dtype=F32)[:, None]
    pmag = jnp.exp(lam_re[..., None, :] * dt[..., None, :] * k)
    parg = lam_im[..., None, :] * dt[..., None, :] * k
    p_re = pmag * jnp.cos(parg)
    p_im = pmag * jnp.sin(parg)
    both = lambda fwd, bwd: jnp.concatenate([fwd, bwd], axis=-1)
    reim = lambda re, im: jnp.stack([re, im], axis=2)
    pw = reim(both(p_re[:, 0, :, L - 1::-1][:, :, :L], p_re[:, 1, :, :L]),
              both(p_im[:, 0, :, L - 1::-1][:, :, :L], p_im[:, 1, :, :L]))
    pv = reim(both(p_re[:, 0, :, 1:], p_re[:, 1, :, :0:-1]), both(p_im[:, 0, :, 1:], p_im[:, 1, :, :0:-1]))
    pa = jnp.stack([both(p_re[:, 0, :, L], p_re[:, 1, :, L]), both(p_im[:, 0, :, L], p_im[:, 1, :, L])], axis=2)
    sw = lambda z: jnp.swapaxes(z, -1, -2)
    bbt = reim(both(sw(bb_re[:, 0]), sw(bb_re[:, 1])), both(sw(bb_im[:, 0]), sw(bb_im[:, 1])))
    cr, ci = c_re.astype(F32), c_im.astype(F32)
    cc = reim(both(cr[:, 0], cr[:, 1]), both(ci[:, 0], ci[:, 1]))
    z = jnp.zeros_like(cr[:, 0])
    csel = jnp.concatenate([jnp.concatenate([cr[:, 0], z, -ci[:, 0], z], axis=-1),
                            jnp.concatenate([z, cr[:, 1], z, -ci[:, 1]], axis=-1)], axis=-2)
    dcol = jnp.broadcast_to(d_skip.astype(F32).reshape(depth, ng, H, 1), (depth, ng, H, LANES))
    f = L * H
    ns = 2 * N_DIR * STATE
    gb = 8
    blk = lambda *shape: pl.BlockSpec((None, gb) + shape, lambda l, j: (l, j) + (0,) * len(shape))
    return pl.pallas_call(
        functools.partial(_ops_kernel, seg_steps),
        grid=(depth, ng // gb),
        in_specs=[blk(2, L, ns // 2), blk(2, L, ns // 2), blk(2, ns // 2), blk(2, H, ns // 2),
                  blk(2, H, ns // 2), blk(2 * H, ns), blk(H, LANES)],
        out_specs=[blk(f, f), blk(ns, f), blk(f, ns), blk(4, ns // 2)],
        out_shape=[jax.ShapeDtypeStruct((depth, ng, f, f), BF16), jax.ShapeDtypeStruct((depth, ng, ns, f), BF16),
                   jax.ShapeDtypeStruct((depth, ng, f, ns), BF16), jax.ShapeDtypeStruct((depth, ng, 4, ns // 2), F32)],
        scratch_shapes=[pltpu.VMEM((f, ns), F32)],
        compiler_params=pltpu.CompilerParams(
            dimension_semantics=("parallel", "parallel"), vmem_limit_bytes=VMEM_LIMIT_BYTES),
        name="ssm_operators",
    )(pw, pv, pa, bbt, cc, csel, dcol)


def _gelu_tanh(x):
    return 0.5 * x * (1.0 + jnp.tanh(np.sqrt(2.0 / np.pi).astype(np.float32)
                                     * (x + np.float32(0.044715) * (x * x * x))))


def _mixer_kernel(ctx_tiles, ctx_len, x_ref, ysl_ref, ysc_ref, mod_ref, g_ref, win_ref, bin_ref, cw_ref,
                  woa_ref, wglu_ref, bglu_ref, wo_ref, o_ref):
    tm, d = x_ref.shape
    dc = woa_ref.shape[0]
    is_ctx = pl.program_id(1) < ctx_tiles
    row = jnp.where(is_ctx, 2, pl.program_id(0))
    x = x_ref[...]
    h = _norm_mod(x, g_ref[...], _mod_vec(mod_ref, row, 0, d), _mod_vec(mod_ref, row, 1, d))
    z = jnp.dot(h.astype(BF16), win_ref[...], preferred_element_type=F32) + bin_ref[...]
    g_b, g_c, x_in = z[:, :dc], z[:, dc:2 * dc], z[:, 2 * dc:3 * dc]
    gate_a, gate_b = z[:, 3 * dc:3 * dc + d], z[:, 3 * dc + d:]
    v = g_c * x_in
    t = lax.broadcasted_iota(jnp.int32, (tm, 1), 0)
    seg = jnp.where(is_ctx, ctx_len - 1, GRID_W - 1)
    pos = t & seg
    v_prev = jnp.where(pos == 0, 0.0, pltpu.roll(v, 1, 0))
    v_next = jnp.where(pos == seg, 0.0, pltpu.roll(v, tm - 1, 0))
    cv = cw_ref[0:1, :] * v_prev + cw_ref[1:2, :] * v + cw_ref[2:3, :] * v_next
    y_a = jnp.dot((g_b * cv).astype(BF16), woa_ref[...], preferred_element_type=F32)
    ys_ctx = jnp.concatenate([ysc_ref[...]] * (tm // ctx_len), axis=0)
    ys = jnp.where(is_ctx, ys_ctx, ysl_ref[:, :SSM_CHUNK, :].reshape(tm, ysc_ref.shape[-1]))
    s = _gelu_tanh(ys)
    gl = jnp.dot(s.astype(BF16), wglu_ref[...], preferred_element_type=F32) + bglu_ref[...]
    y_b = gl[:, :d] * jax.nn.sigmoid(gl[:, d:])
    merged = jax.nn.sigmoid(gate_a) * y_a + jax.nn.sigmoid(gate_b) * y_b
    out = jnp.dot(merged.astype(BF16), wo_ref[...], preferred_element_type=F32)
    o_ref[...] = x + _mod_vec(mod_ref, row, 2, d) * out


def _mixer(xs, ys_lat, ys_ctx, mod, g, w_in, b_in, conv_w, w_out_a, w_glu, b_glu, w_o, ctx_tiles, ctx_len):
    bsz, s, d = xs.shape
    n = ys_ctx.shape[-1]
    steps = ys_lat.shape[1]
    cpt, index = _chunk_tile_spec(n, ctx_tiles)
    tok = lambda w: pl.BlockSpec((None, TOK_TILE, w), lambda b, i: (b, i, 0))
    consts = (mod, g, w_in, b_in, conv_w, w_out_a, w_glu, b_glu, w_o)
    return pl.pallas_call(
        functools.partial(_mixer_kernel, ctx_tiles, ctx_len),
        grid=(bsz, s // TOK_TILE),
        in_specs=[tok(d),
                  pl.BlockSpec((None, cpt, None, CHUNK_PITCH, n), lambda b, i: index(b, i, steps // cpt)),
                  pl.BlockSpec((None, ctx_len, n), lambda b, i: (b, 0, 0))]
        + [_const_spec(c.shape) for c in consts],
        out_specs=tok(d),
        out_shape=jax.ShapeDtypeStruct((bsz, s, d), F32),
        compiler_params=pltpu.CompilerParams(
            dimension_semantics=("parallel", "parallel"), vmem_limit_bytes=VMEM_LIMIT_BYTES),
        name="mixer",
    )(xs, ys_lat, ys_ctx, *consts)


def _ffn_kernel(ctx_tiles, tile_offset, final, x_ref, mod_ref, g_ref, win_ref, wout_ref, *rest):
    d = x_ref.shape[-1]
    dff = wout_ref.shape[0]
    is_ctx = pl.program_id(1) + tile_offset < ctx_tiles
    row = jnp.where(is_ctx, 2, pl.program_id(0))
    x = x_ref[...]
    h = _norm_mod(x, g_ref[...], _mod_vec(mod_ref, row, 3, d), _mod_vec(mod_ref, row, 4, d))
    z = jnp.dot(h.astype(BF16), win_ref[...], preferred_element_type=F32)
    gate, up = z[:, :dff], z[:, dff:]
    act = gate * jax.nn.sigmoid(gate) * up
    out = jnp.dot(act.astype(BF16), wout_ref[...], preferred_element_type=F32)
    y = x + _mod_vec(mod_ref, row, 5, d) * out
    if final:
        fg_ref, o_ref = rest
        ms = jnp.mean(y * y, axis=-1, keepdims=True)
        o_ref[...] = y * lax.rsqrt(ms + RMS_EPS) * fg_ref[...]
    else:
        modn_ref, gn_ref, wu_ref, bu_ref, o_ref, ul_ref, uc_ref = rest
        o_ref[...] = y
        _emit_ssm_inputs(y, row, is_ctx, modn_ref, gn_ref, wu_ref, bu_ref, ul_ref, uc_ref)


def _ffn(xs, mod, g, w_ff_in, w_ff_out, ctx_tiles, ctx_len, final_g=None, nxt=None):
    bsz, s, d = xs.shape
    final = final_g is not None
    tile_offset = ctx_tiles if final else 0
    s_out = s - tile_offset * TOK_TILE
    consts = (mod, g, w_ff_in, w_ff_out) + ((final_g,) if final else tuple(nxt))
    out_specs = [pl.BlockSpec((None, TOK_TILE, d), lambda b, i: (b, i, 0))]
    out_shapes = [jax.ShapeDtypeStruct((bsz, s_out, d), F32)]
    if not final:
        u_specs, u_shapes = _ssm_input_specs(bsz, s, nxt[2].shape[1], ctx_tiles, ctx_len)
        out_specs, out_shapes = out_specs + u_specs, out_shapes + u_shapes
    return pl.pallas_call(
        functools.partial(_ffn_kernel, ctx_tiles, tile_offset, final),
        grid=(bsz, s_out // TOK_TILE),
        in_specs=[pl.BlockSpec((None, TOK_TILE, d), lambda b, i: (b, i + tile_offset, 0))]
        + [_const_spec(c.shape) for c in consts],
        out_specs=out_specs,
        out_shape=out_shapes,
        compiler_params=pltpu.CompilerParams(
            dimension_semantics=("arbitrary", "arbitrary"), vmem_limit_bytes=VMEM_LIMIT_BYTES),
        name="ffn_final" if final else "ffn",
    )(xs, *consts)


def kernel(x, c, ctx, c_ctx, w_mod, b_mod, norm1_g, norm2_g, w_in, b_in, conv_w, w_out_a, lam_re, lam_im,
           log_dt, b_re, b_im, c_re, c_im, d_skip, w_glu, b_glu, w_o, w_ff_in, w_ff_out, final_g):
    bsz, seq, d = x.shape
    depth = w_mod.shape[0]
    ctx_len = ctx.shape[1]
    d_conv = conv_w.shape[-1]
    d_ssm = d_skip.shape[-1]
    u_lo, u_hi = 3 * d_conv, 3 * d_conv + d_ssm
    assert TOK_TILE % ctx_len == 0 and TOK_TILE % GRID_W == 0
    assert ctx_len == SSM_CHUNK * SUBLANES and d_ssm % LANES == 0 and bsz <= 2
    assert seq % (SUBLANES * TOK_TILE) == 0
    seg_steps = seq // (SSM_CHUNK * SUBLANES)
    ctx_tiles = 1

    cvec = jnp.zeros((8, d), F32).at[:bsz].set(c.astype(F32)).at[2].set(c_ctx.astype(F32))
    mod = _mod_all(cvec, w_mod, b_mod)

    mt_op, wt_op, vt_op, a_op = _ssm_operators(lam_re, lam_im, log_dt, b_re, b_im, c_re, c_im, d_skip,
                                               seg_steps)

    w_u = w_in[:, :, u_lo:u_hi].astype(BF16)
    b_u = b_in[:, None, u_lo:u_hi]
    w_rest = jnp.concatenate([w_in[:, :, :u_lo], w_in[:, :, u_hi:]], axis=-1).astype(BF16)
    b_rest = jnp.concatenate([b_in[:, None, :u_lo], b_in[:, None, u_hi:]], axis=-1)
    w_out_a, w_glu, w_o = w_out_a.astype(BF16), w_glu.astype(BF16), w_o.astype(BF16)
    w_ff_in, w_ff_out = w_ff_in.astype(BF16), w_ff_out.astype(BF16)
    fg = final_g.reshape(1, d)

    xs = jnp.concatenate([ctx, jnp.zeros((bsz, TOK_TILE - ctx_len, d), x.dtype), x], axis=1)
    ssm_in = lambda l: (mod[l], norm1_g[l].reshape(1, d), w_u[l], b_u[l])
    u_lat, u_ctx = _uproj(xs, *ssm_in(0), ctx_tiles, ctx_len)
    for l in range(depth):
        g1, g2 = norm1_g[l].reshape(1, d), norm2_g[l].reshape(1, d)
        ys_lat, ys_ctx = _ssm(u_lat, u_ctx, mt_op[l], wt_op[l], vt_op[l], a_op[l])
        xs = _mixer(xs, ys_lat, ys_ctx, mod[l], g1, w_rest[l], b_rest[l], conv_w[l], w_out_a[l], w_glu[l],
                    b_glu[l, None], w_o[l], ctx_tiles, ctx_len)
        if l + 1 < depth:
            xs, u_lat, u_ctx = _ffn(xs, mod[l], g2, w_ff_in[l], w_ff_out[l], ctx_tiles, ctx_len,
                                    nxt=ssm_in(l + 1))
        else:
            xs, = _ffn(xs, mod[l], g2, w_ff_in[l], w_ff_out[l], ctx_tiles, ctx_len, final_g=fg)
    return xs
```

```python
import functools

import numpy as np
import jax
import jax.numpy as jnp
from jax import lax
from jax.experimental import pallas as pl
from jax.experimental.pallas import tpu as pltpu

GRID_W = 64
CONV_K = 3
SSM_GROUP = 16
STATE = 64
N_DIR = 2
RMS_EPS = 1e-6

TOK_TILE = 512
SSM_CHUNK = 32
CHUNK_PITCH = 40
SUBLANES = 8
LANES = 128
SSM_LANE_GROUPS = LANES // SSM_GROUP
SSM_SCAN_GROUPS = 4
VMEM_LIMIT_BYTES = 56 * 1024 * 1024

F32 = jnp.float32
BF16 = jnp.bfloat16


def _layer_spec(arr, l, cols=None):
    _, rows, n = arr.shape
    start, width = (0, n) if cols is None else cols
    assert start % width == 0
    return pl.BlockSpec((None, rows, width), lambda *_: (l, 0, start // width), pipeline_mode=pl.Buffered(1))


def _first_tile(x_ref, c_ref):
    pad = jnp.zeros((x_ref.shape[0] - c_ref.shape[0], x_ref.shape[1]), x_ref.dtype)
    return jnp.concatenate([c_ref[...], pad], axis=0)


def _first_specs(d, ctx_tiles, ctx_len):
    return [pl.BlockSpec((None, TOK_TILE, d), lambda b, i: (b, jnp.maximum(i - ctx_tiles, 0), 0)),
            pl.BlockSpec((None, ctx_len, d), lambda b, i: (b, 0, 0))]


def _norm_mod(x, g, shift, scale):
    ms = jnp.mean(x * x, axis=-1, keepdims=True)
    y = x * lax.rsqrt(ms + RMS_EPS) * g
    return y * (1.0 + scale) + shift


def _mod_vec(mod_ref, row, k, d):
    return mod_ref[pl.ds(row, 1), k * d:(k + 1) * d]


def _chunk_tile_spec(n, ctx_tiles):
    cpt = TOK_TILE // SSM_CHUNK

    def index(b, i, steps_per_seg_tiles):
        k = jnp.maximum(i - ctx_tiles, 0)
        return (b, k % steps_per_seg_tiles, k // steps_per_seg_tiles, 0, 0)

    return cpt, index


def _mod_kernel(c_ref, w_ref, b_ref, o_ref):
    c = c_ref[...]
    s = c * jax.nn.sigmoid(c)
    o_ref[...] = jnp.dot(s.astype(BF16), w_ref[...].astype(BF16),
                         preferred_element_type=F32) + b_ref[...]


def _mod_all(cvec, w_mod, b_mod):
    depth, d, n = w_mod.shape
    tn = 1536
    return pl.pallas_call(
        _mod_kernel,
        grid=(depth, n // tn),
        in_specs=[pl.BlockSpec((8, d), lambda l, j: (0, 0)),
                  pl.BlockSpec((None, d, tn), lambda l, j: (l, 0, j)),
                  pl.BlockSpec((None, 1, tn), lambda l, j: (l, 0, j))],
        out_specs=pl.BlockSpec((None, 8, tn), lambda l, j: (l, 0, j)),
        out_shape=jax.ShapeDtypeStruct((depth, 8, n), F32),
        compiler_params=pltpu.CompilerParams(
            dimension_semantics=("arbitrary", "arbitrary"), vmem_limit_bytes=VMEM_LIMIT_BYTES),
        name="adaln_mod",
    )(cvec, w_mod, b_mod.reshape(depth, 1, n))


def _emit_ssm_inputs(x, row, is_ctx, mod_ref, g_ref, w_ref, b_ref, ul_ref, uc_ref):
    d = x.shape[-1]
    h = _norm_mod(x, g_ref[...], _mod_vec(mod_ref, row, 0, d), _mod_vec(mod_ref, row, 1, d))
    u = jnp.dot(h.astype(BF16), w_ref[...], preferred_element_type=F32) + b_ref[...]
    cpt, pitch, n = ul_ref.shape
    ul_ref[:, :SSM_CHUNK, :] = u.reshape(cpt, SSM_CHUNK, n)
    ul_ref[:, SSM_CHUNK:, :] = jnp.zeros((cpt, pitch - SSM_CHUNK, n), F32)

    @pl.when(is_ctx)
    def _():
        uc_ref[...] = u[:uc_ref.shape[0]]


def _uproj_kernel(ctx_tiles, x_ref, c_ref, mod_ref, g_ref, w_ref, b_ref, ul_ref, uc_ref):
    is_ctx = pl.program_id(1) < ctx_tiles
    row = jnp.where(is_ctx, 2, pl.program_id(0))
    x = jnp.where(is_ctx, _first_tile(x_ref, c_ref), x_ref[...])
    _emit_ssm_inputs(x, row, is_ctx, mod_ref, g_ref, w_ref, b_ref, ul_ref, uc_ref)


def _ssm_input_specs(bsz, s, n, ctx_tiles, ctx_len):
    steps = (s - ctx_tiles * TOK_TILE) // (SSM_CHUNK * SUBLANES)
    cpt, index = _chunk_tile_spec(n, ctx_tiles)
    specs = [pl.BlockSpec((None, cpt, None, CHUNK_PITCH, n), lambda b, i: index(b, i, steps // cpt)),
             pl.BlockSpec((None, ctx_len, n), lambda b, i: (b, 0, 0))]
    shapes = [jax.ShapeDtypeStruct((bsz, steps, SUBLANES, CHUNK_PITCH, n), F32),
              jax.ShapeDtypeStruct((bsz, ctx_len, n), F32)]
    return specs, shapes


def _uproj(x, ctx, ssm_in, ctx_tiles):
    bsz, seq, d = x.shape
    ctx_len = ctx.shape[1]
    s = ctx_tiles * TOK_TILE + seq
    arrays, specs, n_u = ssm_in
    out_specs, out_shapes = _ssm_input_specs(bsz, s, n_u, ctx_tiles, ctx_len)
    return pl.pallas_call(
        functools.partial(_uproj_kernel, ctx_tiles),
        grid=(bsz, s // TOK_TILE),
        in_specs=_first_specs(d, ctx_tiles, ctx_len) + specs,
        out_specs=out_specs,
        out_shape=out_shapes,
        compiler_params=pltpu.CompilerParams(
            dimension_semantics=("arbitrary", "arbitrary"), vmem_limit_bytes=VMEM_LIMIT_BYTES),
        name="ssm_uproj",
    )(x, ctx, *arrays)


def _cmul(a_re, a_im, b_re, b_im):
    return a_re * b_re - a_im * b_im, a_re * b_im + a_im * b_re


def _ssm_kernel(ctx_chunks, ul_ref, uc_ref, mt_ref, wt_ref, vt_ref, a_ref, yl_ref, yc_ref,
                ut_scr, yt_scr, x_scr, hf_scr, hb_scr):
    L = SSM_CHUNK
    H = SSM_GROUP
    sub = SUBLANES
    n_lat = ul_ref.shape[0] // CHUNK_PITCH
    n_all = ut_scr.shape[-1]
    steps = n_lat // sub
    half = STATE * N_DIR
    gl = SSM_LANE_GROUPS
    gs = SSM_SCAN_GROUPS

    pad = jnp.zeros((n_all - n_lat - ctx_chunks, LANES), F32)

    def fill(t, carry):
        lat = ul_ref[pl.ds(t, n_lat, stride=CHUNK_PITCH), :]
        cx = uc_ref[pl.ds(t, ctx_chunks, stride=L), :]
        at = jnp.concatenate([lat, cx, pad], axis=0).T
        for g in range(gl):
            ut_scr[g, pl.ds(pl.multiple_of(t * H, H), H), :] = at[g * H:(g + 1) * H, :].astype(BF16)
        return carry

    lax.fori_loop(0, L, fill, 0, unroll=4)

    is_fwd = lax.broadcasted_iota(jnp.int32, (sub, half), 1) < STATE
    is_fwd1 = is_fwd[:1]
    sl = lax.broadcasted_iota(jnp.int32, (sub, half), 0)
    blk = lambda j: pl.ds(pl.multiple_of(j * sub, sub), sub)
    zero = jnp.zeros((sub, half), F32)
    lane2 = lax.broadcasted_iota(jnp.int32, (n_all, 2 * half), 1)
    take_fwd = (lane2 % half) < STATE
    tail0 = n_lat + ctx_chunks
    assert tail0 % sub == 0 and ctx_chunks == sub

    for g0 in range(0, gl, gs):
        coef = lambda g, r, rows=sub: jnp.broadcast_to(a_ref[g0 + g, r:r + 1, :], (rows, half))
        a_re = [coef(g, 0) for g in range(gs)]
        a_im = [coef(g, 1) for g in range(gs)]

        for g in range(gs):
            xt = jnp.dot(wt_ref[g0 + g], ut_scr[g0 + g], preferred_element_type=F32)
            x_scr[g] = xt.T
            hf_scr[g, tail0:, :] = jnp.zeros((n_all - tail0, 2 * half), F32)
            hb_scr[g, tail0:, :] = jnp.zeros((n_all - tail0, 2 * half), F32)

        h0 = []
        for g in range(gs):
            xc = x_scr[g, n_lat:n_lat + ctx_chunks, :]
            ar, ai = a_re[g][:1], a_im[g][:1]
            h_re = h_im = jnp.zeros((1, half), F32)
            hf_rows, hb_rows = [], [None] * ctx_chunks
            for k in range(ctx_chunks):
                kb = ctx_chunks - 1 - k
                hf_rows.append((h_re, h_im))
                hb_rows[kb] = (h_re, h_im)
                x_re = jnp.where(is_fwd1, xc[k:k + 1, :half], xc[kb:kb + 1, :half])
                x_im = jnp.where(is_fwd1, xc[k:k + 1, half:], xc[kb:kb + 1, half:])
                p_re, p_im = _cmul(ar, ai, h_re, h_im)
                h_re, h_im = p_re + x_re, p_im + x_im
            cat = lambda rows, part: jnp.concatenate([r[part] for r in rows], axis=0)
            hf_scr[g, n_lat:n_lat + ctx_chunks, :half] = cat(hf_rows, 0)
            hf_scr[g, n_lat:n_lat + ctx_chunks, half:] = cat(hf_rows, 1)
            hb_scr[g, n_lat:n_lat + ctx_chunks, :half] = cat(hb_rows, 0)
            hb_scr[g, n_lat:n_lat + ctx_chunks, half:] = cat(hb_rows, 1)
            h0.append((jnp.broadcast_to(h_re, (sub, half)), jnp.broadcast_to(h_im, (sub, half))))

        def scan_step(j, carry):
            jb = steps - 1 - j
            new = []
            for g in range(gs):
                h_re, h_im = carry[g]
                x_re = jnp.where(is_fwd, x_scr[g, blk(j), :half], x_scr[g, blk(jb), :half])
                x_im = jnp.where(is_fwd, x_scr[g, blk(j), half:], x_scr[g, blk(jb), half:])
                hf_scr[g, blk(j), :half] = h_re
                hf_scr[g, blk(j), half:] = h_im
                hb_scr[g, blk(jb), :half] = h_re
                hb_scr[g, blk(jb), half:] = h_im
                p_re, p_im = _cmul(a_re[g], a_im[g], h_re, h_im)
                new.append((p_re + x_re, p_im + x_im))
            return tuple(new)

        ends = lax.fori_loop(0, steps, scan_step, tuple((zero, zero) for _ in range(gs)), unroll=2)

        carries = []
        for g in range(gs):
            def shift(t, h):
                return jnp.where(is_fwd, jnp.where(sl < 1, h, pltpu.roll(t, 1, 0)),
                                 jnp.where(sl >= sub - 1, h, pltpu.roll(t, sub - 1, 0)))
            e_re, e_im = ends[g]
            s_re, s_im = coef(g, 2), coef(g, 3)
            c_re, c_im = shift(zero, h0[g][0]), shift(zero, h0[g][1])
            for _ in range(sub - 1):
                p_re, p_im = _cmul(s_re, s_im, c_re, c_im)
                c_re, c_im = shift(e_re + p_re, h0[g][0]), shift(e_im + p_im, h0[g][1])
            carries.append((c_re, c_im))

        def fix_step(j, carry):
            jb = steps - 1 - j
            new = []
            for g in range(gs):
                d_re, d_im = carry[g]
                hf_scr[g, blk(j), :half] += d_re
                hf_scr[g, blk(j), half:] += d_im
                hb_scr[g, blk(jb), :half] += d_re
                hb_scr[g, blk(jb), half:] += d_im
                new.append(_cmul(a_re[g], a_im[g], d_re, d_im))
            return tuple(new)

        lax.fori_loop(0, steps, fix_step, tuple(carries), unroll=2)

        for g in range(gs):
            h_in = jnp.where(take_fwd, hf_scr[g], hb_scr[g]).T.astype(BF16)
            yt = (jnp.dot(mt_ref[g0 + g], ut_scr[g0 + g], preferred_element_type=F32)
                  + jnp.dot(vt_ref[g0 + g], h_in, preferred_element_type=F32))
            for t in range(L):
                yt_scr[t, (g0 + g) * H:(g0 + g + 1) * H, :] = yt[t * H:(t + 1) * H, :].astype(BF16)

    def drain(t, carry):
        y = yt_scr[t].astype(F32).T
        yl_ref[pl.ds(t, n_lat, stride=CHUNK_PITCH), :] = y[:n_lat]
        yc_ref[pl.ds(t, ctx_chunks, stride=L), :] = y[n_lat:n_lat + ctx_chunks]
        return carry

    lax.fori_loop(0, L, drain, 0, unroll=4)
    for t in range(L, CHUNK_PITCH):
        yl_ref[pl.ds(t, n_lat, stride=CHUNK_PITCH), :] = jnp.zeros((n_lat, LANES), F32)


def _ssm(u_lat, u_ctx, mt, wt, vt, a, l):
    bsz, steps, nseg, pitch, n = u_lat.shape
    L = SSM_CHUNK
    ctx_len = u_ctx.shape[1]
    ctx_chunks = ctx_len // L
    n_lat = steps * nseg
    n_all = -(-(n_lat + ctx_chunks) // LANES) * LANES
    gl = SSM_LANE_GROUPS
    f = L * SSM_GROUP
    ns = wt.shape[-2]
    lat_spec = pl.BlockSpec((None, n_lat * pitch, LANES), lambda b, j: (b, 0, j), pipeline_mode=pl.Buffered(1))
    ctx_spec = pl.BlockSpec((None, ctx_len, LANES), lambda b, j: (b, 0, j))
    grp = lambda shape: pl.BlockSpec((None, gl) + shape, lambda b, j: (l, j, 0, 0), pipeline_mode=pl.Buffered(1))
    y_lat, y_ctx = pl.pallas_call(
        functools.partial(_ssm_kernel, ctx_chunks),
        grid=(bsz, n // LANES),
        in_specs=[lat_spec, ctx_spec, grp((f, f)), grp((ns, f)), grp((f, ns)), grp((4, ns // 2))],
        out_specs=[lat_spec, ctx_spec],
        out_shape=[jax.ShapeDtypeStruct((bsz, n_lat * pitch, n), F32),
                   jax.ShapeDtypeStruct((bsz, ctx_len, n), F32)],
        scratch_shapes=[pltpu.VMEM((gl, f, n_all), BF16), pltpu.VMEM((L, LANES, n_all), BF16)]
        + [pltpu.VMEM((SSM_SCAN_GROUPS, n_all, ns), F32)] * 3,
        compiler_params=pltpu.CompilerParams(
            dimension_semantics=("arbitrary", "arbitrary"), vmem_limit_bytes=VMEM_LIMIT_BYTES),
        name="ssm_chunked",
    )(u_lat.reshape(bsz, n_lat * pitch, n), u_ctx, mt, wt, vt, a)
    return y_lat.reshape(u_lat.shape), y_ctx


def _cpow(z_re, z_im, n):
    out = None
    while n:
        if n & 1:
            out = (z_re, z_im) if out is None else _cmul(out[0], out[1], z_re, z_im)
        n >>= 1
        if n:
            z_re, z_im = _cmul(z_re, z_im, z_re, z_im)
    return out


def _ops_kernel(seg_steps, pw_ref, pv_ref, pa_ref, bbt_ref, c_ref, csel_ref, dcol_ref,
                mt_ref, wt_ref, vt_ref, a_ref, w_scr):
    gb = mt_ref.shape[0]
    f = mt_ref.shape[-1]
    H = SSM_GROUP
    L = f // H
    half = STATE * N_DIR
    lane = lax.broadcasted_iota(jnp.int32, (H, 2 * f), 1)
    row = lax.broadcasted_iota(jnp.int32, (H, 2 * f), 0)
    centre = lane == (L - 1) * H + row
    zeros = jnp.zeros((H, f), F32)
    for g in range(gb):
        bt_re, bt_im = bbt_ref[g, 0], bbt_ref[g, 1]
        c_re, c_im = c_ref[g, 0], c_ref[g, 1]
        for t in range(L):
            rows = slice(t * H, (t + 1) * H)
            w_re, w_im = _cmul(pw_ref[g, 0, t:t + 1, :], pw_ref[g, 1, t:t + 1, :], bt_re, bt_im)
            w_scr[rows, :half] = w_re
            w_scr[rows, half:] = w_im
            g_re, g_im = _cmul(pv_ref[g, 0, t:t + 1, :], pv_ref[g, 1, t:t + 1, :], c_re, c_im)
            vt_ref[g, rows, :half] = g_re.astype(vt_ref.dtype)
            vt_ref[g, rows, half:] = (-g_im).astype(vt_ref.dtype)
        wt = w_scr[...].T
        wt_ref[g] = wt.astype(wt_ref.dtype)
        kk = jnp.dot(csel_ref[g], wt, preferred_element_type=F32, precision=lax.Precision.HIGHEST)
        kf = jnp.concatenate([kk[:H], zeros], axis=1)
        kb = pltpu.roll(jnp.concatenate([kk[H:], zeros], axis=1), (L - 1) * H, 1)
        dmat = jnp.concatenate([dcol_ref[g]] * (2 * f // LANES), axis=1)
        k = kf + kb + jnp.where(centre, dmat, 0.0)
        for t in range(L):
            off = f - (t + 1) * H
            win = k if off == 0 else pltpu.roll(k, 2 * f - off, 1)
            mt_ref[g, t * H:(t + 1) * H, :] = win[:, :f].astype(mt_ref.dtype)
        ac_re, ac_im = pa_ref[g, 0:1, :], pa_ref[g, 1:2, :]
        as_re, as_im = _cpow(ac_re, ac_im, seg_steps)
        a_ref[g, 0:1, :] = ac_re
        a_ref[g, 1:2, :] = ac_im
        a_ref[g, 2:3, :] = as_re
        a_ref[g, 3:4, :] = as_im


def _ssm_operators(lam_re, lam_im, log_dt, b_re, b_im, c_re, c_im, d_skip, seg_steps):
    L = SSM_CHUNK
    H = SSM_GROUP
    lam_re = lam_re.astype(F32)
    lam_im = lam_im.astype(F32)
    depth, _, ng, _ = lam_re.shape
    dt = jnp.exp(log_dt.astype(F32))[..., None]
    mag = jnp.exp(lam_re * dt)
    a_re = mag * jnp.cos(lam_im * dt)
    a_im = mag * jnp.sin(lam_im * dt)
    nr, ni = a_re - 1.0, a_im
    den = lam_re * lam_re + lam_im * lam_im
    f_re = (nr * lam_re + ni * lam_im) / den
    f_im = (ni * lam_re - nr * lam_im) / den
    br, bi = b_re.astype(F32), b_im.astype(F32)
    bb_re = f_re[..., None] * br - f_im[..., None] * bi
    bb_im = f_re[..., None] * bi + f_im[..., None] * br
    k = jnp.arange(L + 1, dtype=F32)[:, None]
    pmag = jnp.exp(lam_re[..., None, :] * dt[..., None, :] * k)
    parg = lam_im[..., None, :] * dt[..., None, :] * k
    p_re = pmag * jnp.cos(parg)
    p_im = pmag * jnp.sin(parg)
    both = lambda fwd, bwd: jnp.concatenate([fwd, bwd], axis=-1)
    reim = lambda re, im: jnp.stack([re, im], axis=2)
    pw = reim(both(p_re[:, 0, :, L - 1::-1][:, :, :L], p_re[:, 1, :, :L]),
              both(p_im[:, 0, :, L - 1::-1][:, :, :L], p_im[:, 1, :, :L]))
    pv = reim(both(p_re[:, 0, :, 1:], p_re[:, 1, :, :0:-1]), both(p_im[:, 0, :, 1:], p_im[:, 1, :, :0:-1]))
    pa = jnp.stack([both(p_re[:, 0, :, L], p_re[:, 1, :, L]), both(p_im[:, 0, :, L], p_im[:, 1, :, L])], axis=2)
    sw = lambda z: jnp.swapaxes(z, -1, -2)
    bbt = reim(both(sw(bb_re[:, 0]), sw(bb_re[:, 1])), both(sw(bb_im[:, 0]), sw(bb_im[:, 1])))
    cr, ci = c_re.astype(F32), c_im.astype(F32)
    cc = reim(both(cr[:, 0], cr[:, 1]), both(ci[:, 0], ci[:, 1]))
    z = jnp.zeros_like(cr[:, 0])
    csel = jnp.concatenate([jnp.concatenate([cr[:, 0], z, -ci[:, 0], z], axis=-1),
                            jnp.concatenate([z, cr[:, 1], z, -ci[:, 1]], axis=-1)], axis=-2)
    dcol = jnp.broadcast_to(d_skip.astype(F32).reshape(depth, ng, H, 1), (depth, ng, H, LANES))
    f = L * H
    ns = 2 * N_DIR * STATE
    gb = 8
    blk = lambda *shape: pl.BlockSpec((None, gb) + shape, lambda l, j: (l, j) + (0,) * len(shape))
    return pl.pallas_call(
        functools.partial(_ops_kernel, seg_steps),
        grid=(depth, ng // gb),
        in_specs=[blk(2, L, ns // 2), blk(2, L, ns // 2), blk(2, ns // 2), blk(2, H, ns // 2),
                  blk(2, H, ns // 2), blk(2 * H, ns), blk(H, LANES)],
        out_specs=[blk(f, f), blk(ns, f), blk(f, ns), blk(4, ns // 2)],
        out_shape=[jax.ShapeDtypeStruct((depth, ng, f, f), BF16), jax.ShapeDtypeStruct((depth, ng, ns, f), BF16),
                   jax.ShapeDtypeStruct((depth, ng, f, ns), BF16), jax.ShapeDtypeStruct((depth, ng, 4, ns // 2), F32)],
        scratch_shapes=[pltpu.VMEM((f, ns), F32)],
        compiler_params=pltpu.CompilerParams(
            dimension_semantics=("parallel", "parallel"), vmem_limit_bytes=VMEM_LIMIT_BYTES),
        name="ssm_operators",
    )(pw, pv, pa, bbt, cc, csel, dcol)


def _gelu_tanh(x):
    return 0.5 * x * (1.0 + jnp.tanh(np.sqrt(2.0 / np.pi).astype(np.float32)
                                     * (x + np.float32(0.044715) * (x * x * x))))


def _mixer_kernel(ctx_tiles, ctx_len, first, x_ref, *refs):
    if first:
        c_ref, *refs = refs
    (ysl_ref, ysc_ref, mod_ref, g_ref, wa_ref, ba_ref, wg_ref, bg_ref, cw_ref,
     woa_ref, wglu_ref, bglu_ref, wo_ref, o_ref) = refs
    tm, d = x_ref.shape
    dc = woa_ref.shape[0]
    is_ctx = pl.program_id(1) < ctx_tiles
    row = jnp.where(is_ctx, 2, pl.program_id(0))
    x = jnp.where(is_ctx, _first_tile(x_ref, c_ref), x_ref[...]) if first else x_ref[...]
    h = _norm_mod(x, g_ref[...], _mod_vec(mod_ref, row, 0, d), _mod_vec(mod_ref, row, 1, d)).astype(BF16)
    za = jnp.dot(h, wa_ref[...], preferred_element_type=F32) + ba_ref[...]
    zg = jnp.dot(h, wg_ref[...], preferred_element_type=F32) + bg_ref[...]
    g_b, g_c, x_in = za[:, :dc], za[:, dc:2 * dc], za[:, 2 * dc:]
    gate_a, gate_b = zg[:, :d], zg[:, d:]
    v = g_c * x_in
    t = lax.broadcasted_iota(jnp.int32, (tm, 1), 0)
    seg = jnp.where(is_ctx, ctx_len - 1, GRID_W - 1)
    pos = t & seg
    v_prev = jnp.where(pos == 0, 0.0, pltpu.roll(v, 1, 0))
    v_next = jnp.where(pos == seg, 0.0, pltpu.roll(v, tm - 1, 0))
    cv = cw_ref[0:1, :] * v_prev + cw_ref[1:2, :] * v + cw_ref[2:3, :] * v_next
    y_a = jnp.dot((g_b * cv).astype(BF16), woa_ref[...], preferred_element_type=F32)
    ys_ctx = jnp.concatenate([ysc_ref[...]] * (tm // ctx_len), axis=0)
    ys = jnp.where(is_ctx, ys_ctx, ysl_ref[:, :SSM_CHUNK, :].reshape(tm, ysc_ref.shape[-1]))
    s = _gelu_tanh(ys)
    gl = jnp.dot(s.astype(BF16), wglu_ref[...], preferred_element_type=F32) + bglu_ref[...]
    y_b = gl[:, :d] * jax.nn.sigmoid(gl[:, d:])
    merged = jax.nn.sigmoid(gate_a) * y_a + jax.nn.sigmoid(gate_b) * y_b
    out = jnp.dot(merged.astype(BF16), wo_ref[...], preferred_element_type=F32)
    o_ref[...] = x + _mod_vec(mod_ref, row, 2, d) * out


def _mixer(tokens, ys_lat, ys_ctx, params, ctx_tiles, ctx_len):
    first = len(tokens) == 2
    bsz, _, d = tokens[0].shape
    s = tokens[0].shape[1] + (ctx_tiles * TOK_TILE if first else 0)
    n = ys_ctx.shape[-1]
    steps = ys_lat.shape[1]
    cpt, index = _chunk_tile_spec(n, ctx_tiles)
    tok = pl.BlockSpec((None, TOK_TILE, d), lambda b, i: (b, i, 0))
    arrays, specs = params
    return pl.pallas_call(
        functools.partial(_mixer_kernel, ctx_tiles, ctx_len, first),
        grid=(bsz, s // TOK_TILE),
        in_specs=(_first_specs(d, ctx_tiles, ctx_len) if first else [tok])
        + [pl.BlockSpec((None, cpt, None, CHUNK_PITCH, n), lambda b, i: index(b, i, steps // cpt)),
           pl.BlockSpec((None, ctx_len, n), lambda b, i: (b, 0, 0))]
        + specs,
        out_specs=tok,
        out_shape=jax.ShapeDtypeStruct((bsz, s, d), F32),
        compiler_params=pltpu.CompilerParams(
            dimension_semantics=("parallel", "parallel"), vmem_limit_bytes=VMEM_LIMIT_BYTES),
        name="mixer",
    )(*tokens, ys_lat, ys_ctx, *arrays)


def _ffn_kernel(ctx_tiles, tile_offset, final, x_ref, mod_ref, g_ref, win_ref, wout_ref, *rest):
    d = x_ref.shape[-1]
    dff = wout_ref.shape[0]
    is_ctx = pl.program_id(1) + tile_offset < ctx_tiles
    row = jnp.where(is_ctx, 2, pl.program_id(0))
    x = x_ref[...]
    h = _norm_mod(x, g_ref[...], _mod_vec(mod_ref, row, 3, d), _mod_vec(mod_ref, row, 4, d))
    z = jnp.dot(h.astype(BF16), win_ref[...], preferred_element_type=F32)
    gate, up = z[:, :dff], z[:, dff:]
    act = gate * jax.nn.sigmoid(gate) * up
    out = jnp.dot(act.astype(BF16), wout_ref[...], preferred_element_type=F32)
    y = x + _mod_vec(mod_ref, row, 5, d) * out
    if final:
        fg_ref, o_ref = rest
        ms = jnp.mean(y * y, axis=-1, keepdims=True)
        o_ref[...] = y * lax.rsqrt(ms + RMS_EPS) * fg_ref[...]
    else:
        modn_ref, gn_ref, wu_ref, bu_ref, o_ref, ul_ref, uc_ref = rest
        o_ref[...] = y
        _emit_ssm_inputs(y, row, is_ctx, modn_ref, gn_ref, wu_ref, bu_ref, ul_ref, uc_ref)


def _ffn(xs, params, ctx_tiles, ctx_len, final_g=None, nxt=None):
    bsz, s, d = xs.shape
    final = final_g is not None
    tile_offset = ctx_tiles if final else 0
    s_out = s - tile_offset * TOK_TILE
    arrays, specs = params
    if final:
        arrays, specs = arrays + [final_g], specs + [_layer_spec(final_g, 0)]
    else:
        arrays, specs = arrays + nxt[0], specs + nxt[1]
    out_specs = [pl.BlockSpec((None, TOK_TILE, d), lambda b, i: (b, i, 0))]
    out_shapes = [jax.ShapeDtypeStruct((bsz, s_out, d), F32)]
    if not final:
        u_specs, u_shapes = _ssm_input_specs(bsz, s, nxt[2], ctx_tiles, ctx_len)
        out_specs, out_shapes = out_specs + u_specs, out_shapes + u_shapes
    return pl.pallas_call(
        functools.partial(_ffn_kernel, ctx_tiles, tile_offset, final),
        grid=(bsz, s_out // TOK_TILE),
        in_specs=[pl.BlockSpec((None, TOK_TILE, d), lambda b, i: (b, i + tile_offset, 0))] + specs,
        out_specs=out_specs,
        out_shape=out_shapes,
        compiler_params=pltpu.CompilerParams(
            dimension_semantics=("arbitrary", "arbitrary"), vmem_limit_bytes=VMEM_LIMIT_BYTES),
        name="ffn_final" if final else "ffn",
    )(xs, *arrays)


def kernel(x, c, ctx, c_ctx, w_mod, b_mod, norm1_g, norm2_g, w_in, b_in, conv_w, w_out_a, lam_re, lam_im,
           log_dt, b_re, b_im, c_re, c_im, d_skip, w_glu, b_glu, w_o, w_ff_in, w_ff_out, final_g):
    bsz, seq, d = x.shape
    depth = w_mod.shape[0]
    ctx_len = ctx.shape[1]
    d_conv = conv_w.shape[-1]
    d_ssm = d_skip.shape[-1]
    u_lo, u_hi = 3 * d_conv, 3 * d_conv + d_ssm
    assert TOK_TILE % ctx_len == 0 and TOK_TILE % GRID_W == 0
    assert ctx_len == SSM_CHUNK * SUBLANES and d_ssm % LANES == 0 and bsz <= 2
    assert seq % (SUBLANES * TOK_TILE) == 0
    seg_steps = seq // (SSM_CHUNK * SUBLANES)
    ctx_tiles = 1

    cvec = jnp.zeros((8, d), F32).at[:bsz].set(c.astype(F32)).at[2].set(c_ctx.astype(F32))
    mod = _mod_all(cvec, w_mod, b_mod)

    mt_op, wt_op, vt_op, a_op = _ssm_operators(lam_re, lam_im, log_dt, b_re, b_im, c_re, c_im, d_skip,
                                               seg_steps)

    w_in, w_out_a, w_glu, w_o = (w.astype(BF16) for w in (w_in, w_out_a, w_glu, w_o))
    w_ff_in, w_ff_out = w_ff_in.astype(BF16), w_ff_out.astype(BF16)
    row = lambda v: v.reshape(v.shape[0], 1, v.shape[-1])
    b_in, b_glu, g1, g2, fg = row(b_in), row(b_glu), row(norm1_g), row(norm2_g), final_g.reshape(1, 1, d)
    gates_lo = u_hi

    def operands(l, *items):
        arrays = [arr for arr, _ in items]
        return arrays, [_layer_spec(arr, l, cols) for arr, cols in items]

    u_cols = (u_lo, d_ssm)
    ssm_in = lambda l: operands(l, (mod, None), (g1, None), (w_in, u_cols), (b_in, u_cols)) + (d_ssm,)
    mixer_in = lambda l: operands(
        l, (mod, None), (g1, None), (w_in, (0, u_lo)), (b_in, (0, u_lo)),
        (w_in, (gates_lo, 2 * d)), (b_in, (gates_lo, 2 * d)), (conv_w, None), (w_out_a, None),
        (w_glu, None), (b_glu, None), (w_o, None))
    ffn_in = lambda l: operands(l, (mod, None), (g2, None), (w_ff_in, None), (w_ff_out, None))

    u_lat, u_ctx = _uproj(x, ctx, ssm_in(0), ctx_tiles)
    xs = None
    for l in range(depth):
        ys_lat, ys_ctx = _ssm(u_lat, u_ctx, mt_op, wt_op, vt_op, a_op, l)
        xs = _mixer((x, ctx) if l == 0 else (xs,), ys_lat, ys_ctx, mixer_in(l), ctx_tiles, ctx_len)
        if l + 1 < depth:
            xs, u_lat, u_ctx = _ffn(xs, ffn_in(l), ctx_tiles, ctx_len, nxt=ssm_in(l + 1))
        else:
            xs, = _ffn(xs, ffn_in(l), ctx_tiles, ctx_len, final_g=fg)
    return xs
```

```python
import functools

import numpy as np
import jax
import jax.numpy as jnp
from jax import lax
from jax.experimental import pallas as pl
from jax.experimental.pallas import tpu as pltpu

GRID_W = 64
CONV_K = 3
SSM_GROUP = 16
STATE = 64
N_DIR = 2
RMS_EPS = 1e-6

TOK_TILE = 512
SSM_CHUNK = 32
CHUNK_PITCH = 40
SUBLANES = 8
LANES = 128
SSM_LANE_GROUPS = LANES // SSM_GROUP
SSM_SCAN_GROUPS = 4
SSM_IO_PARTS = 4
VMEM_LIMIT_BYTES = 56 * 1024 * 1024

F32 = jnp.float32
BF16 = jnp.bfloat16


def _layer_spec(arr, l, cols=None):
    _, rows, n = arr.shape
    start, width = (0, n) if cols is None else cols
    assert start % width == 0
    return pl.BlockSpec((None, rows, width), lambda *_: (l, 0, start // width), pipeline_mode=pl.Buffered(1))


def _first_tile(x_ref, c_ref):
    pad = jnp.zeros((x_ref.shape[0] - c_ref.shape[0], x_ref.shape[1]), x_ref.dtype)
    return jnp.concatenate([c_ref[...], pad], axis=0)


def _first_specs(d, ctx_tiles, ctx_len):
    return [pl.BlockSpec((None, TOK_TILE, d), lambda b, i: (b, jnp.maximum(i - ctx_tiles, 0), 0)),
            pl.BlockSpec((None, ctx_len, d), lambda b, i: (b, 0, 0))]


def _norm_mod(x, g, shift, scale):
    ms = jnp.mean(x * x, axis=-1, keepdims=True)
    y = x * lax.rsqrt(ms + RMS_EPS) * g
    return y * (1.0 + scale) + shift


def _mod_vec(mod_ref, row, k, d):
    return mod_ref[pl.ds(row, 1), k * d:(k + 1) * d]


def _chunk_tile_spec(n, ctx_tiles):
    cpt = TOK_TILE // SSM_CHUNK

    def index(b, i, steps_per_seg_tiles):
        k = jnp.maximum(i - ctx_tiles, 0)
        return (b, k % steps_per_seg_tiles, k // steps_per_seg_tiles, 0, 0)

    return cpt, index


def _mod_kernel(c_ref, w_ref, b_ref, o_ref):
    c = c_ref[...]
    s = c * jax.nn.sigmoid(c)
    o_ref[...] = jnp.dot(s.astype(BF16), w_ref[...].astype(BF16),
                         preferred_element_type=F32) + b_ref[...]


def _mod_all(cvec, w_mod, b_mod):
    depth, d, n = w_mod.shape
    tn = 1536
    return pl.pallas_call(
        _mod_kernel,
        grid=(depth, n // tn),
        in_specs=[pl.BlockSpec((8, d), lambda l, j: (0, 0)),
                  pl.BlockSpec((None, d, tn), lambda l, j: (l, 0, j)),
                  pl.BlockSpec((None, 1, tn), lambda l, j: (l, 0, j))],
        out_specs=pl.BlockSpec((None, 8, tn), lambda l, j: (l, 0, j)),
        out_shape=jax.ShapeDtypeStruct((depth, 8, n), F32),
        compiler_params=pltpu.CompilerParams(
            dimension_semantics=("arbitrary", "arbitrary"), vmem_limit_bytes=VMEM_LIMIT_BYTES),
        name="adaln_mod",
    )(cvec, w_mod, b_mod.reshape(depth, 1, n))


def _emit_ssm_inputs(x, row, is_ctx, mod_ref, g_ref, w_ref, b_ref, ul_ref, uc_ref):
    d = x.shape[-1]
    h = _norm_mod(x, g_ref[...], _mod_vec(mod_ref, row, 0, d), _mod_vec(mod_ref, row, 1, d))
    u = jnp.dot(h.astype(BF16), w_ref[...], preferred_element_type=F32) + b_ref[...]
    cpt, pitch, n = ul_ref.shape
    ul_ref[:, :SSM_CHUNK, :] = u.reshape(cpt, SSM_CHUNK, n)
    ul_ref[:, SSM_CHUNK:, :] = jnp.zeros((cpt, pitch - SSM_CHUNK, n), F32)

    @pl.when(is_ctx)
    def _():
        uc_ref[...] = u[:uc_ref.shape[0]]


def _uproj_kernel(ctx_tiles, x_ref, c_ref, mod_ref, g_ref, w_ref, b_ref, ul_ref, uc_ref):
    is_ctx = pl.program_id(1) < ctx_tiles
    row = jnp.where(is_ctx, 2, pl.program_id(0))
    x = jnp.where(is_ctx, _first_tile(x_ref, c_ref), x_ref[...])
    _emit_ssm_inputs(x, row, is_ctx, mod_ref, g_ref, w_ref, b_ref, ul_ref, uc_ref)


def _ssm_input_specs(bsz, s, n, ctx_tiles, ctx_len):
    steps = (s - ctx_tiles * TOK_TILE) // (SSM_CHUNK * SUBLANES)
    cpt, index = _chunk_tile_spec(n, ctx_tiles)
    specs = [pl.BlockSpec((None, cpt, None, CHUNK_PITCH, n), lambda b, i: index(b, i, steps // cpt)),
             pl.BlockSpec((None, ctx_len, n), lambda b, i: (b, 0, 0))]
    shapes = [jax.ShapeDtypeStruct((bsz, steps, SUBLANES, CHUNK_PITCH, n), F32),
              jax.ShapeDtypeStruct((bsz, ctx_len, n), F32)]
    return specs, shapes


def _uproj(x, ctx, ssm_in, ctx_tiles):
    bsz, seq, d = x.shape
    ctx_len = ctx.shape[1]
    s = ctx_tiles * TOK_TILE + seq
    arrays, specs, n_u = ssm_in
    out_specs, out_shapes = _ssm_input_specs(bsz, s, n_u, ctx_tiles, ctx_len)
    return pl.pallas_call(
        functools.partial(_uproj_kernel, ctx_tiles),
        grid=(bsz, s // TOK_TILE),
        in_specs=_first_specs(d, ctx_tiles, ctx_len) + specs,
        out_specs=out_specs,
        out_shape=out_shapes,
        compiler_params=pltpu.CompilerParams(
            dimension_semantics=("arbitrary", "arbitrary"), vmem_limit_bytes=VMEM_LIMIT_BYTES),
        name="ssm_uproj",
    )(x, ctx, *arrays)


def _cmul(a_re, a_im, b_re, b_im):
    return a_re * b_re - a_im * b_im, a_re * b_im + a_im * b_re


def _ssm_kernel(ctx_chunks, ul_ref, uc_ref, mt_ref, wt_ref, vt_ref, a_ref, yl_ref, yc_ref,
                ut_scr, yt_scr, x_scr, hf_scr, hb_scr):
    L = SSM_CHUNK
    H = SSM_GROUP
    parts = SSM_IO_PARTS
    n_part = ul_ref.shape[0] // CHUNK_PITCH
    n_lat = n_part * parts
    n_all = ut_scr.shape[-1]
    phase = pl.program_id(2)
    trow = lambda t: pl.ds(pl.multiple_of(t * H, H), H)

    def fill(cols, rows_of_t):
        def body(t, carry):
            at = rows_of_t(t).T
            for g in range(SSM_LANE_GROUPS):
                ut_scr[g, trow(t), cols] = at[g * H:(g + 1) * H, :].astype(BF16)
            return carry
        lax.fori_loop(0, L, body, 0, unroll=4)

    for q in range(parts):
        @pl.when(phase == q)
        def _():
            fill(slice(q * n_part, (q + 1) * n_part), lambda t: ul_ref[pl.ds(t, n_part, stride=CHUNK_PITCH), :])

    @pl.when(phase == parts)
    def _():
        pad = jnp.zeros((n_all - n_lat - ctx_chunks, LANES), F32)
        fill(slice(n_lat, n_all),
             lambda t: jnp.concatenate([uc_ref[pl.ds(t, ctx_chunks, stride=L), :], pad], axis=0))
        _ssm_compute(ctx_chunks, n_lat, mt_ref, wt_ref, vt_ref, a_ref, ut_scr, yt_scr, x_scr, hf_scr, hb_scr)

        def drain_ctx(t, carry):
            y = yt_scr[t, :, n_lat:].astype(F32).T
            yc_ref[pl.ds(t, ctx_chunks, stride=L), :] = y[:ctx_chunks]
            return carry
        lax.fori_loop(0, L, drain_ctx, 0, unroll=4)

    for q in range(parts):
        @pl.when(phase == parts + 1 + q)
        def _():
            def drain(t, carry):
                y = yt_scr[t, :, q * n_part:(q + 1) * n_part].astype(F32).T
                yl_ref[pl.ds(t, n_part, stride=CHUNK_PITCH), :] = y
                return carry
            lax.fori_loop(0, L, drain, 0, unroll=4)
            for t in range(L, CHUNK_PITCH):
                yl_ref[pl.ds(t, n_part, stride=CHUNK_PITCH), :] = jnp.zeros((n_part, LANES), F32)


def _ssm_compute(ctx_chunks, n_lat, mt_ref, wt_ref, vt_ref, a_ref, ut_scr, yt_scr, x_scr, hf_scr, hb_scr):
    L = SSM_CHUNK
    H = SSM_GROUP
    sub = SUBLANES
    n_all = ut_scr.shape[-1]
    steps = n_lat // sub
    half = STATE * N_DIR
    gl = SSM_LANE_GROUPS
    gs = SSM_SCAN_GROUPS

    is_fwd = lax.broadcasted_iota(jnp.int32, (sub, half), 1) < STATE
    is_fwd1 = is_fwd[:1]
    sl = lax.broadcasted_iota(jnp.int32, (sub, half), 0)
    blk = lambda j: pl.ds(pl.multiple_of(j * sub, sub), sub)
    zero = jnp.zeros((sub, half), F32)
    lane2 = lax.broadcasted_iota(jnp.int32, (n_all, 2 * half), 1)
    take_fwd = (lane2 % half) < STATE
    tail0 = n_lat + ctx_chunks
    assert tail0 % sub == 0 and ctx_chunks == sub

    for g0 in range(0, gl, gs):
        coef = lambda g, r, rows=sub: jnp.broadcast_to(a_ref[g0 + g, r:r + 1, :], (rows, half))
        a_re = [coef(g, 0) for g in range(gs)]
        a_im = [coef(g, 1) for g in range(gs)]

        for g in range(gs):
            xt = jnp.dot(wt_ref[g0 + g], ut_scr[g0 + g], preferred_element_type=F32)
            x_scr[g] = xt.T
            hf_scr[g, tail0:, :] = jnp.zeros((n_all - tail0, 2 * half), F32)
            hb_scr[g, tail0:, :] = jnp.zeros((n_all - tail0, 2 * half), F32)

        h0 = []
        for g in range(gs):
            xc = x_scr[g, n_lat:n_lat + ctx_chunks, :]
            ar, ai = a_re[g][:1], a_im[g][:1]
            h_re = h_im = jnp.zeros((1, half), F32)
            hf_rows, hb_rows = [], [None] * ctx_chunks
            for k in range(ctx_chunks):
                kb = ctx_chunks - 1 - k
                hf_rows.append((h_re, h_im))
                hb_rows[kb] = (h_re, h_im)
                x_re = jnp.where(is_fwd1, xc[k:k + 1, :half], xc[kb:kb + 1, :half])
                x_im = jnp.where(is_fwd1, xc[k:k + 1, half:], xc[kb:kb + 1, half:])
                p_re, p_im = _cmul(ar, ai, h_re, h_im)
                h_re, h_im = p_re + x_re, p_im + x_im
            cat = lambda rows, part: jnp.concatenate([r[part] for r in rows], axis=0)
            hf_scr[g, n_lat:n_lat + ctx_chunks, :half] = cat(hf_rows, 0)
            hf_scr[g, n_lat:n_lat + ctx_chunks, half:] = cat(hf_rows, 1)
            hb_scr[g, n_lat:n_lat + ctx_chunks, :half] = cat(hb_rows, 0)
            hb_scr[g, n_lat:n_lat + ctx_chunks, half:] = cat(hb_rows, 1)
            h0.append((jnp.broadcast_to(h_re, (sub, half)), jnp.broadcast_to(h_im, (sub, half))))

        def scan_step(j, carry):
            jb = steps - 1 - j
            new = []
            for g in range(gs):
                h_re, h_im = carry[g]
                x_re = jnp.where(is_fwd, x_scr[g, blk(j), :half], x_scr[g, blk(jb), :half])
                x_im = jnp.where(is_fwd, x_scr[g, blk(j), half:], x_scr[g, blk(jb), half:])
                hf_scr[g, blk(j), :half] = h_re
                hf_scr[g, blk(j), half:] = h_im
                hb_scr[g, blk(jb), :half] = h_re
                hb_scr[g, blk(jb), half:] = h_im
                p_re, p_im = _cmul(a_re[g], a_im[g], h_re, h_im)
                new.append((p_re + x_re, p_im + x_im))
            return tuple(new)

        ends = lax.fori_loop(0, steps, scan_step, tuple((zero, zero) for _ in range(gs)), unroll=2)

        carries = []
        for g in range(gs):
            def shift(t, h):
                return jnp.where(is_fwd, jnp.where(sl < 1, h, pltpu.roll(t, 1, 0)),
                                 jnp.where(sl >= sub - 1, h, pltpu.roll(t, sub - 1, 0)))
            e_re, e_im = ends[g]
            s_re, s_im = coef(g, 2), coef(g, 3)
            c_re, c_im = shift(zero, h0[g][0]), shift(zero, h0[g][1])
            for _ in range(sub - 1):
                p_re, p_im = _cmul(s_re, s_im, c_re, c_im)
                c_re, c_im = shift(e_re + p_re, h0[g][0]), shift(e_im + p_im, h0[g][1])
            carries.append((c_re, c_im))

        def fix_step(j, carry):
            jb = steps - 1 - j
            new = []
            for g in range(gs):
                d_re, d_im = carry[g]
                hf_scr[g, blk(j), :half] += d_re
                hf_scr[g, blk(j), half:] += d_im
                hb_scr[g, blk(jb), :half] += d_re
                hb_scr[g, blk(jb), half:] += d_im
                new.append(_cmul(a_re[g], a_im[g], d_re, d_im))
            return tuple(new)

        lax.fori_loop(0, steps, fix_step, tuple(carries), unroll=2)

        for g in range(gs):
            h_in = jnp.where(take_fwd, hf_scr[g], hb_scr[g]).T.astype(BF16)
            yt = (jnp.dot(mt_ref[g0 + g], ut_scr[g0 + g], preferred_element_type=F32)
                  + jnp.dot(vt_ref[g0 + g], h_in, preferred_element_type=F32))
            for t in range(L):
                yt_scr[t, (g0 + g) * H:(g0 + g + 1) * H, :] = yt[t * H:(t + 1) * H, :].astype(BF16)


def _ssm(u_lat, u_ctx, mt, wt, vt, a, l):
    bsz, steps, nseg, pitch, n = u_lat.shape
    L = SSM_CHUNK
    ctx_len = u_ctx.shape[1]
    ctx_chunks = ctx_len // L
    n_lat = steps * nseg
    n_all = -(-(n_lat + ctx_chunks) // LANES) * LANES
    gl = SSM_LANE_GROUPS
    f = L * SSM_GROUP
    ns = wt.shape[-2]
    parts = SSM_IO_PARTS
    n_blocks = n // LANES
    part_rows = n_lat * pitch // parts
    lat_in = pl.BlockSpec((None, part_rows, LANES), lambda b, j, p: (b, jnp.minimum(p, parts - 1), j))
    lat_out = pl.BlockSpec((None, part_rows, LANES), lambda b, j, p: (b, jnp.maximum(p - parts - 1, 0), j))
    ctx_spec = pl.BlockSpec((None, ctx_len, LANES), lambda b, j, p: (b, 0, j))
    grp = lambda shape: pl.BlockSpec(
        (None, gl) + shape, lambda b, j, p: (l, jnp.where(p > parts, (j + 1) % n_blocks, j), 0, 0))
    y_lat, y_ctx = pl.pallas_call(
        functools.partial(_ssm_kernel, ctx_chunks),
        grid=(bsz, n_blocks, 2 * parts + 1),
        in_specs=[lat_in, ctx_spec, grp((f, f)), grp((ns, f)), grp((f, ns)), grp((4, ns // 2))],
        out_specs=[lat_out, ctx_spec],
        out_shape=[jax.ShapeDtypeStruct((bsz, n_lat * pitch, n), F32),
                   jax.ShapeDtypeStruct((bsz, ctx_len, n), F32)],
        scratch_shapes=[pltpu.VMEM((gl, f, n_all), BF16), pltpu.VMEM((L, LANES, n_all), BF16)]
        + [pltpu.VMEM((SSM_SCAN_GROUPS, n_all, ns), F32)] * 3,
        compiler_params=pltpu.CompilerParams(
            dimension_semantics=("arbitrary", "arbitrary", "arbitrary"), vmem_limit_bytes=VMEM_LIMIT_BYTES),
        name="ssm_chunked",
    )(u_lat.reshape(bsz, n_lat * pitch, n), u_ctx, mt, wt, vt, a)
    return y_lat.reshape(u_lat.shape), y_ctx


def _cpow(z_re, z_im, n):
    out = None
    while n:
        if n & 1:
            out = (z_re, z_im) if out is None else _cmul(out[0], out[1], z_re, z_im)
        n >>= 1
        if n:
            z_re, z_im = _cmul(z_re, z_im, z_re, z_im)
    return out


def _ops_kernel(seg_steps, pw_ref, pv_ref, pa_ref, bbt_ref, c_ref, csel_ref, dcol_ref,
                mt_ref, wt_ref, vt_ref, a_ref, w_scr):
    gb = mt_ref.shape[0]
    f = mt_ref.shape[-1]
    H = SSM_GROUP
    L = f // H
    half = STATE * N_DIR
    lane = lax.broadcasted_iota(jnp.int32, (H, 2 * f), 1)
    row = lax.broadcasted_iota(jnp.int32, (H, 2 * f), 0)
    centre = lane == (L - 1) * H + row
    zeros = jnp.zeros((H, f), F32)
    for g in range(gb):
        bt_re, bt_im = bbt_ref[g, 0], bbt_ref[g, 1]
        c_re, c_im = c_ref[g, 0], c_ref[g, 1]
        for t in range(L):
            rows = slice(t * H, (t + 1) * H)
            w_re, w_im = _cmul(pw_ref[g, 0, t:t + 1, :], pw_ref[g, 1, t:t + 1, :], bt_re, bt_im)
            w_scr[rows, :half] = w_re
            w_scr[rows, half:] = w_im
            g_re, g_im = _cmul(pv_ref[g, 0, t:t + 1, :], pv_ref[g, 1, t:t + 1, :], c_re, c_im)
            vt_ref[g, rows, :half] = g_re.astype(vt_ref.dtype)
            vt_ref[g, rows, half:] = (-g_im).astype(vt_ref.dtype)
        wt = w_scr[...].T
        wt_ref[g] = wt.astype(wt_ref.dtype)
        kk = jnp.dot(csel_ref[g], wt, preferred_element_type=F32, precision=lax.Precision.HIGHEST)
        kf = jnp.concatenate([kk[:H], zeros], axis=1)
        kb = pltpu.roll(jnp.concatenate([kk[H:], zeros], axis=1), (L - 1) * H, 1)
        dmat = jnp.concatenate([dcol_ref[g]] * (2 * f // LANES), axis=1)
        k = kf + kb + jnp.where(centre, dmat, 0.0)
        for t in range(L):
            off = f - (t + 1) * H
            win = k if off == 0 else pltpu.roll(k, 2 * f - off, 1)
            mt_ref[g, t * H:(t + 1) * H, :] = win[:, :f].astype(mt_ref.dtype)
        ac_re, ac_im = pa_ref[g, 0:1, :], pa_ref[g, 1:2, :]
        as_re, as_im = _cpow(ac_re, ac_im, seg_steps)
        a_ref[g, 0:1, :] = ac_re
        a_ref[g, 1:2, :] = ac_im
        a_ref[g, 2:3, :] = as_re
        a_ref[g, 3:4, :] = as_im


def _ssm_operators(lam_re, lam_im, log_dt, b_re, b_im, c_re, c_im, d_skip, seg_steps):
    L = SSM_CHUNK
    H = SSM_GROUP
    lam_re = lam_re.astype(F32)
    lam_im = lam_im.astype(F32)
    depth, _, ng, _ = lam_re.shape
    dt = jnp.exp(log_dt.astype(F32))[..., None]
    mag = jnp.exp(lam_re * dt)
    a_re = mag * jnp.cos(lam_im * dt)
    a_im = mag * jnp.sin(lam_im * dt)
    nr, ni = a_re - 1.0, a_im
    den = lam_re * lam_re + lam_im * lam_im
    f_re = (nr * lam_re + ni * lam_im) / den
    f_im = (ni * lam_re - nr * lam_im) / den
    br, bi = b_re.astype(F32), b_im.astype(F32)
    bb_re = f_re[..., None] * br - f_im[..., None] * bi
    bb_im = f_re[..., None] * bi + f_im[..., None] * br
    k = jnp.arange(L + 1, dtype=F32)[:, None]
    pmag = jnp.exp(lam_re[..., None, :] * dt[..., None, :] * k)
    parg = lam_im[..., None, :] * dt[..., None, :] * k
    p_re = pmag * jnp.cos(parg)
    p_im = pmag * jnp.sin(parg)
    both = lambda fwd, bwd: jnp.concatenate([fwd, bwd], axis=-1)
    reim = lambda re, im: jnp.stack([re, im], axis=2)
    pw = reim(both(p_re[:, 0, :, L - 1::-1][:, :, :L], p_re[:, 1, :, :L]),
              both(p_im[:, 0, :, L - 1::-1][:, :, :L], p_im[:, 1, :, :L]))
    pv = reim(both(p_re[:, 0, :, 1:], p_re[:, 1, :, :0:-1]), both(p_im[:, 0, :, 1:], p_im[:, 1, :, :0:-1]))
    pa = jnp.stack([both(p_re[:, 0, :, L], p_re[:, 1, :, L]), both(p_im[:, 0, :, L], p_im[:, 1, :, L])], axis=2)
    sw = lambda z: jnp.swapaxes(z, -1, -2)
    bbt = reim(both(sw(bb_re[:, 0]), sw(bb_re[:, 1])), both(sw(bb_im[:, 0]), sw(bb_im[:, 1])))
    cr, ci = c_re.astype(F32), c_im.astype(F32)
    cc = reim(both(cr[:, 0], cr[:, 1]), both(ci[:, 0], ci[:, 1]))
    z = jnp.zeros_like(cr[:, 0])
    csel = jnp.concatenate([jnp.concatenate([cr[:, 0], z, -ci[:, 0], z], axis=-1),
                            jnp.concatenate([z, cr[:, 1], z, -ci[:, 1]], axis=-1)], axis=-2)
    dcol = jnp.broadcast_to(d_skip.astype(F32).reshape(depth, ng, H, 1), (depth, ng, H, LANES))
    f = L * H
    ns = 2 * N_DIR * STATE
    gb = 8
    blk = lambda *shape: pl.BlockSpec((None, gb) + shape, lambda l, j: (l, j) + (0,) * len(shape))
    return pl.pallas_call(
        functools.partial(_ops_kernel, seg_steps),
        grid=(depth, ng // gb),
        in_specs=[blk(2, L, ns // 2), blk(2, L, ns // 2), blk(2, ns // 2), blk(2, H, ns // 2),
                  blk(2, H, ns // 2), blk(2 * H, ns), blk(H, LANES)],
        out_specs=[blk(f, f), blk(ns, f), blk(f, ns), blk(4, ns // 2)],
        out_shape=[jax.ShapeDtypeStruct((depth, ng, f, f), BF16), jax.ShapeDtypeStruct((depth, ng, ns, f), BF16),
                   jax.ShapeDtypeStruct((depth, ng, f, ns), BF16), jax.ShapeDtypeStruct((depth, ng, 4, ns // 2), F32)],
        scratch_shapes=[pltpu.VMEM((f, ns), F32)],
        compiler_params=pltpu.CompilerParams(
            dimension_semantics=("parallel", "parallel"), vmem_limit_bytes=VMEM_LIMIT_BYTES),
        name="ssm_operators",
    )(pw, pv, pa, bbt, cc, csel, dcol)


def _gelu_tanh(x):
    return 0.5 * x * (1.0 + jnp.tanh(np.sqrt(2.0 / np.pi).astype(np.float32)
                                     * (x + np.float32(0.044715) * (x * x * x))))


def _mixer_kernel(ctx_tiles, ctx_len, first, x_ref, *refs):
    if first:
        c_ref, *refs = refs
    (ysl_ref, ysc_ref, mod_ref, g_ref, wa_ref, ba_ref, wg_ref, bg_ref, cw_ref,
     woa_ref, wglu_ref, bglu_ref, wo_ref, o_ref) = refs
    tm, d = x_ref.shape
    dc = woa_ref.shape[0]
    is_ctx = pl.program_id(1) < ctx_tiles
    row = jnp.where(is_ctx, 2, pl.program_id(0))
    x = jnp.where(is_ctx, _first_tile(x_ref, c_ref), x_ref[...]) if first else x_ref[...]
    h = _norm_mod(x, g_ref[...], _mod_vec(mod_ref, row, 0, d), _mod_vec(mod_ref, row, 1, d)).astype(BF16)
    za = jnp.dot(h, wa_ref[...], preferred_element_type=F32) + ba_ref[...]
    zg = jnp.dot(h, wg_ref[...], preferred_element_type=F32) + bg_ref[...]
    g_b, g_c, x_in = za[:, :dc], za[:, dc:2 * dc], za[:, 2 * dc:]
    gate_a, gate_b = zg[:, :d], zg[:, d:]
    v = g_c * x_in
    t = lax.broadcasted_iota(jnp.int32, (tm, 1), 0)
    seg = jnp.where(is_ctx, ctx_len - 1, GRID_W - 1)
    pos = t & seg
    v_prev = jnp.where(pos == 0, 0.0, pltpu.roll(v, 1, 0))
    v_next = jnp.where(pos == seg, 0.0, pltpu.roll(v, tm - 1, 0))
    cv = cw_ref[0:1, :] * v_prev + cw_ref[1:2, :] * v + cw_ref[2:3, :] * v_next
    y_a = jnp.dot((g_b * cv).astype(BF16), woa_ref[...], preferred_element_type=F32)
    ys_ctx = jnp.concatenate([ysc_ref[...]] * (tm // ctx_len), axis=0)
    ys = jnp.where(is_ctx, ys_ctx, ysl_ref[:, :SSM_CHUNK, :].reshape(tm, ysc_ref.shape[-1]))
    s = _gelu_tanh(ys)
    gl = jnp.dot(s.astype(BF16), wglu_ref[...], preferred_element_type=F32) + bglu_ref[...]
    y_b = gl[:, :d] * jax.nn.sigmoid(gl[:, d:])
    merged = jax.nn.sigmoid(gate_a) * y_a + jax.nn.sigmoid(gate_b) * y_b
    out = jnp.dot(merged.astype(BF16), wo_ref[...], preferred_element_type=F32)
    o_ref[...] = x + _mod_vec(mod_ref, row, 2, d) * out


def _mixer(tokens, ys_lat, ys_ctx, params, ctx_tiles, ctx_len):
    first = len(tokens) == 2
    bsz, _, d = tokens[0].shape
    s = tokens[0].shape[1] + (ctx_tiles * TOK_TILE if first else 0)
    n = ys_ctx.shape[-1]
    steps = ys_lat.shape[1]
    cpt, index = _chunk_tile_spec(n, ctx_tiles)
    tok = pl.BlockSpec((None, TOK_TILE, d), lambda b, i: (b, i, 0))
    arrays, specs = params
    return pl.pallas_call(
        functools.partial(_mixer_kernel, ctx_tiles, ctx_len, first),
        grid=(bsz, s // TOK_TILE),
        in_specs=(_first_specs(d, ctx_tiles, ctx_len) if first else [tok])
        + [pl.BlockSpec((None, cpt, None, CHUNK_PITCH, n), lambda b, i: index(b, i, steps // cpt)),
           pl.BlockSpec((None, ctx_len, n), lambda b, i: (b, 0, 0))]
        + specs,
        out_specs=tok,
        out_shape=jax.ShapeDtypeStruct((bsz, s, d), F32),
        compiler_params=pltpu.CompilerParams(
            dimension_semantics=("parallel", "parallel"), vmem_limit_bytes=VMEM_LIMIT_BYTES),
        name="mixer",
    )(*tokens, ys_lat, ys_ctx, *arrays)


def _ffn_kernel(ctx_tiles, tile_offset, final, x_ref, mod_ref, g_ref, win_ref, wout_ref, *rest):
    d = x_ref.shape[-1]
    dff = wout_ref.shape[0]
    is_ctx = pl.program_id(1) + tile_offset < ctx_tiles
    row = jnp.where(is_ctx, 2, pl.program_id(0))
    x = x_ref[...]
    h = _norm_mod(x, g_ref[...], _mod_vec(mod_ref, row, 3, d), _mod_vec(mod_ref, row, 4, d))
    z = jnp.dot(h.astype(BF16), win_ref[...], preferred_element_type=F32)
    gate, up = z[:, :dff], z[:, dff:]
    act = gate * jax.nn.sigmoid(gate) * up
    out = jnp.dot(act.astype(BF16), wout_ref[...], preferred_element_type=F32)
    y = x + _mod_vec(mod_ref, row, 5, d) * out
    if final:
        fg_ref, o_ref = rest
        ms = jnp.mean(y * y, axis=-1, keepdims=True)
        o_ref[...] = y * lax.rsqrt(ms + RMS_EPS) * fg_ref[...]
    else:
        modn_ref, gn_ref, wu_ref, bu_ref, o_ref, ul_ref, uc_ref = rest
        o_ref[...] = y
        _emit_ssm_inputs(y, row, is_ctx, modn_ref, gn_ref, wu_ref, bu_ref, ul_ref, uc_ref)


def _ffn(xs, params, ctx_tiles, ctx_len, final_g=None, nxt=None):
    bsz, s, d = xs.shape
    final = final_g is not None
    tile_offset = ctx_tiles if final else 0
    s_out = s - tile_offset * TOK_TILE
    arrays, specs = params
    if final:
        arrays, specs = arrays + [final_g], specs + [_layer_spec(final_g, 0)]
    else:
        arrays, specs = arrays + nxt[0], specs + nxt[1]
    out_specs = [pl.BlockSpec((None, TOK_TILE, d), lambda b, i: (b, i, 0))]
    out_shapes = [jax.ShapeDtypeStruct((bsz, s_out, d), F32)]
    if not final:
        u_specs, u_shapes = _ssm_input_specs(bsz, s, nxt[2], ctx_tiles, ctx_len)
        out_specs, out_shapes = out_specs + u_specs, out_shapes + u_shapes
    return pl.pallas_call(
        functools.partial(_ffn_kernel, ctx_tiles, tile_offset, final),
        grid=(bsz, s_out // TOK_TILE),
        in_specs=[pl.BlockSpec((None, TOK_TILE, d), lambda b, i: (b, i + tile_offset, 0))] + specs,
        out_specs=out_specs,
        out_shape=out_shapes,
        compiler_params=pltpu.CompilerParams(
            dimension_semantics=("arbitrary", "arbitrary"), vmem_limit_bytes=VMEM_LIMIT_BYTES),
        name="ffn_final" if final else "ffn",
    )(xs, *arrays)


def kernel(x, c, ctx, c_ctx, w_mod, b_mod, norm1_g, norm2_g, w_in, b_in, conv_w, w_out_a, lam_re, lam_im,
           log_dt, b_re, b_im, c_re, c_im, d_skip, w_glu, b_glu, w_o, w_ff_in, w_ff_out, final_g):
    bsz, seq, d = x.shape
    depth = w_mod.shape[0]
    ctx_len = ctx.shape[1]
    d_conv = conv_w.shape[-1]
    d_ssm = d_skip.shape[-1]
    u_lo, u_hi = 3 * d_conv, 3 * d_conv + d_ssm
    assert TOK_TILE % ctx_len == 0 and TOK_TILE % GRID_W == 0
    assert ctx_len == SSM_CHUNK * SUBLANES and d_ssm % LANES == 0 and bsz <= 2
    assert seq % (SUBLANES * TOK_TILE) == 0
    seg_steps = seq // (SSM_CHUNK * SUBLANES)
    ctx_tiles = 1

    cvec = jnp.zeros((8, d), F32).at[:bsz].set(c.astype(F32)).at[2].set(c_ctx.astype(F32))
    mod = _mod_all(cvec, w_mod, b_mod)

    mt_op, wt_op, vt_op, a_op = _ssm_operators(lam_re, lam_im, log_dt, b_re, b_im, c_re, c_im, d_skip,
                                               seg_steps)

    w_in, w_out_a, w_glu, w_o = (w.astype(BF16) for w in (w_in, w_out_a, w_glu, w_o))
    w_ff_in, w_ff_out = w_ff_in.astype(BF16), w_ff_out.astype(BF16)
    row = lambda v: v.reshape(v.shape[0], 1, v.shape[-1])
    b_in, b_glu, g1, g2, fg = row(b_in), row(b_glu), row(norm1_g), row(norm2_g), final_g.reshape(1, 1, d)
    gates_lo = u_hi

    def operands(l, *items):
        arrays = [arr for arr, _ in items]
        return arrays, [_layer_spec(arr, l, cols) for arr, cols in items]

    u_cols = (u_lo, d_ssm)
    ssm_in = lambda l: operands(l, (mod, None), (g1, None), (w_in, u_cols), (b_in, u_cols)) + (d_ssm,)
    mixer_in = lambda l: operands(
        l, (mod, None), (g1, None), (w_in, (0, u_lo)), (b_in, (0, u_lo)),
        (w_in, (gates_lo, 2 * d)), (b_in, (gates_lo, 2 * d)), (conv_w, None), (w_out_a, None),
        (w_glu, None), (b_glu, None), (w_o, None))
    ffn_in = lambda l: operands(l, (mod, None), (g2, None), (w_ff_in, None), (w_ff_out, None))

    u_lat, u_ctx = _uproj(x, ctx, ssm_in(0), ctx_tiles)
    xs = None
    for l in range(depth):
        ys_lat, ys_ctx = _ssm(u_lat, u_ctx, mt_op, wt_op, vt_op, a_op, l)
        xs = _mixer((x, ctx) if l == 0 else (xs,), ys_lat, ys_ctx, mixer_in(l), ctx_tiles, ctx_len)
        if l + 1 < depth:
            xs, u_lat, u_ctx = _ffn(xs, ffn_in(l), ctx_tiles, ctx_len, nxt=ssm_in(l + 1))
        else:
            xs, = _ffn(xs, ffn_in(l), ctx_tiles, ctx_len, final_g=fg)
    return xs
```

```python
import functools

import numpy as np
import jax
import jax.numpy as jnp
from jax import lax
from jax.experimental import pallas as pl
from jax.experimental.pallas import tpu as pltpu

GRID_W = 64
CONV_K = 3
SSM_GROUP = 16
STATE = 64
N_DIR = 2
RMS_EPS = 1e-6

TOK_TILE = 512
SSM_CHUNK = 32
CHUNK_PITCH = 40
SUBLANES = 8
LANES = 128
SSM_LANE_GROUPS = LANES // SSM_GROUP
SSM_SCAN_GROUPS = 4
SSM_IO_PARTS = 4
MIXER_ROW_BLOCKS = 2
FFN_ROW_BLOCKS = 2
VMEM_LIMIT_BYTES = 56 * 1024 * 1024

F32 = jnp.float32
BF16 = jnp.bfloat16


def _layer_spec(arr, l, cols=None):
    _, rows, n = arr.shape
    start, width = (0, n) if cols is None else cols
    assert start % width == 0
    return pl.BlockSpec((None, rows, width), lambda *_: (l, 0, start // width), pipeline_mode=pl.Buffered(1))


def _first_tile(x_ref, c_ref):
    pad = jnp.zeros((x_ref.shape[0] - c_ref.shape[0], x_ref.shape[1]), x_ref.dtype)
    return jnp.concatenate([c_ref[...], pad], axis=0)


def _first_specs(d, ctx_tiles, ctx_len):
    return [pl.BlockSpec((None, TOK_TILE, d), lambda b, i: (b, jnp.maximum(i - ctx_tiles, 0), 0)),
            pl.BlockSpec((None, ctx_len, d), lambda b, i: (b, 0, 0))]


def _norm_mod(x, g, shift, scale):
    ms = jnp.mean(x * x, axis=-1, keepdims=True)
    y = x * lax.rsqrt(ms + RMS_EPS) * g
    return y * (1.0 + scale) + shift


def _mod_vec(mod_ref, row, k, d):
    return mod_ref[pl.ds(row, 1), k * d:(k + 1) * d]


def _chunk_tile_spec(n, ctx_tiles):
    cpt = TOK_TILE // SSM_CHUNK

    def index(b, i, steps_per_seg_tiles):
        k = jnp.maximum(i - ctx_tiles, 0)
        return (b, k % steps_per_seg_tiles, k // steps_per_seg_tiles, 0, 0)

    return cpt, index


def _mod_kernel(c_ref, w_ref, b_ref, o_ref):
    c = c_ref[...]
    s = c * jax.nn.sigmoid(c)
    o_ref[...] = jnp.dot(s.astype(BF16), w_ref[...].astype(BF16),
                         preferred_element_type=F32) + b_ref[...]


def _mod_all(cvec, w_mod, b_mod):
    depth, d, n = w_mod.shape
    tn = 1536
    return pl.pallas_call(
        _mod_kernel,
        grid=(depth, n // tn),
        in_specs=[pl.BlockSpec((8, d), lambda l, j: (0, 0)),
                  pl.BlockSpec((None, d, tn), lambda l, j: (l, 0, j)),
                  pl.BlockSpec((None, 1, tn), lambda l, j: (l, 0, j))],
        out_specs=pl.BlockSpec((None, 8, tn), lambda l, j: (l, 0, j)),
        out_shape=jax.ShapeDtypeStruct((depth, 8, n), F32),
        compiler_params=pltpu.CompilerParams(
            dimension_semantics=("arbitrary", "arbitrary"), vmem_limit_bytes=VMEM_LIMIT_BYTES),
        name="adaln_mod",
    )(cvec, w_mod, b_mod.reshape(depth, 1, n))


def _emit_ssm_inputs(x, row, is_ctx, mod_ref, g_ref, w_ref, b_ref, ul_ref, uc_ref):
    d = x.shape[-1]
    h = _norm_mod(x, g_ref[...], _mod_vec(mod_ref, row, 0, d), _mod_vec(mod_ref, row, 1, d))
    u = jnp.dot(h.astype(BF16), w_ref[...], preferred_element_type=F32) + b_ref[...]
    cpt, pitch, n = ul_ref.shape
    ul_ref[:, :SSM_CHUNK, :] = u.reshape(cpt, SSM_CHUNK, n)
    ul_ref[:, SSM_CHUNK:, :] = jnp.zeros((cpt, pitch - SSM_CHUNK, n), F32)

    @pl.when(is_ctx)
    def _():
        uc_ref[...] = u[:uc_ref.shape[0]]


def _uproj_kernel(ctx_tiles, x_ref, c_ref, mod_ref, g_ref, w_ref, b_ref, ul_ref, uc_ref):
    is_ctx = pl.program_id(1) < ctx_tiles
    row = jnp.where(is_ctx, 2, pl.program_id(0))
    x = jnp.where(is_ctx, _first_tile(x_ref, c_ref), x_ref[...])
    _emit_ssm_inputs(x, row, is_ctx, mod_ref, g_ref, w_ref, b_ref, ul_ref, uc_ref)


def _ssm_input_specs(bsz, s, n, ctx_tiles, ctx_len):
    steps = (s - ctx_tiles * TOK_TILE) // (SSM_CHUNK * SUBLANES)
    cpt, index = _chunk_tile_spec(n, ctx_tiles)
    specs = [pl.BlockSpec((None, cpt, None, CHUNK_PITCH, n), lambda b, i: index(b, i, steps // cpt)),
             pl.BlockSpec((None, ctx_len, n), lambda b, i: (b, 0, 0))]
    shapes = [jax.ShapeDtypeStruct((bsz, steps, SUBLANES, CHUNK_PITCH, n), F32),
              jax.ShapeDtypeStruct((bsz, ctx_len, n), F32)]
    return specs, shapes


def _uproj(x, ctx, ssm_in, ctx_tiles):
    bsz, seq, d = x.shape
    ctx_len = ctx.shape[1]
    s = ctx_tiles * TOK_TILE + seq
    arrays, specs, n_u = ssm_in
    out_specs, out_shapes = _ssm_input_specs(bsz, s, n_u, ctx_tiles, ctx_len)
    return pl.pallas_call(
        functools.partial(_uproj_kernel, ctx_tiles),
        grid=(bsz, s // TOK_TILE),
        in_specs=_first_specs(d, ctx_tiles, ctx_len) + specs,
        out_specs=out_specs,
        out_shape=out_shapes,
        compiler_params=pltpu.CompilerParams(
            dimension_semantics=("arbitrary", "arbitrary"), vmem_limit_bytes=VMEM_LIMIT_BYTES),
        name="ssm_uproj",
    )(x, ctx, *arrays)


def _cmul(a_re, a_im, b_re, b_im):
    return a_re * b_re - a_im * b_im, a_re * b_im + a_im * b_re


def _ssm_kernel(ctx_chunks, ul_ref, uc_ref, mt_ref, wt_ref, vt_ref, a_ref, yl_ref, yc_ref,
                ut_scr, yt_scr, x_scr, hf_scr, hb_scr):
    L = SSM_CHUNK
    H = SSM_GROUP
    parts = SSM_IO_PARTS
    n_part = ul_ref.shape[0] // CHUNK_PITCH
    n_lat = n_part * parts
    n_all = ut_scr.shape[-1]
    phase = pl.program_id(2)
    trow = lambda t: pl.ds(pl.multiple_of(t * H, H), H)

    def fill(cols, rows_of_t):
        def body(t, carry):
            at = rows_of_t(t).T
            for g in range(SSM_LANE_GROUPS):
                ut_scr[g, trow(t), cols] = at[g * H:(g + 1) * H, :].astype(BF16)
            return carry
        lax.fori_loop(0, L, body, 0, unroll=16)

    for q in range(parts):
        @pl.when(phase == q)
        def _():
            fill(slice(q * n_part, (q + 1) * n_part), lambda t: ul_ref[pl.ds(t, n_part, stride=CHUNK_PITCH), :])

    @pl.when(phase == parts)
    def _():
        pad = jnp.zeros((n_all - n_lat - ctx_chunks, LANES), F32)
        fill(slice(n_lat, n_all),
             lambda t: jnp.concatenate([uc_ref[pl.ds(t, ctx_chunks, stride=L), :], pad], axis=0))
        _ssm_compute(ctx_chunks, n_lat, mt_ref, wt_ref, vt_ref, a_ref, ut_scr, yt_scr, x_scr, hf_scr, hb_scr)

        def drain_ctx(t, carry):
            y = yt_scr[t, :, n_lat:].astype(F32).T
            yc_ref[pl.ds(t, ctx_chunks, stride=L), :] = y[:ctx_chunks]
            return carry
        lax.fori_loop(0, L, drain_ctx, 0, unroll=16)

    for q in range(parts):
        @pl.when(phase == parts + 1 + q)
        def _():
            def drain(t, carry):
                y = yt_scr[t, :, q * n_part:(q + 1) * n_part].astype(F32).T
                yl_ref[pl.ds(t, n_part, stride=CHUNK_PITCH), :] = y
                return carry
            lax.fori_loop(0, L, drain, 0, unroll=16)
            for t in range(L, CHUNK_PITCH):
                yl_ref[pl.ds(t, n_part, stride=CHUNK_PITCH), :] = jnp.zeros((n_part, LANES), F32)


def _ssm_compute(ctx_chunks, n_lat, mt_ref, wt_ref, vt_ref, a_ref, ut_scr, yt_scr, x_scr, hf_scr, hb_scr):
    L = SSM_CHUNK
    H = SSM_GROUP
    sub = SUBLANES
    n_all = ut_scr.shape[-1]
    steps = n_lat // sub
    half = STATE * N_DIR
    gl = SSM_LANE_GROUPS
    gs = SSM_SCAN_GROUPS

    is_fwd = lax.broadcasted_iota(jnp.int32, (sub, half), 1) < STATE
    is_fwd1 = is_fwd[:1]
    sl = lax.broadcasted_iota(jnp.int32, (sub, half), 0)
    blk = lambda j: pl.ds(pl.multiple_of(j * sub, sub), sub)
    zero = jnp.zeros((sub, half), F32)
    lane2 = lax.broadcasted_iota(jnp.int32, (n_all, 2 * half), 1)
    take_fwd = (lane2 % half) < STATE
    tail0 = n_lat + ctx_chunks
    assert tail0 % sub == 0 and ctx_chunks == sub

    for g0 in range(0, gl, gs):
        coef = lambda g, r, rows=sub: jnp.broadcast_to(a_ref[g0 + g, r:r + 1, :], (rows, half))
        a_re = [coef(g, 0) for g in range(gs)]
        a_im = [coef(g, 1) for g in range(gs)]

        for g in range(gs):
            xt = jnp.dot(wt_ref[g0 + g], ut_scr[g0 + g], preferred_element_type=F32)
            x_scr[g] = xt.T
            hf_scr[g, tail0:, :] = jnp.zeros((n_all - tail0, 2 * half), F32)
            hb_scr[g, tail0:, :] = jnp.zeros((n_all - tail0, 2 * half), F32)

        h0 = []
        for g in range(gs):
            xc = x_scr[g, n_lat:n_lat + ctx_chunks, :]
            ar, ai = a_re[g][:1], a_im[g][:1]
            h_re = h_im = jnp.zeros((1, half), F32)
            hf_rows, hb_rows = [], [None] * ctx_chunks
            for k in range(ctx_chunks):
                kb = ctx_chunks - 1 - k
                hf_rows.append((h_re, h_im))
                hb_rows[kb] = (h_re, h_im)
                x_re = jnp.where(is_fwd1, xc[k:k + 1, :half], xc[kb:kb + 1, :half])
                x_im = jnp.where(is_fwd1, xc[k:k + 1, half:], xc[kb:kb + 1, half:])
                p_re, p_im = _cmul(ar, ai, h_re, h_im)
                h_re, h_im = p_re + x_re, p_im + x_im
            cat = lambda rows, part: jnp.concatenate([r[part] for r in rows], axis=0)
            hf_scr[g, n_lat:n_lat + ctx_chunks, :half] = cat(hf_rows, 0)
            hf_scr[g, n_lat:n_lat + ctx_chunks, half:] = cat(hf_rows, 1)
            hb_scr[g, n_lat:n_lat + ctx_chunks, :half] = cat(hb_rows, 0)
            hb_scr[g, n_lat:n_lat + ctx_chunks, half:] = cat(hb_rows, 1)
            h0.append((jnp.broadcast_to(h_re, (sub, half)), jnp.broadcast_to(h_im, (sub, half))))

        def scan_step(j, carry):
            jb = steps - 1 - j
            new = []
            for g in range(gs):
                h_re, h_im = carry[g]
                x_re = jnp.where(is_fwd, x_scr[g, blk(j), :half], x_scr[g, blk(jb), :half])
                x_im = jnp.where(is_fwd, x_scr[g, blk(j), half:], x_scr[g, blk(jb), half:])
                hf_scr[g, blk(j), :half] = h_re
                hf_scr[g, blk(j), half:] = h_im
                hb_scr[g, blk(jb), :half] = h_re
                hb_scr[g, blk(jb), half:] = h_im
                p_re, p_im = _cmul(a_re[g], a_im[g], h_re, h_im)
                new.append((p_re + x_re, p_im + x_im))
            return tuple(new)

        ends = lax.fori_loop(0, steps, scan_step, tuple((zero, zero) for _ in range(gs)), unroll=2)

        carries = []
        for g in range(gs):
            def shift(t, h):
                return jnp.where(is_fwd, jnp.where(sl < 1, h, pltpu.roll(t, 1, 0)),
                                 jnp.where(sl >= sub - 1, h, pltpu.roll(t, sub - 1, 0)))
            e_re, e_im = ends[g]
            s_re, s_im = coef(g, 2), coef(g, 3)
            c_re, c_im = shift(zero, h0[g][0]), shift(zero, h0[g][1])
            for _ in range(sub - 1):
                p_re, p_im = _cmul(s_re, s_im, c_re, c_im)
                c_re, c_im = shift(e_re + p_re, h0[g][0]), shift(e_im + p_im, h0[g][1])
            carries.append((c_re, c_im))

        def fix_step(j, carry):
            jb = steps - 1 - j
            new = []
            for g in range(gs):
                d_re, d_im = carry[g]
                hf_scr[g, blk(j), :half] += d_re
                hf_scr[g, blk(j), half:] += d_im
                hb_scr[g, blk(jb), :half] += d_re
                hb_scr[g, blk(jb), half:] += d_im
                new.append(_cmul(a_re[g], a_im[g], d_re, d_im))
            return tuple(new)

        lax.fori_loop(0, steps, fix_step, tuple(carries), unroll=2)

        for g in range(gs):
            h_in = jnp.where(take_fwd, hf_scr[g], hb_scr[g]).T.astype(BF16)
            yt = (jnp.dot(mt_ref[g0 + g], ut_scr[g0 + g], preferred_element_type=F32)
                  + jnp.dot(vt_ref[g0 + g], h_in, preferred_element_type=F32))
            for t in range(L):
                yt_scr[t, (g0 + g) * H:(g0 + g + 1) * H, :] = yt[t * H:(t + 1) * H, :].astype(BF16)


def _ssm(u_lat, u_ctx, mt, wt, vt, a, l):
    bsz, steps, nseg, pitch, n = u_lat.shape
    L = SSM_CHUNK
    ctx_len = u_ctx.shape[1]
    ctx_chunks = ctx_len // L
    n_lat = steps * nseg
    n_all = -(-(n_lat + ctx_chunks) // LANES) * LANES
    gl = SSM_LANE_GROUPS
    f = L * SSM_GROUP
    ns = wt.shape[-2]
    parts = SSM_IO_PARTS
    n_blocks = n // LANES
    part_rows = n_lat * pitch // parts
    lat_in = pl.BlockSpec((None, part_rows, LANES), lambda b, j, p: (b, jnp.minimum(p, parts - 1), j))
    lat_out = pl.BlockSpec((None, part_rows, LANES), lambda b, j, p: (b, jnp.maximum(p - parts - 1, 0), j))
    ctx_spec = pl.BlockSpec((None, ctx_len, LANES), lambda b, j, p: (b, 0, j))
    grp = lambda shape: pl.BlockSpec(
        (None, gl) + shape, lambda b, j, p: (l, jnp.where(p > parts, (j + 1) % n_blocks, j), 0, 0))
    y_lat, y_ctx = pl.pallas_call(
        functools.partial(_ssm_kernel, ctx_chunks),
        grid=(bsz, n_blocks, 2 * parts + 1),
        in_specs=[lat_in, ctx_spec, grp((f, f)), grp((ns, f)), grp((f, ns)), grp((4, ns // 2))],
        out_specs=[lat_out, ctx_spec],
        out_shape=[jax.ShapeDtypeStruct((bsz, n_lat * pitch, n), F32),
                   jax.ShapeDtypeStruct((bsz, ctx_len, n), F32)],
        scratch_shapes=[pltpu.VMEM((gl, f, n_all), BF16), pltpu.VMEM((L, LANES, n_all), BF16)]
        + [pltpu.VMEM((SSM_SCAN_GROUPS, n_all, ns), F32)] * 3,
        compiler_params=pltpu.CompilerParams(
            dimension_semantics=("arbitrary", "arbitrary", "arbitrary"), vmem_limit_bytes=VMEM_LIMIT_BYTES),
        name="ssm_chunked",
    )(u_lat.reshape(bsz, n_lat * pitch, n), u_ctx, mt, wt, vt, a)
    return y_lat.reshape(u_lat.shape), y_ctx


def _cpow(z_re, z_im, n):
    out = None
    while n:
        if n & 1:
            out = (z_re, z_im) if out is None else _cmul(out[0], out[1], z_re, z_im)
        n >>= 1
        if n:
            z_re, z_im = _cmul(z_re, z_im, z_re, z_im)
    return out


def _ops_kernel(seg_steps, pw_ref, pv_ref, pa_ref, bbt_ref, c_ref, csel_ref, dcol_ref,
                mt_ref, wt_ref, vt_ref, a_ref, w_scr):
    gb = mt_ref.shape[0]
    f = mt_ref.shape[-1]
    H = SSM_GROUP
    L = f // H
    half = STATE * N_DIR
    lane = lax.broadcasted_iota(jnp.int32, (H, 2 * f), 1)
    row = lax.broadcasted_iota(jnp.int32, (H, 2 * f), 0)
    centre = lane == (L - 1) * H + row
    zeros = jnp.zeros((H, f), F32)
    for g in range(gb):
        bt_re, bt_im = bbt_ref[g, 0], bbt_ref[g, 1]
        c_re, c_im = c_ref[g, 0], c_ref[g, 1]
        for t in range(L):
            rows = slice(t * H, (t + 1) * H)
            w_re, w_im = _cmul(pw_ref[g, 0, t:t + 1, :], pw_ref[g, 1, t:t + 1, :], bt_re, bt_im)
            w_scr[rows, :half] = w_re
            w_scr[rows, half:] = w_im
            g_re, g_im = _cmul(pv_ref[g, 0, t:t + 1, :], pv_ref[g, 1, t:t + 1, :], c_re, c_im)
            vt_ref[g, rows, :half] = g_re.astype(vt_ref.dtype)
            vt_ref[g, rows, half:] = (-g_im).astype(vt_ref.dtype)
        wt = w_scr[...].T
        wt_ref[g] = wt.astype(wt_ref.dtype)
        kk = jnp.dot(csel_ref[g], wt, preferred_element_type=F32, precision=lax.Precision.HIGHEST)
        kf = jnp.concatenate([kk[:H], zeros], axis=1)
        kb = pltpu.roll(jnp.concatenate([kk[H:], zeros], axis=1), (L - 1) * H, 1)
        dmat = jnp.concatenate([dcol_ref[g]] * (2 * f // LANES), axis=1)
        k = kf + kb + jnp.where(centre, dmat, 0.0)
        for t in range(L):
            off = f - (t + 1) * H
            win = k if off == 0 else pltpu.roll(k, 2 * f - off, 1)
            mt_ref[g, t * H:(t + 1) * H, :] = win[:, :f].astype(mt_ref.dtype)
        ac_re, ac_im = pa_ref[g, 0:1, :], pa_ref[g, 1:2, :]
        as_re, as_im = _cpow(ac_re, ac_im, seg_steps)
        a_ref[g, 0:1, :] = ac_re
        a_ref[g, 1:2, :] = ac_im
        a_ref[g, 2:3, :] = as_re
        a_ref[g, 3:4, :] = as_im


def _ssm_operators(lam_re, lam_im, log_dt, b_re, b_im, c_re, c_im, d_skip, seg_steps):
    L = SSM_CHUNK
    H = SSM_GROUP
    lam_re = lam_re.astype(F32)
    lam_im = lam_im.astype(F32)
    depth, _, ng, _ = lam_re.shape
    dt = jnp.exp(log_dt.astype(F32))[..., None]
    mag = jnp.exp(lam_re * dt)
    a_re = mag * jnp.cos(lam_im * dt)
    a_im = mag * jnp.sin(lam_im * dt)
    nr, ni = a_re - 1.0, a_im
    den = lam_re * lam_re + lam_im * lam_im
    f_re = (nr * lam_re + ni * lam_im) / den
    f_im = (ni * lam_re - nr * lam_im) / den
    br, bi = b_re.astype(F32), b_im.astype(F32)
    bb_re = f_re[..., None] * br - f_im[..., None] * bi
    bb_im = f_re[..., None] * bi + f_im[..., None] * br
    k = jnp.arange(L + 1, dtype=F32)[:, None]
    pmag = jnp.exp(lam_re[..., None, :] * dt[..., None, :] * k)
    parg = lam_im[..., None, :] * dt[..., None, :] * k
    p_re = pmag * jnp.cos(parg)
    p_im = pmag * jnp.sin(parg)
    both = lambda fwd, bwd: jnp.concatenate([fwd, bwd], axis=-1)
    reim = lambda re, im: jnp.stack([re, im], axis=2)
    pw = reim(both(p_re[:, 0, :, L - 1::-1][:, :, :L], p_re[:, 1, :, :L]),
              both(p_im[:, 0, :, L - 1::-1][:, :, :L], p_im[:, 1, :, :L]))
    pv = reim(both(p_re[:, 0, :, 1:], p_re[:, 1, :, :0:-1]), both(p_im[:, 0, :, 1:], p_im[:, 1, :, :0:-1]))
    pa = jnp.stack([both(p_re[:, 0, :, L], p_re[:, 1, :, L]), both(p_im[:, 0, :, L], p_im[:, 1, :, L])], axis=2)
    sw = lambda z: jnp.swapaxes(z, -1, -2)
    bbt = reim(both(sw(bb_re[:, 0]), sw(bb_re[:, 1])), both(sw(bb_im[:, 0]), sw(bb_im[:, 1])))
    cr, ci = c_re.astype(F32), c_im.astype(F32)
    cc = reim(both(cr[:, 0], cr[:, 1]), both(ci[:, 0], ci[:, 1]))
    z = jnp.zeros_like(cr[:, 0])
    csel = jnp.concatenate([jnp.concatenate([cr[:, 0], z, -ci[:, 0], z], axis=-1),
                            jnp.concatenate([z, cr[:, 1], z, -ci[:, 1]], axis=-1)], axis=-2)
    dcol = jnp.broadcast_to(d_skip.astype(F32).reshape(depth, ng, H, 1), (depth, ng, H, LANES))
    f = L * H
    ns = 2 * N_DIR * STATE
    gb = 8
    blk = lambda *shape: pl.BlockSpec((None, gb) + shape, lambda l, j: (l, j) + (0,) * len(shape))
    return pl.pallas_call(
        functools.partial(_ops_kernel, seg_steps),
        grid=(depth, ng // gb),
        in_specs=[blk(2, L, ns // 2), blk(2, L, ns // 2), blk(2, ns // 2), blk(2, H, ns // 2),
                  blk(2, H, ns // 2), blk(2 * H, ns), blk(H, LANES)],
        out_specs=[blk(f, f), blk(ns, f), blk(f, ns), blk(4, ns // 2)],
        out_shape=[jax.ShapeDtypeStruct((depth, ng, f, f), BF16), jax.ShapeDtypeStruct((depth, ng, ns, f), BF16),
                   jax.ShapeDtypeStruct((depth, ng, f, ns), BF16), jax.ShapeDtypeStruct((depth, ng, 4, ns // 2), F32)],
        scratch_shapes=[pltpu.VMEM((f, ns), F32)],
        compiler_params=pltpu.CompilerParams(
            dimension_semantics=("parallel", "parallel"), vmem_limit_bytes=VMEM_LIMIT_BYTES),
        name="ssm_operators",
    )(pw, pv, pa, bbt, cc, csel, dcol)


def _gelu_tanh(x):
    return 0.5 * x * (1.0 + jnp.tanh(np.sqrt(2.0 / np.pi).astype(np.float32)
                                     * (x + np.float32(0.044715) * (x * x * x))))


def _mixer_kernel(ctx_tiles, ctx_len, first, x_ref, *refs):
    if first:
        c_ref, *refs = refs
    (ysl_ref, ysc_ref, mod_ref, g_ref, wa_ref, ba_ref, wg_ref, bg_ref, cw_ref,
     woa_ref, wglu_ref, bglu_ref, wo_ref, o_ref) = refs
    tm, d = x_ref.shape
    dc = woa_ref.shape[0]
    n = ysc_ref.shape[-1]
    is_ctx = pl.program_id(1) < ctx_tiles
    row = jnp.where(is_ctx, 2, pl.program_id(0))
    shift, scale, gate = (_mod_vec(mod_ref, row, k, d) for k in range(3))
    nb = MIXER_ROW_BLOCKS
    rows = tm // nb
    assert rows % ctx_len == 0 and rows % SSM_CHUNK == 0
    t = lax.broadcasted_iota(jnp.int32, (rows, 1), 0)
    seg = jnp.where(is_ctx, ctx_len - 1, GRID_W - 1)
    pos = t & seg
    blocks = [slice(k * rows, (k + 1) * rows) for k in range(nb)]
    xs, za, zg = [], [], []
    for k, r in enumerate(blocks):
        x = x_ref[r, :]
        if first:
            x = jnp.where(is_ctx, _first_tile(x_ref, c_ref)[r, :], x)
        h = _norm_mod(x, g_ref[...], shift, scale).astype(BF16)
        xs.append(x)
        za.append(jnp.dot(h, wa_ref[...], preferred_element_type=F32) + ba_ref[...])
        zg.append(jnp.dot(h, wg_ref[...], preferred_element_type=F32) + bg_ref[...])
    y_a, y_b = [], []
    for k, r in enumerate(blocks):
        g_b, g_c, x_in = za[k][:, :dc], za[k][:, dc:2 * dc], za[k][:, 2 * dc:]
        v = g_c * x_in
        v_prev = jnp.where(pos == 0, 0.0, pltpu.roll(v, 1, 0))
        v_next = jnp.where(pos == seg, 0.0, pltpu.roll(v, rows - 1, 0))
        cv = cw_ref[0:1, :] * v_prev + cw_ref[1:2, :] * v + cw_ref[2:3, :] * v_next
        y_a.append(jnp.dot((g_b * cv).astype(BF16), woa_ref[...], preferred_element_type=F32))
        ys = ysl_ref[k * rows // SSM_CHUNK:(k + 1) * rows // SSM_CHUNK, :SSM_CHUNK, :].reshape(rows, n)
        ys_ctx = jnp.concatenate([ysc_ref[...]] * (rows // ctx_len), axis=0)
        s = _gelu_tanh(jnp.where(is_ctx, ys_ctx, ys))
        gl = jnp.dot(s.astype(BF16), wglu_ref[...], preferred_element_type=F32) + bglu_ref[...]
        y_b.append(gl[:, :d] * jax.nn.sigmoid(gl[:, d:]))
    for k, r in enumerate(blocks):
        gate_a, gate_b = zg[k][:, :d], zg[k][:, d:]
        merged = jax.nn.sigmoid(gate_a) * y_a[k] + jax.nn.sigmoid(gate_b) * y_b[k]
        out = jnp.dot(merged.astype(BF16), wo_ref[...], preferred_element_type=F32)
        o_ref[r, :] = xs[k] + gate * out


def _mixer(tokens, ys_lat, ys_ctx, params, ctx_tiles, ctx_len):
    first = len(tokens) == 2
    bsz, _, d = tokens[0].shape
    s = tokens[0].shape[1] + (ctx_tiles * TOK_TILE if first else 0)
    n = ys_ctx.shape[-1]
    steps = ys_lat.shape[1]
    cpt, index = _chunk_tile_spec(n, ctx_tiles)
    tok = pl.BlockSpec((None, TOK_TILE, d), lambda b, i: (b, i, 0))
    arrays, specs = params
    return pl.pallas_call(
        functools.partial(_mixer_kernel, ctx_tiles, ctx_len, first),
        grid=(bsz, s // TOK_TILE),
        in_specs=(_first_specs(d, ctx_tiles, ctx_len) if first else [tok])
        + [pl.BlockSpec((None, cpt, None, CHUNK_PITCH, n), lambda b, i: index(b, i, steps // cpt)),
           pl.BlockSpec((None, ctx_len, n), lambda b, i: (b, 0, 0))]
        + specs,
        out_specs=tok,
        out_shape=jax.ShapeDtypeStruct((bsz, s, d), F32),
        compiler_params=pltpu.CompilerParams(
            dimension_semantics=("parallel", "parallel"), vmem_limit_bytes=VMEM_LIMIT_BYTES),
        name="mixer",
    )(*tokens, ys_lat, ys_ctx, *arrays)


def _ffn_kernel(ctx_tiles, tile_offset, final, x_ref, mod_ref, g_ref, win_ref, wout_ref, *rest):
    tm, d = x_ref.shape
    dff = wout_ref.shape[0]
    is_ctx = pl.program_id(1) + tile_offset < ctx_tiles
    row = jnp.where(is_ctx, 2, pl.program_id(0))
    shift, scale, gate_mod = (_mod_vec(mod_ref, row, k, d) for k in (3, 4, 5))
    rows = tm // FFN_ROW_BLOCKS
    blocks = [slice(k * rows, (k + 1) * rows) for k in range(FFN_ROW_BLOCKS)]
    xs, zs = [], []
    for r in blocks:
        x = x_ref[r, :]
        h = _norm_mod(x, g_ref[...], shift, scale)
        xs.append(x)
        zs.append(jnp.dot(h.astype(BF16), win_ref[...], preferred_element_type=F32))
    ys = []
    for x, z in zip(xs, zs):
        gate, up = z[:, :dff], z[:, dff:]
        act = gate * jax.nn.sigmoid(gate) * up
        out = jnp.dot(act.astype(BF16), wout_ref[...], preferred_element_type=F32)
        ys.append(x + gate_mod * out)
    if final:
        fg_ref, o_ref = rest
        for r, y in zip(blocks, ys):
            ms = jnp.mean(y * y, axis=-1, keepdims=True)
            o_ref[r, :] = y * lax.rsqrt(ms + RMS_EPS) * fg_ref[...]
    else:
        modn_ref, gn_ref, wu_ref, bu_ref, o_ref, ul_ref, uc_ref = rest
        for r, y in zip(blocks, ys):
            o_ref[r, :] = y
        _emit_ssm_inputs(jnp.concatenate(ys, axis=0), row, is_ctx, modn_ref, gn_ref, wu_ref, bu_ref, ul_ref, uc_ref)


def _ffn(xs, params, ctx_tiles, ctx_len, final_g=None, nxt=None):
    bsz, s, d = xs.shape
    final = final_g is not None
    tile_offset = ctx_tiles if final else 0
    s_out = s - tile_offset * TOK_TILE
    arrays, specs = params
    if final:
        arrays, specs = arrays + [final_g], specs + [_layer_spec(final_g, 0)]
    else:
        arrays, specs = arrays + nxt[0], specs + nxt[1]
    out_specs = [pl.BlockSpec((None, TOK_TILE, d), lambda b, i: (b, i, 0))]
    out_shapes = [jax.ShapeDtypeStruct((bsz, s_out, d), F32)]
    if not final:
        u_specs, u_shapes = _ssm_input_specs(bsz, s, nxt[2], ctx_tiles, ctx_len)
        out_specs, out_shapes = out_specs + u_specs, out_shapes + u_shapes
    return pl.pallas_call(
        functools.partial(_ffn_kernel, ctx_tiles, tile_offset, final),
        grid=(bsz, s_out // TOK_TILE),
        in_specs=[pl.BlockSpec((None, TOK_TILE, d), lambda b, i: (b, i + tile_offset, 0))] + specs,
        out_specs=out_specs,
        out_shape=out_shapes,
        compiler_params=pltpu.CompilerParams(
            dimension_semantics=("arbitrary", "arbitrary"), vmem_limit_bytes=VMEM_LIMIT_BYTES),
        name="ffn_final" if final else "ffn",
    )(xs, *arrays)


def kernel(x, c, ctx, c_ctx, w_mod, b_mod, norm1_g, norm2_g, w_in, b_in, conv_w, w_out_a, lam_re, lam_im,
           log_dt, b_re, b_im, c_re, c_im, d_skip, w_glu, b_glu, w_o, w_ff_in, w_ff_out, final_g):
    bsz, seq, d = x.shape
    depth = w_mod.shape[0]
    ctx_len = ctx.shape[1]
    d_conv = conv_w.shape[-1]
    d_ssm = d_skip.shape[-1]
    u_lo, u_hi = 3 * d_conv, 3 * d_conv + d_ssm
    assert TOK_TILE % ctx_len == 0 and TOK_TILE % GRID_W == 0
    assert ctx_len == SSM_CHUNK * SUBLANES and d_ssm % LANES == 0 and bsz <= 2
    assert seq % (SUBLANES * TOK_TILE) == 0
    seg_steps = seq // (SSM_CHUNK * SUBLANES)
    ctx_tiles = 1

    cvec = jnp.zeros((8, d), F32).at[:bsz].set(c.astype(F32)).at[2].set(c_ctx.astype(F32))
    mod = _mod_all(cvec, w_mod, b_mod)

    mt_op, wt_op, vt_op, a_op = _ssm_operators(lam_re, lam_im, log_dt, b_re, b_im, c_re, c_im, d_skip,
                                               seg_steps)

    w_in, w_out_a, w_glu, w_o = (w.astype(BF16) for w in (w_in, w_out_a, w_glu, w_o))
    w_ff_in, w_ff_out = w_ff_in.astype(BF16), w_ff_out.astype(BF16)
    row = lambda v: v.reshape(v.shape[0], 1, v.shape[-1])
    b_in, b_glu, g1, g2, fg = row(b_in), row(b_glu), row(norm1_g), row(norm2_g), final_g.reshape(1, 1, d)
    gates_lo = u_hi

    def operands(l, *items):
        arrays = [arr for arr, _ in items]
        return arrays, [_layer_spec(arr, l, cols) for arr, cols in items]

    u_cols = (u_lo, d_ssm)
    ssm_in = lambda l: operands(l, (mod, None), (g1, None), (w_in, u_cols), (b_in, u_cols)) + (d_ssm,)
    mixer_in = lambda l: operands(
        l, (mod, None), (g1, None), (w_in, (0, u_lo)), (b_in, (0, u_lo)),
        (w_in, (gates_lo, 2 * d)), (b_in, (gates_lo, 2 * d)), (conv_w, None), (w_out_a, None),
        (w_glu, None), (b_glu, None), (w_o, None))
    ffn_in = lambda l: operands(l, (mod, None), (g2, None), (w_ff_in, None), (w_ff_out, None))

    u_lat, u_ctx = _uproj(x, ctx, ssm_in(0), ctx_tiles)
    xs = None
    for l in range(depth):
        ys_lat, ys_ctx = _ssm(u_lat, u_ctx, mt_op, wt_op, vt_op, a_op, l)
        xs = _mixer((x, ctx) if l == 0 else (xs,), ys_lat, ys_ctx, mixer_in(l), ctx_tiles, ctx_len)
        if l + 1 < depth:
            xs, u_lat, u_ctx = _ffn(xs, ffn_in(l), ctx_tiles, ctx_len, nxt=ssm_in(l + 1))
        else:
            xs, = _ffn(xs, ffn_in(l), ctx_tiles, ctx_len, final_g=fg)
    return xs
```

```python
import functools

import numpy as np
import jax
import jax.numpy as jnp
from jax import lax
from jax.experimental import pallas as pl
from jax.experimental.pallas import tpu as pltpu

GRID_W = 64
CONV_K = 3
SSM_GROUP = 16
STATE = 64
N_DIR = 2
RMS_EPS = 1e-6

TOK_TILE = 512
SSM_CHUNK = 32
CHUNK_PITCH = 40
SUBLANES = 8
LANES = 128
SSM_LANE_GROUPS = LANES // SSM_GROUP
SSM_SCAN_GROUPS = 4
SSM_IO_PARTS = 4
MIXER_ROW_BLOCKS = 2
FFN_ROW_BLOCKS = 2
VMEM_LIMIT_BYTES = 56 * 1024 * 1024

F32 = jnp.float32
BF16 = jnp.bfloat16


def _layer_spec(arr, l, cols=None):
    _, rows, n = arr.shape
    start, width = (0, n) if cols is None else cols
    assert start % width == 0
    return pl.BlockSpec((None, rows, width), lambda *_: (l, 0, start // width), pipeline_mode=pl.Buffered(1))


def _tile_coords(s, tiles_per_seq):
    k = jnp.maximum(s - 1, 0)
    return k // tiles_per_seq, k % tiles_per_seq


def _token_specs(d, tiles_per_seq):
    return [pl.BlockSpec((None, TOK_TILE, d), lambda s: (*_tile_coords(s, tiles_per_seq), 0)),
            pl.BlockSpec((TOK_TILE, d), lambda s: (0, 0))]


def _chunk_specs(n, tiles_per_seq):
    cpt = TOK_TILE // SSM_CHUNK
    tiles_per_seg = tiles_per_seq // SUBLANES

    def index(s):
        b, i = _tile_coords(s, tiles_per_seq)
        return (b, i % tiles_per_seg, i // tiles_per_seg, 0, 0)

    return [pl.BlockSpec((None, cpt, None, CHUNK_PITCH, n), index),
            pl.BlockSpec((TOK_TILE, n), lambda s: (0, 0))]


def _tile_kind(tiles_per_seq, first_step=0):
    s = pl.program_id(0) + first_step
    is_ctx = s == 0
    return is_ctx, jnp.where(is_ctx, 2, _tile_coords(s, tiles_per_seq)[0])


def _norm_mod(x, g, shift, scale):
    ms = jnp.mean(x * x, axis=-1, keepdims=True)
    y = x * lax.rsqrt(ms + RMS_EPS) * g
    return y * (1.0 + scale) + shift


def _mod_vec(mod_ref, row, k, d):
    return mod_ref[pl.ds(row, 1), k * d:(k + 1) * d]


def _mod_kernel(c_ref, w_ref, b_ref, o_ref):
    c = c_ref[...]
    s = c * jax.nn.sigmoid(c)
    o_ref[...] = jnp.dot(s.astype(BF16), w_ref[...].astype(BF16),
                         preferred_element_type=F32) + b_ref[...]


def _mod_all(cvec, w_mod, b_mod):
    depth, d, n = w_mod.shape
    tn = 1536
    return pl.pallas_call(
        _mod_kernel,
        grid=(depth, n // tn),
        in_specs=[pl.BlockSpec((8, d), lambda l, j: (0, 0)),
                  pl.BlockSpec((None, d, tn), lambda l, j: (l, 0, j)),
                  pl.BlockSpec((None, 1, tn), lambda l, j: (l, 0, j))],
        out_specs=pl.BlockSpec((None, 8, tn), lambda l, j: (l, 0, j)),
        out_shape=jax.ShapeDtypeStruct((depth, 8, n), F32),
        compiler_params=pltpu.CompilerParams(
            dimension_semantics=("arbitrary", "arbitrary"), vmem_limit_bytes=VMEM_LIMIT_BYTES),
        name="adaln_mod",
    )(cvec, w_mod, b_mod.reshape(depth, 1, n))


def _emit_ssm_inputs(x, row, is_ctx, mod_ref, g_ref, w_ref, b_ref, ul_ref, uc_ref):
    d = x.shape[-1]
    h = _norm_mod(x, g_ref[...], _mod_vec(mod_ref, row, 0, d), _mod_vec(mod_ref, row, 1, d))
    u = jnp.dot(h.astype(BF16), w_ref[...], preferred_element_type=F32) + b_ref[...]
    cpt, pitch, n = ul_ref.shape
    ul_ref[:, :SSM_CHUNK, :] = u.reshape(cpt, SSM_CHUNK, n)
    ul_ref[:, SSM_CHUNK:, :] = jnp.zeros((cpt, pitch - SSM_CHUNK, n), F32)

    @pl.when(is_ctx)
    def _():
        uc_ref[...] = u


def _uproj_kernel(tiles_per_seq, xl_ref, xc_ref, mod_ref, g_ref, w_ref, b_ref, ul_ref, uc_ref):
    is_ctx, row = _tile_kind(tiles_per_seq)
    x = jnp.where(is_ctx, xc_ref[...], xl_ref[...])
    _emit_ssm_inputs(x, row, is_ctx, mod_ref, g_ref, w_ref, b_ref, ul_ref, uc_ref)


def _ssm_input_shapes(bsz, seq, n):
    steps = seq // (SSM_CHUNK * SUBLANES)
    return [jax.ShapeDtypeStruct((bsz, steps, SUBLANES, CHUNK_PITCH, n), F32),
            jax.ShapeDtypeStruct((TOK_TILE, n), F32)]


def _uproj(xl, xc, ssm_in):
    bsz, seq, d = xl.shape
    tiles_per_seq = seq // TOK_TILE
    arrays, specs, n_u = ssm_in
    return pl.pallas_call(
        functools.partial(_uproj_kernel, tiles_per_seq),
        grid=(1 + bsz * tiles_per_seq,),
        in_specs=_token_specs(d, tiles_per_seq) + specs,
        out_specs=_chunk_specs(n_u, tiles_per_seq),
        out_shape=_ssm_input_shapes(bsz, seq, n_u),
        compiler_params=pltpu.CompilerParams(
            dimension_semantics=("arbitrary",), vmem_limit_bytes=VMEM_LIMIT_BYTES),
        name="ssm_uproj",
    )(xl, xc, *arrays)


def _cmul(a_re, a_im, b_re, b_im):
    return a_re * b_re - a_im * b_im, a_re * b_im + a_im * b_re


def _ssm_kernel(ctx_chunks, ul_ref, uc_ref, mt_ref, wt_ref, vt_ref, a_ref, yl_ref, yc_ref,
                ut_scr, yt_scr, x_scr, hf_scr, hb_scr):
    L = SSM_CHUNK
    H = SSM_GROUP
    parts = SSM_IO_PARTS
    n_part = ul_ref.shape[0] // CHUNK_PITCH
    n_lat = n_part * parts
    n_all = ut_scr.shape[-1]
    phase = pl.program_id(2)
    trow = lambda t: pl.ds(pl.multiple_of(t * H, H), H)

    def fill(cols, rows_of_t):
        def body(t, carry):
            at = rows_of_t(t).T
            for g in range(SSM_LANE_GROUPS):
                ut_scr[g, trow(t), cols] = at[g * H:(g + 1) * H, :].astype(BF16)
            return carry
        lax.fori_loop(0, L, body, 0, unroll=16)

    for q in range(parts):
        @pl.when(phase == q)
        def _():
            fill(slice(q * n_part, (q + 1) * n_part), lambda t: ul_ref[pl.ds(t, n_part, stride=CHUNK_PITCH), :])

    @pl.when(phase == parts)
    def _():
        pad = jnp.zeros((n_all - n_lat - ctx_chunks, LANES), F32)
        fill(slice(n_lat, n_all),
             lambda t: jnp.concatenate([uc_ref[pl.ds(t, ctx_chunks, stride=L), :], pad], axis=0))
        _ssm_compute(ctx_chunks, n_lat, mt_ref, wt_ref, vt_ref, a_ref, ut_scr, yt_scr, x_scr, hf_scr, hb_scr)

        def drain_ctx(t, carry):
            y = yt_scr[t, :, n_lat:].astype(F32).T
            yc_ref[pl.ds(t, ctx_chunks, stride=L), :] = y[:ctx_chunks]
            return carry
        lax.fori_loop(0, L, drain_ctx, 0, unroll=16)

    for q in range(parts):
        @pl.when(phase == parts + 1 + q)
        def _():
            def drain(t, carry):
                y = yt_scr[t, :, q * n_part:(q + 1) * n_part].astype(F32).T
                yl_ref[pl.ds(t, n_part, stride=CHUNK_PITCH), :] = y
                return carry
            lax.fori_loop(0, L, drain, 0, unroll=16)
            for t in range(L, CHUNK_PITCH):
                yl_ref[pl.ds(t, n_part, stride=CHUNK_PITCH), :] = jnp.zeros((n_part, LANES), F32)


def _ssm_compute(ctx_chunks, n_lat, mt_ref, wt_ref, vt_ref, a_ref, ut_scr, yt_scr, x_scr, hf_scr, hb_scr):
    L = SSM_CHUNK
    H = SSM_GROUP
    sub = SUBLANES
    n_all = ut_scr.shape[-1]
    steps = n_lat // sub
    half = STATE * N_DIR
    gl = SSM_LANE_GROUPS
    gs = SSM_SCAN_GROUPS

    is_fwd = lax.broadcasted_iota(jnp.int32, (sub, half), 1) < STATE
    is_fwd1 = is_fwd[:1]
    sl = lax.broadcasted_iota(jnp.int32, (sub, half), 0)
    blk = lambda j: pl.ds(pl.multiple_of(j * sub, sub), sub)
    zero = jnp.zeros((sub, half), F32)
    lane2 = lax.broadcasted_iota(jnp.int32, (n_all, 2 * half), 1)
    take_fwd = (lane2 % half) < STATE
    tail0 = n_lat + ctx_chunks
    assert tail0 % sub == 0 and ctx_chunks == sub

    for g0 in range(0, gl, gs):
        coef = lambda g, r, rows=sub: jnp.broadcast_to(a_ref[g0 + g, r:r + 1, :], (rows, half))
        a_re = [coef(g, 0) for g in range(gs)]
        a_im = [coef(g, 1) for g in range(gs)]

        for g in range(gs):
            xt = jnp.dot(wt_ref[g0 + g], ut_scr[g0 + g], preferred_element_type=F32)
            x_scr[g] = xt.T
            hf_scr[g, tail0:, :] = jnp.zeros((n_all - tail0, 2 * half), F32)
            hb_scr[g, tail0:, :] = jnp.zeros((n_all - tail0, 2 * half), F32)

        h0 = []
        for g in range(gs):
            xc = x_scr[g, n_lat:n_lat + ctx_chunks, :]
            ar, ai = a_re[g][:1], a_im[g][:1]
            h_re = h_im = jnp.zeros((1, half), F32)
            hf_rows, hb_rows = [], [None] * ctx_chunks
            for k in range(ctx_chunks):
                kb = ctx_chunks - 1 - k
                hf_rows.append((h_re, h_im))
                hb_rows[kb] = (h_re, h_im)
                x_re = jnp.where(is_fwd1, xc[k:k + 1, :half], xc[kb:kb + 1, :half])
                x_im = jnp.where(is_fwd1, xc[k:k + 1, half:], xc[kb:kb + 1, half:])
                p_re, p_im = _cmul(ar, ai, h_re, h_im)
                h_re, h_im = p_re + x_re, p_im + x_im
            cat = lambda rows, part: jnp.concatenate([r[part] for r in rows], axis=0)
            hf_scr[g, n_lat:n_lat + ctx_chunks, :half] = cat(hf_rows, 0)
            hf_scr[g, n_lat:n_lat + ctx_chunks, half:] = cat(hf_rows, 1)
            hb_scr[g, n_lat:n_lat + ctx_chunks, :half] = cat(hb_rows, 0)
            hb_scr[g, n_lat:n_lat + ctx_chunks, half:] = cat(hb_rows, 1)
            h0.append((jnp.broadcast_to(h_re, (sub, half)), jnp.broadcast_to(h_im, (sub, half))))

        def scan_step(j, carry):
            jb = steps - 1 - j
            new = []
            for g in range(gs):
                h_re, h_im = carry[g]
                x_re = jnp.where(is_fwd, x_scr[g, blk(j), :half], x_scr[g, blk(jb), :half])
                x_im = jnp.where(is_fwd, x_scr[g, blk(j), half:], x_scr[g, blk(jb), half:])
                hf_scr[g, blk(j), :half] = h_re
                hf_scr[g, blk(j), half:] = h_im
                hb_scr[g, blk(jb), :half] = h_re
                hb_scr[g, blk(jb), half:] = h_im
                p_re, p_im = _cmul(a_re[g], a_im[g], h_re, h_im)
                new.append((p_re + x_re, p_im + x_im))
            return tuple(new)

        ends = lax.fori_loop(0, steps, scan_step, tuple((zero, zero) for _ in range(gs)), unroll=2)

        carries = []
        for g in range(gs):
            def shift(t, h):
                return jnp.where(is_fwd, jnp.where(sl < 1, h, pltpu.roll(t, 1, 0)),
                                 jnp.where(sl >= sub - 1, h, pltpu.roll(t, sub - 1, 0)))
            e_re, e_im = ends[g]
            s_re, s_im = coef(g, 2), coef(g, 3)
            c_re, c_im = shift(zero, h0[g][0]), shift(zero, h0[g][1])
            for _ in range(sub - 1):
                p_re, p_im = _cmul(s_re, s_im, c_re, c_im)
                c_re, c_im = shift(e_re + p_re, h0[g][0]), shift(e_im + p_im, h0[g][1])
            carries.append((c_re, c_im))

        def fix_step(j, carry):
            jb = steps - 1 - j
            new = []
            for g in range(gs):
                d_re, d_im = carry[g]
                hf_scr[g, blk(j), :half] += d_re
                hf_scr[g, blk(j), half:] += d_im
                hb_scr[g, blk(jb), :half] += d_re
                hb_scr[g, blk(jb), half:] += d_im
                new.append(_cmul(a_re[g], a_im[g], d_re, d_im))
            return tuple(new)

        lax.fori_loop(0, steps, fix_step, tuple(carries), unroll=2)

        for g in range(gs):
            h_in = jnp.where(take_fwd, hf_scr[g], hb_scr[g]).T.astype(BF16)
            yt = (jnp.dot(mt_ref[g0 + g], ut_scr[g0 + g], preferred_element_type=F32)
                  + jnp.dot(vt_ref[g0 + g], h_in, preferred_element_type=F32))
            for t in range(L):
                yt_scr[t, (g0 + g) * H:(g0 + g + 1) * H, :] = yt[t * H:(t + 1) * H, :].astype(BF16)


def _ssm(u_lat, u_ctx, mt, wt, vt, a, l):
    bsz, steps, nseg, pitch, n = u_lat.shape
    L = SSM_CHUNK
    ctx_len = u_ctx.shape[0] // bsz
    ctx_chunks = ctx_len // L
    n_lat = steps * nseg
    n_all = -(-(n_lat + ctx_chunks) // LANES) * LANES
    gl = SSM_LANE_GROUPS
    f = L * SSM_GROUP
    ns = wt.shape[-2]
    parts = SSM_IO_PARTS
    n_blocks = n // LANES
    part_rows = n_lat * pitch // parts
    lat_in = pl.BlockSpec((None, part_rows, LANES), lambda b, j, p: (b, jnp.minimum(p, parts - 1), j))
    lat_out = pl.BlockSpec((None, part_rows, LANES), lambda b, j, p: (b, jnp.maximum(p - parts - 1, 0), j))
    ctx_spec = pl.BlockSpec((None, ctx_len, LANES), lambda b, j, p: (b, 0, j))
    grp = lambda shape: pl.BlockSpec(
        (None, gl) + shape, lambda b, j, p: (l, jnp.where(p > parts, (j + 1) % n_blocks, j), 0, 0))
    y_lat, y_ctx = pl.pallas_call(
        functools.partial(_ssm_kernel, ctx_chunks),
        grid=(bsz, n_blocks, 2 * parts + 1),
        in_specs=[lat_in, ctx_spec, grp((f, f)), grp((ns, f)), grp((f, ns)), grp((4, ns // 2))],
        out_specs=[lat_out, ctx_spec],
        out_shape=[jax.ShapeDtypeStruct((bsz, n_lat * pitch, n), F32),
                   jax.ShapeDtypeStruct((bsz, ctx_len, n), F32)],
        scratch_shapes=[pltpu.VMEM((gl, f, n_all), BF16), pltpu.VMEM((L, LANES, n_all), BF16)]
        + [pltpu.VMEM((SSM_SCAN_GROUPS, n_all, ns), F32)] * 3,
        compiler_params=pltpu.CompilerParams(
            dimension_semantics=("arbitrary", "arbitrary", "arbitrary"), vmem_limit_bytes=VMEM_LIMIT_BYTES),
        name="ssm_chunked",
    )(u_lat.reshape(bsz, n_lat * pitch, n), u_ctx.reshape(bsz, ctx_len, n), mt, wt, vt, a)
    return y_lat.reshape(u_lat.shape), y_ctx.reshape(u_ctx.shape)


def _cpow(z_re, z_im, n):
    out = None
    while n:
        if n & 1:
            out = (z_re, z_im) if out is None else _cmul(out[0], out[1], z_re, z_im)
        n >>= 1
        if n:
            z_re, z_im = _cmul(z_re, z_im, z_re, z_im)
    return out


def _ops_kernel(seg_steps, pw_ref, pv_ref, pa_ref, bbt_ref, c_ref, csel_ref, dcol_ref,
                mt_ref, wt_ref, vt_ref, a_ref, w_scr):
    gb = mt_ref.shape[0]
    f = mt_ref.shape[-1]
    H = SSM_GROUP
    L = f // H
    half = STATE * N_DIR
    lane = lax.broadcasted_iota(jnp.int32, (H, 2 * f), 1)
    row = lax.broadcasted_iota(jnp.int32, (H, 2 * f), 0)
    centre = lane == (L - 1) * H + row
    zeros = jnp.zeros((H, f), F32)
    for g in range(gb):
        bt_re, bt_im = bbt_ref[g, 0], bbt_ref[g, 1]
        c_re, c_im = c_ref[g, 0], c_ref[g, 1]
        for t in range(L):
            rows = slice(t * H, (t + 1) * H)
            w_re, w_im = _cmul(pw_ref[g, 0, t:t + 1, :], pw_ref[g, 1, t:t + 1, :], bt_re, bt_im)
            w_scr[rows, :half] = w_re
            w_scr[rows, half:] = w_im
            g_re, g_im = _cmul(pv_ref[g, 0, t:t + 1, :], pv_ref[g, 1, t:t + 1, :], c_re, c_im)
            vt_ref[g, rows, :half] = g_re.astype(vt_ref.dtype)
            vt_ref[g, rows, half:] = (-g_im).astype(vt_ref.dtype)
        wt = w_scr[...].T
        wt_ref[g] = wt.astype(wt_ref.dtype)
        kk = jnp.dot(csel_ref[g], wt, preferred_element_type=F32, precision=lax.Precision.HIGHEST)
        kf = jnp.concatenate([kk[:H], zeros], axis=1)
        kb = pltpu.roll(jnp.concatenate([kk[H:], zeros], axis=1), (L - 1) * H, 1)
        dmat = jnp.concatenate([dcol_ref[g]] * (2 * f // LANES), axis=1)
        k = kf + kb + jnp.where(centre, dmat, 0.0)
        for t in range(L):
            off = f - (t + 1) * H
            win = k if off == 0 else pltpu.roll(k, 2 * f - off, 1)
            mt_ref[g, t * H:(t + 1) * H, :] = win[:, :f].astype(mt_ref.dtype)
        ac_re, ac_im = pa_ref[g, 0:1, :], pa_ref[g, 1:2, :]
        as_re, as_im = _cpow(ac_re, ac_im, seg_steps)
        a_ref[g, 0:1, :] = ac_re
        a_ref[g, 1:2, :] = ac_im
        a_ref[g, 2:3, :] = as_re
        a_ref[g, 3:4, :] = as_im


def _ssm_operators(lam_re, lam_im, log_dt, b_re, b_im, c_re, c_im, d_skip, seg_steps):
    L = SSM_CHUNK
    H = SSM_GROUP
    lam_re = lam_re.astype(F32)
    lam_im = lam_im.astype(F32)
    depth, _, ng, _ = lam_re.shape
    dt = jnp.exp(log_dt.astype(F32))[..., None]
    mag = jnp.exp(lam_re * dt)
    a_re = mag * jnp.cos(lam_im * dt)
    a_im = mag * jnp.sin(lam_im * dt)
    nr, ni = a_re - 1.0, a_im
    den = lam_re * lam_re + lam_im * lam_im
    f_re = (nr * lam_re + ni * lam_im) / den
    f_im = (ni * lam_re - nr * lam_im) / den
    br, bi = b_re.astype(F32), b_im.astype(F32)
    bb_re = f_re[..., None] * br - f_im[..., None] * bi
    bb_im = f_re[..., None] * bi + f_im[..., None] * br
    k = jnp.arange(L + 1, dtype=F32)[:, None]
    pmag = jnp.exp(lam_re[..., None, :] * dt[..., None, :] * k)
    parg = lam_im[..., None, :] * dt[..., None, :] * k
    p_re = pmag * jnp.cos(parg)
    p_im = pmag * jnp.sin(parg)
    both = lambda fwd, bwd: jnp.concatenate([fwd, bwd], axis=-1)
    reim = lambda re, im: jnp.stack([re, im], axis=2)
    pw = reim(both(p_re[:, 0, :, L - 1::-1][:, :, :L], p_re[:, 1, :, :L]),
              both(p_im[:, 0, :, L - 1::-1][:, :, :L], p_im[:, 1, :, :L]))
    pv = reim(both(p_re[:, 0, :, 1:], p_re[:, 1, :, :0:-1]), both(p_im[:, 0, :, 1:], p_im[:, 1, :, :0:-1]))
    pa = jnp.stack([both(p_re[:, 0, :, L], p_re[:, 1, :, L]), both(p_im[:, 0, :, L], p_im[:, 1, :, L])], axis=2)
    sw = lambda z: jnp.swapaxes(z, -1, -2)
    bbt = reim(both(sw(bb_re[:, 0]), sw(bb_re[:, 1])), both(sw(bb_im[:, 0]), sw(bb_im[:, 1])))
    cr, ci = c_re.astype(F32), c_im.astype(F32)
    cc = reim(both(cr[:, 0], cr[:, 1]), both(ci[:, 0], ci[:, 1]))
    z = jnp.zeros_like(cr[:, 0])
    csel = jnp.concatenate([jnp.concatenate([cr[:, 0], z, -ci[:, 0], z], axis=-1),
                            jnp.concatenate([z, cr[:, 1], z, -ci[:, 1]], axis=-1)], axis=-2)
    dcol = jnp.broadcast_to(d_skip.astype(F32).reshape(depth, ng, H, 1), (depth, ng, H, LANES))
    f = L * H
    ns = 2 * N_DIR * STATE
    gb = 8
    blk = lambda *shape: pl.BlockSpec((None, gb) + shape, lambda l, j: (l, j) + (0,) * len(shape))
    return pl.pallas_call(
        functools.partial(_ops_kernel, seg_steps),
        grid=(depth, ng // gb),
        in_specs=[blk(2, L, ns // 2), blk(2, L, ns // 2), blk(2, ns // 2), blk(2, H, ns // 2),
                  blk(2, H, ns // 2), blk(2 * H, ns), blk(H, LANES)],
        out_specs=[blk(f, f), blk(ns, f), blk(f, ns), blk(4, ns // 2)],
        out_shape=[jax.ShapeDtypeStruct((depth, ng, f, f), BF16), jax.ShapeDtypeStruct((depth, ng, ns, f), BF16),
                   jax.ShapeDtypeStruct((depth, ng, f, ns), BF16), jax.ShapeDtypeStruct((depth, ng, 4, ns // 2), F32)],
        scratch_shapes=[pltpu.VMEM((f, ns), F32)],
        compiler_params=pltpu.CompilerParams(
            dimension_semantics=("parallel", "parallel"), vmem_limit_bytes=VMEM_LIMIT_BYTES),
        name="ssm_operators",
    )(pw, pv, pa, bbt, cc, csel, dcol)


def _gelu_tanh(x):
    return 0.5 * x * (1.0 + jnp.tanh(np.sqrt(2.0 / np.pi).astype(np.float32)
                                     * (x + np.float32(0.044715) * (x * x * x))))


def _mixer_kernel(tiles_per_seq, ctx_len, xl_ref, xc_ref, ysl_ref, ysc_ref, mod_ref, g_ref, wa_ref, ba_ref,
                  wg_ref, bg_ref, cw_ref, woa_ref, wglu_ref, bglu_ref, wo_ref, ol_ref, oc_ref):
    tm, d = xl_ref.shape
    dc = woa_ref.shape[0]
    n = ysc_ref.shape[-1]
    is_ctx, row = _tile_kind(tiles_per_seq)
    shift, scale, gate = (_mod_vec(mod_ref, row, k, d) for k in range(3))
    nb = MIXER_ROW_BLOCKS
    rows = tm // nb
    assert rows % ctx_len == 0 and rows % SSM_CHUNK == 0
    t = lax.broadcasted_iota(jnp.int32, (rows, 1), 0)
    seg = jnp.where(is_ctx, ctx_len - 1, GRID_W - 1)
    pos = t & seg
    blocks = [slice(k * rows, (k + 1) * rows) for k in range(nb)]
    xs, za, zg = [], [], []
    for k, r in enumerate(blocks):
        x = jnp.where(is_ctx, xc_ref[r, :], xl_ref[r, :])
        h = _norm_mod(x, g_ref[...], shift, scale).astype(BF16)
        xs.append(x)
        za.append(jnp.dot(h, wa_ref[...], preferred_element_type=F32) + ba_ref[...])
        zg.append(jnp.dot(h, wg_ref[...], preferred_element_type=F32) + bg_ref[...])
    y_a, y_b = [], []
    for k, r in enumerate(blocks):
        g_b, g_c, x_in = za[k][:, :dc], za[k][:, dc:2 * dc], za[k][:, 2 * dc:]
        v = g_c * x_in
        v_prev = jnp.where(pos == 0, 0.0, pltpu.roll(v, 1, 0))
        v_next = jnp.where(pos == seg, 0.0, pltpu.roll(v, rows - 1, 0))
        cv = cw_ref[0:1, :] * v_prev + cw_ref[1:2, :] * v + cw_ref[2:3, :] * v_next
        y_a.append(jnp.dot((g_b * cv).astype(BF16), woa_ref[...], preferred_element_type=F32))
        ys = ysl_ref[k * rows // SSM_CHUNK:(k + 1) * rows // SSM_CHUNK, :SSM_CHUNK, :].reshape(rows, n)
        s = _gelu_tanh(jnp.where(is_ctx, ysc_ref[r, :], ys))
        gl = jnp.dot(s.astype(BF16), wglu_ref[...], preferred_element_type=F32) + bglu_ref[...]
        y_b.append(gl[:, :d] * jax.nn.sigmoid(gl[:, d:]))
    ys_out = []
    for k, r in enumerate(blocks):
        gate_a, gate_b = zg[k][:, :d], zg[k][:, d:]
        merged = jax.nn.sigmoid(gate_a) * y_a[k] + jax.nn.sigmoid(gate_b) * y_b[k]
        out = jnp.dot(merged.astype(BF16), wo_ref[...], preferred_element_type=F32)
        ys_out.append(xs[k] + gate * out)
    for r, y in zip(blocks, ys_out):
        ol_ref[r, :] = y

    @pl.when(is_ctx)
    def _():
        for r, y in zip(blocks, ys_out):
            oc_ref[r, :] = y


def _mixer(xl, xc, ys_lat, ys_ctx, params, ctx_len):
    bsz, seq, d = xl.shape
    tiles_per_seq = seq // TOK_TILE
    arrays, specs = params
    tokens = _token_specs(d, tiles_per_seq)
    return pl.pallas_call(
        functools.partial(_mixer_kernel, tiles_per_seq, ctx_len),
        grid=(1 + bsz * tiles_per_seq,),
        in_specs=tokens + _chunk_specs(ys_ctx.shape[-1], tiles_per_seq) + specs,
        out_specs=tokens,
        out_shape=[jax.ShapeDtypeStruct(xl.shape, F32), jax.ShapeDtypeStruct(xc.shape, F32)],
        compiler_params=pltpu.CompilerParams(
            dimension_semantics=("arbitrary",), vmem_limit_bytes=VMEM_LIMIT_BYTES),
        name="mixer",
    )(xl, xc, ys_lat, ys_ctx, *arrays)


def _ffn_kernel(tiles_per_seq, final, xl_ref, *refs):
    if final:
        mod_ref, g_ref, win_ref, wout_ref, fg_ref, ol_ref = refs
    else:
        (xc_ref, mod_ref, g_ref, win_ref, wout_ref, modn_ref, gn_ref, wu_ref, bu_ref,
         ol_ref, oc_ref, ul_ref, uc_ref) = refs
    tm, d = xl_ref.shape
    dff = wout_ref.shape[0]
    is_ctx, row = _tile_kind(tiles_per_seq, first_step=1 if final else 0)
    shift, scale, gate_mod = (_mod_vec(mod_ref, row, k, d) for k in (3, 4, 5))
    rows = tm // FFN_ROW_BLOCKS
    blocks = [slice(k * rows, (k + 1) * rows) for k in range(FFN_ROW_BLOCKS)]
    xs, zs = [], []
    for r in blocks:
        x = xl_ref[r, :] if final else jnp.where(is_ctx, xc_ref[r, :], xl_ref[r, :])
        h = _norm_mod(x, g_ref[...], shift, scale)
        xs.append(x)
        zs.append(jnp.dot(h.astype(BF16), win_ref[...], preferred_element_type=F32))
    ys = []
    for x, z in zip(xs, zs):
        gate, up = z[:, :dff], z[:, dff:]
        act = gate * jax.nn.sigmoid(gate) * up
        out = jnp.dot(act.astype(BF16), wout_ref[...], preferred_element_type=F32)
        ys.append(x + gate_mod * out)
    if final:
        for r, y in zip(blocks, ys):
            ms = jnp.mean(y * y, axis=-1, keepdims=True)
            ol_ref[r, :] = y * lax.rsqrt(ms + RMS_EPS) * fg_ref[...]
    else:
        for r, y in zip(blocks, ys):
            ol_ref[r, :] = y

        @pl.when(is_ctx)
        def _():
            for r, y in zip(blocks, ys):
                oc_ref[r, :] = y

        _emit_ssm_inputs(jnp.concatenate(ys, axis=0), row, is_ctx, modn_ref, gn_ref, wu_ref, bu_ref, ul_ref, uc_ref)


def _ffn(xl, xc, params, nxt):
    bsz, seq, d = xl.shape
    tiles_per_seq = seq // TOK_TILE
    arrays, specs = params
    tokens = _token_specs(d, tiles_per_seq)
    return pl.pallas_call(
        functools.partial(_ffn_kernel, tiles_per_seq, False),
        grid=(1 + bsz * tiles_per_seq,),
        in_specs=tokens + specs + nxt[1],
        out_specs=tokens + _chunk_specs(nxt[2], tiles_per_seq),
        out_shape=[jax.ShapeDtypeStruct(xl.shape, F32), jax.ShapeDtypeStruct(xc.shape, F32)]
        + _ssm_input_shapes(bsz, seq, nxt[2]),
        compiler_params=pltpu.CompilerParams(
            dimension_semantics=("arbitrary",), vmem_limit_bytes=VMEM_LIMIT_BYTES),
        name="ffn",
    )(xl, xc, *arrays, *nxt[0])


def _ffn_final(xl, params, final_g):
    bsz, seq, d = xl.shape
    tiles_per_seq = seq // TOK_TILE
    arrays, specs = params
    tok = pl.BlockSpec((None, TOK_TILE, d), lambda s: (s // tiles_per_seq, s % tiles_per_seq, 0))
    return pl.pallas_call(
        functools.partial(_ffn_kernel, tiles_per_seq, True),
        grid=(bsz * tiles_per_seq,),
        in_specs=[tok] + specs + [_layer_spec(final_g, 0)],
        out_specs=tok,
        out_shape=jax.ShapeDtypeStruct(xl.shape, F32),
        compiler_params=pltpu.CompilerParams(
            dimension_semantics=("arbitrary",), vmem_limit_bytes=VMEM_LIMIT_BYTES),
        name="ffn_final",
    )(xl, *arrays, final_g)


def kernel(x, c, ctx, c_ctx, w_mod, b_mod, norm1_g, norm2_g, w_in, b_in, conv_w, w_out_a, lam_re, lam_im,
           log_dt, b_re, b_im, c_re, c_im, d_skip, w_glu, b_glu, w_o, w_ff_in, w_ff_out, final_g):
    bsz, seq, d = x.shape
    depth = w_mod.shape[0]
    ctx_len = ctx.shape[1]
    d_conv = conv_w.shape[-1]
    d_ssm = d_skip.shape[-1]
    u_lo, u_hi = 3 * d_conv, 3 * d_conv + d_ssm
    assert bsz * ctx_len == TOK_TILE and TOK_TILE % GRID_W == 0 and bsz <= 2
    assert ctx_len == SSM_CHUNK * SUBLANES and d_ssm % LANES == 0
    assert seq % (SUBLANES * TOK_TILE) == 0
    seg_steps = seq // (SSM_CHUNK * SUBLANES)

    cvec = jnp.zeros((8, d), F32).at[:bsz].set(c.astype(F32)).at[2].set(c_ctx.astype(F32))
    mod = _mod_all(cvec, w_mod, b_mod)

    mt_op, wt_op, vt_op, a_op = _ssm_operators(lam_re, lam_im, log_dt, b_re, b_im, c_re, c_im, d_skip,
                                               seg_steps)

    w_in, w_out_a, w_glu, w_o = (w.astype(BF16) for w in (w_in, w_out_a, w_glu, w_o))
    w_ff_in, w_ff_out = w_ff_in.astype(BF16), w_ff_out.astype(BF16)
    row = lambda v: v.reshape(v.shape[0], 1, v.shape[-1])
    b_in, b_glu, g1, g2, fg = row(b_in), row(b_glu), row(norm1_g), row(norm2_g), final_g.reshape(1, 1, d)
    gates_lo = u_hi

    def operands(l, *items):
        arrays = [arr for arr, _ in items]
        return arrays, [_layer_spec(arr, l, cols) for arr, cols in items]

    u_cols = (u_lo, d_ssm)
    ssm_in = lambda l: operands(l, (mod, None), (g1, None), (w_in, u_cols), (b_in, u_cols)) + (d_ssm,)
    mixer_in = lambda l: operands(
        l, (mod, None), (g1, None), (w_in, (0, u_lo)), (b_in, (0, u_lo)),
        (w_in, (gates_lo, 2 * d)), (b_in, (gates_lo, 2 * d)), (conv_w, None), (w_out_a, None),
        (w_glu, None), (b_glu, None), (w_o, None))
    ffn_in = lambda l: operands(l, (mod, None), (g2, None), (w_ff_in, None), (w_ff_out, None))

    xl, xc = x, ctx.reshape(bsz * ctx_len, d)
    u_lat, u_ctx = _uproj(xl, xc, ssm_in(0))
    for l in range(depth):
        ys_lat, ys_ctx = _ssm(u_lat, u_ctx, mt_op, wt_op, vt_op, a_op, l)
        xl, xc = _mixer(xl, xc, ys_lat, ys_ctx, mixer_in(l), ctx_len)
        if l + 1 < depth:
            xl, xc, u_lat, u_ctx = _ffn(xl, xc, ffn_in(l), ssm_in(l + 1))
    return _ffn_final(xl, ffn_in(depth - 1), fg)
```

```python
import functools

import numpy as np
import jax
import jax.numpy as jnp
from jax import lax
from jax.experimental import pallas as pl
from jax.experimental.pallas import tpu as pltpu

GRID_W = 64
CONV_K = 3
SSM_GROUP = 16
STATE = 64
N_DIR = 2
RMS_EPS = 1e-6

TOK_TILE = 512
SSM_CHUNK = 32
CHUNK_PITCH = 40
SUBLANES = 8
LANES = 128
SSM_LANE_GROUPS = LANES // SSM_GROUP
SSM_SCAN_GROUPS = 4
SSM_IO_PARTS = 4
MIXER_ROW_BLOCKS = 2
FFN_ROW_BLOCKS = 2
VMEM_LIMIT_BYTES = 56 * 1024 * 1024

F32 = jnp.float32
BF16 = jnp.bfloat16


def _layer_spec(arr, l, cols=None):
    _, rows, n = arr.shape
    start, width = (0, n) if cols is None else cols
    assert start % width == 0
    return pl.BlockSpec((None, rows, width), lambda *_: (l, 0, start // width), pipeline_mode=pl.Buffered(1))


def _tile_coords(s, tiles_per_seq):
    k = jnp.maximum(s - 1, 0)
    return k // tiles_per_seq, k % tiles_per_seq


def _input_token_specs(d, tiles_per_seq):
    return [pl.BlockSpec((None, TOK_TILE, d), lambda s: (*_tile_coords(s, tiles_per_seq), 0)),
            pl.BlockSpec((TOK_TILE, d), lambda s: (0, 0))]


def _stream_spec(d, first_step=0):
    return pl.BlockSpec((None, TOK_TILE, d), lambda s: (s + first_step, 0, 0))


def _stream_shape(bsz, seq, d):
    return jax.ShapeDtypeStruct((1 + bsz * (seq // TOK_TILE), TOK_TILE, d), F32)


def _chunk_specs(n, tiles_per_seq):
    cpt = TOK_TILE // SSM_CHUNK
    tiles_per_seg = tiles_per_seq // SUBLANES

    def index(s):
        b, i = _tile_coords(s, tiles_per_seq)
        return (b, i % tiles_per_seg, i // tiles_per_seg, 0, 0)

    return [pl.BlockSpec((None, cpt, None, CHUNK_PITCH, n), index),
            pl.BlockSpec((TOK_TILE, n), lambda s: (0, 0))]


def _tile_kind(tiles_per_seq, first_step=0):
    s = pl.program_id(0) + first_step
    is_ctx = s == 0
    return is_ctx, jnp.where(is_ctx, 2, _tile_coords(s, tiles_per_seq)[0])


def _norm_mod(x, g, shift, scale):
    ms = jnp.mean(x * x, axis=-1, keepdims=True)
    y = x * lax.rsqrt(ms + RMS_EPS) * g
    return y * (1.0 + scale) + shift


def _mod_vec(mod_ref, row, k, d):
    return mod_ref[pl.ds(row, 1), k * d:(k + 1) * d]


def _mod_kernel(c_ref, w_ref, b_ref, o_ref):
    c = c_ref[...]
    s = c * jax.nn.sigmoid(c)
    o_ref[...] = jnp.dot(s.astype(BF16), w_ref[...].astype(BF16),
                         preferred_element_type=F32) + b_ref[...]


def _mod_all(cvec, w_mod, b_mod):
    depth, d, n = w_mod.shape
    tn = 1536
    return pl.pallas_call(
        _mod_kernel,
        grid=(depth, n // tn),
        in_specs=[pl.BlockSpec((8, d), lambda l, j: (0, 0)),
                  pl.BlockSpec((None, d, tn), lambda l, j: (l, 0, j)),
                  pl.BlockSpec((None, 1, tn), lambda l, j: (l, 0, j))],
        out_specs=pl.BlockSpec((None, 8, tn), lambda l, j: (l, 0, j)),
        out_shape=jax.ShapeDtypeStruct((depth, 8, n), F32),
        compiler_params=pltpu.CompilerParams(
            dimension_semantics=("arbitrary", "arbitrary"), vmem_limit_bytes=VMEM_LIMIT_BYTES),
        name="adaln_mod",
    )(cvec, w_mod, b_mod.reshape(depth, 1, n))


def _emit_ssm_inputs(x, row, is_ctx, mod_ref, g_ref, w_ref, b_ref, ul_ref, uc_ref):
    d = x.shape[-1]
    h = _norm_mod(x, g_ref[...], _mod_vec(mod_ref, row, 0, d), _mod_vec(mod_ref, row, 1, d))
    u = jnp.dot(h.astype(BF16), w_ref[...], preferred_element_type=F32) + b_ref[...]
    cpt, pitch, n = ul_ref.shape
    ul_ref[:, :SSM_CHUNK, :] = u.reshape(cpt, SSM_CHUNK, n)
    ul_ref[:, SSM_CHUNK:, :] = jnp.zeros((cpt, pitch - SSM_CHUNK, n), F32)

    @pl.when(is_ctx)
    def _():
        uc_ref[...] = u


def _uproj_kernel(tiles_per_seq, xl_ref, xc_ref, mod_ref, g_ref, w_ref, b_ref, ul_ref, uc_ref):
    is_ctx, row = _tile_kind(tiles_per_seq)
    x = jnp.where(is_ctx, xc_ref[...], xl_ref[...])
    _emit_ssm_inputs(x, row, is_ctx, mod_ref, g_ref, w_ref, b_ref, ul_ref, uc_ref)


def _ssm_input_shapes(bsz, seq, n):
    steps = seq // (SSM_CHUNK * SUBLANES)
    return [jax.ShapeDtypeStruct((bsz, steps, SUBLANES, CHUNK_PITCH, n), F32),
            jax.ShapeDtypeStruct((TOK_TILE, n), F32)]


def _uproj(xl, xc, ssm_in):
    bsz, seq, d = xl.shape
    tiles_per_seq = seq // TOK_TILE
    arrays, specs, n_u = ssm_in
    return pl.pallas_call(
        functools.partial(_uproj_kernel, tiles_per_seq),
        grid=(1 + bsz * tiles_per_seq,),
        in_specs=_input_token_specs(d, tiles_per_seq) + specs,
        out_specs=_chunk_specs(n_u, tiles_per_seq),
        out_shape=_ssm_input_shapes(bsz, seq, n_u),
        compiler_params=pltpu.CompilerParams(
            dimension_semantics=("arbitrary",), vmem_limit_bytes=VMEM_LIMIT_BYTES),
        name="ssm_uproj",
    )(xl, xc, *arrays)


def _cmul(a_re, a_im, b_re, b_im):
    return a_re * b_re - a_im * b_im, a_re * b_im + a_im * b_re


def _ssm_kernel(ctx_chunks, ul_ref, uc_ref, mt_ref, wt_ref, vt_ref, a_ref, yl_ref, yc_ref,
                ut_scr, yt_scr, x_scr, hf_scr, hb_scr):
    L = SSM_CHUNK
    H = SSM_GROUP
    parts = SSM_IO_PARTS
    n_part = ul_ref.shape[0] // CHUNK_PITCH
    n_lat = n_part * parts
    n_all = ut_scr.shape[-1]
    phase = pl.program_id(2)
    trow = lambda t: pl.ds(pl.multiple_of(t * H, H), H)

    def fill(cols, rows_of_t):
        def body(t, carry):
            at = rows_of_t(t).T
            for g in range(SSM_LANE_GROUPS):
                ut_scr[g, trow(t), cols] = at[g * H:(g + 1) * H, :].astype(BF16)
            return carry
        lax.fori_loop(0, L, body, 0, unroll=16)

    for q in range(parts):
        @pl.when(phase == q)
        def _():
            fill(slice(q * n_part, (q + 1) * n_part), lambda t: ul_ref[pl.ds(t, n_part, stride=CHUNK_PITCH), :])

    @pl.when(phase == parts)
    def _():
        pad = jnp.zeros((n_all - n_lat - ctx_chunks, LANES), F32)
        fill(slice(n_lat, n_all),
             lambda t: jnp.concatenate([uc_ref[pl.ds(t, ctx_chunks, stride=L), :], pad], axis=0))
        _ssm_compute(ctx_chunks, n_lat, mt_ref, wt_ref, vt_ref, a_ref, ut_scr, yt_scr, x_scr, hf_scr, hb_scr)

        def drain_ctx(t, carry):
            y = yt_scr[t, :, n_lat:].astype(F32).T
            yc_ref[pl.ds(t, ctx_chunks, stride=L), :] = y[:ctx_chunks]
            return carry
        lax.fori_loop(0, L, drain_ctx, 0, unroll=16)

    for q in range(parts):
        @pl.when(phase == parts + 1 + q)
        def _():
            def drain(t, carry):
                y = yt_scr[t, :, q * n_part:(q + 1) * n_part].astype(F32).T
                yl_ref[pl.ds(t, n_part, stride=CHUNK_PITCH), :] = y
                return carry
            lax.fori_loop(0, L, drain, 0, unroll=16)
            for t in range(L, CHUNK_PITCH):
                yl_ref[pl.ds(t, n_part, stride=CHUNK_PITCH), :] = jnp.zeros((n_part, LANES), F32)


def _ssm_compute(ctx_chunks, n_lat, mt_ref, wt_ref, vt_ref, a_ref, ut_scr, yt_scr, x_scr, hf_scr, hb_scr):
    L = SSM_CHUNK
    H = SSM_GROUP
    sub = SUBLANES
    n_all = ut_scr.shape[-1]
    steps = n_lat // sub
    half = STATE * N_DIR
    gl = SSM_LANE_GROUPS
    gs = SSM_SCAN_GROUPS

    is_fwd = lax.broadcasted_iota(jnp.int32, (sub, half), 1) < STATE
    is_fwd1 = is_fwd[:1]
    sl = lax.broadcasted_iota(jnp.int32, (sub, half), 0)
    blk = lambda j: pl.ds(pl.multiple_of(j * sub, sub), sub)
    zero = jnp.zeros((sub, half), F32)
    lane2 = lax.broadcasted_iota(jnp.int32, (n_all, 2 * half), 1)
    take_fwd = (lane2 % half) < STATE
    tail0 = n_lat + ctx_chunks
    assert tail0 % sub == 0 and ctx_chunks == sub

    for g0 in range(0, gl, gs):
        coef = lambda g, r, rows=sub: jnp.broadcast_to(a_ref[g0 + g, r:r + 1, :], (rows, half))
        a_re = [coef(g, 0) for g in range(gs)]
        a_im = [coef(g, 1) for g in range(gs)]

        for g in range(gs):
            xt = jnp.dot(wt_ref[g0 + g], ut_scr[g0 + g], preferred_element_type=F32)
            x_scr[g] = xt.T
            hf_scr[g, tail0:, :] = jnp.zeros((n_all - tail0, 2 * half), F32)
            hb_scr[g, tail0:, :] = jnp.zeros((n_all - tail0, 2 * half), F32)

        h0 = []
        for g in range(gs):
            xc = x_scr[g, n_lat:n_lat + ctx_chunks, :]
            ar, ai = a_re[g][:1], a_im[g][:1]
            h_re = h_im = jnp.zeros((1, half), F32)
            hf_rows, hb_rows = [], [None] * ctx_chunks
            for k in range(ctx_chunks):
                kb = ctx_chunks - 1 - k
                hf_rows.append((h_re, h_im))
                hb_rows[kb] = (h_re, h_im)
                x_re = jnp.where(is_fwd1, xc[k:k + 1, :half], xc[kb:kb + 1, :half])
                x_im = jnp.where(is_fwd1, xc[k:k + 1, half:], xc[kb:kb + 1, half:])
                p_re, p_im = _cmul(ar, ai, h_re, h_im)
                h_re, h_im = p_re + x_re, p_im + x_im
            cat = lambda rows, part: jnp.concatenate([r[part] for r in rows], axis=0)
            hf_scr[g, n_lat:n_lat + ctx_chunks, :half] = cat(hf_rows, 0)
            hf_scr[g, n_lat:n_lat + ctx_chunks, half:] = cat(hf_rows, 1)
            hb_scr[g, n_lat:n_lat + ctx_chunks, :half] = cat(hb_rows, 0)
            hb_scr[g, n_lat:n_lat + ctx_chunks, half:] = cat(hb_rows, 1)
            h0.append((jnp.broadcast_to(h_re, (sub, half)), jnp.broadcast_to(h_im, (sub, half))))

        def scan_step(j, carry):
            jb = steps - 1 - j
            new = []
            for g in range(gs):
                h_re, h_im = carry[g]
                x_re = jnp.where(is_fwd, x_scr[g, blk(j), :half], x_scr[g, blk(jb), :half])
                x_im = jnp.where(is_fwd, x_scr[g, blk(j), half:], x_scr[g, blk(jb), half:])
                hf_scr[g, blk(j), :half] = h_re
                hf_scr[g, blk(j), half:] = h_im
                hb_scr[g, blk(jb), :half] = h_re
                hb_scr[g, blk(jb), half:] = h_im
                p_re, p_im = _cmul(a_re[g], a_im[g], h_re, h_im)
                new.append((p_re + x_re, p_im + x_im))
            return tuple(new)

        ends = lax.fori_loop(0, steps, scan_step, tuple((zero, zero) for _ in range(gs)), unroll=2)

        carries = []
        for g in range(gs):
            def shift(t, h):
                return jnp.where(is_fwd, jnp.where(sl < 1, h, pltpu.roll(t, 1, 0)),
                                 jnp.where(sl >= sub - 1, h, pltpu.roll(t, sub - 1, 0)))
            e_re, e_im = ends[g]
            s_re, s_im = coef(g, 2), coef(g, 3)
            c_re, c_im = shift(zero, h0[g][0]), shift(zero, h0[g][1])
            for _ in range(sub - 1):
                p_re, p_im = _cmul(s_re, s_im, c_re, c_im)
                c_re, c_im = shift(e_re + p_re, h0[g][0]), shift(e_im + p_im, h0[g][1])
            carries.append((c_re, c_im))

        def fix_step(j, carry):
            jb = steps - 1 - j
            new = []
            for g in range(gs):
                d_re, d_im = carry[g]
                hf_scr[g, blk(j), :half] += d_re
                hf_scr[g, blk(j), half:] += d_im
                hb_scr[g, blk(jb), :half] += d_re
                hb_scr[g, blk(jb), half:] += d_im
                new.append(_cmul(a_re[g], a_im[g], d_re, d_im))
            return tuple(new)

        lax.fori_loop(0, steps, fix_step, tuple(carries), unroll=2)

        for g in range(gs):
            h_in = jnp.where(take_fwd, hf_scr[g], hb_scr[g]).T.astype(BF16)
            yt = (jnp.dot(mt_ref[g0 + g], ut_scr[g0 + g], preferred_element_type=F32)
                  + jnp.dot(vt_ref[g0 + g], h_in, preferred_element_type=F32))
            for t in range(L):
                yt_scr[t, (g0 + g) * H:(g0 + g + 1) * H, :] = yt[t * H:(t + 1) * H, :].astype(BF16)


def _ssm(u_lat, u_ctx, mt, wt, vt, a, l):
    bsz, steps, nseg, pitch, n = u_lat.shape
    L = SSM_CHUNK
    ctx_len = u_ctx.shape[0] // bsz
    ctx_chunks = ctx_len // L
    n_lat = steps * nseg
    n_all = -(-(n_lat + ctx_chunks) // LANES) * LANES
    gl = SSM_LANE_GROUPS
    f = L * SSM_GROUP
    ns = wt.shape[-2]
    parts = SSM_IO_PARTS
    n_blocks = n // LANES
    part_rows = n_lat * pitch // parts
    lat_in = pl.BlockSpec((None, part_rows, LANES), lambda b, j, p: (b, jnp.minimum(p, parts - 1), j))
    lat_out = pl.BlockSpec((None, part_rows, LANES), lambda b, j, p: (b, jnp.maximum(p - parts - 1, 0), j))
    ctx_spec = pl.BlockSpec((None, ctx_len, LANES), lambda b, j, p: (b, 0, j))
    grp = lambda shape: pl.BlockSpec(
        (None, gl) + shape, lambda b, j, p: (l, jnp.where(p > parts, (j + 1) % n_blocks, j), 0, 0))
    y_lat, y_ctx = pl.pallas_call(
        functools.partial(_ssm_kernel, ctx_chunks),
        grid=(bsz, n_blocks, 2 * parts + 1),
        in_specs=[lat_in, ctx_spec, grp((f, f)), grp((ns, f)), grp((f, ns)), grp((4, ns // 2))],
        out_specs=[lat_out, ctx_spec],
        out_shape=[jax.ShapeDtypeStruct((bsz, n_lat * pitch, n), F32),
                   jax.ShapeDtypeStruct((bsz, ctx_len, n), F32)],
        scratch_shapes=[pltpu.VMEM((gl, f, n_all), BF16), pltpu.VMEM((L, LANES, n_all), BF16)]
        + [pltpu.VMEM((SSM_SCAN_GROUPS, n_all, ns), F32)] * 3,
        compiler_params=pltpu.CompilerParams(
            dimension_semantics=("arbitrary", "arbitrary", "arbitrary"), vmem_limit_bytes=VMEM_LIMIT_BYTES),
        name="ssm_chunked",
    )(u_lat.reshape(bsz, n_lat * pitch, n), u_ctx.reshape(bsz, ctx_len, n), mt, wt, vt, a)
    return y_lat.reshape(u_lat.shape), y_ctx.reshape(u_ctx.shape)


def _cpow(z_re, z_im, n):
    out = None
    while n:
        if n & 1:
            out = (z_re, z_im) if out is None else _cmul(out[0], out[1], z_re, z_im)
        n >>= 1
        if n:
            z_re, z_im = _cmul(z_re, z_im, z_re, z_im)
    return out


def _ops_kernel(seg_steps, pw_ref, pv_ref, pa_ref, bbt_ref, c_ref, csel_ref, dcol_ref,
                mt_ref, wt_ref, vt_ref, a_ref, w_scr):
    gb = mt_ref.shape[0]
    f = mt_ref.shape[-1]
    H = SSM_GROUP
    L = f // H
    half = STATE * N_DIR
    lane = lax.broadcasted_iota(jnp.int32, (H, 2 * f), 1)
    row = lax.broadcasted_iota(jnp.int32, (H, 2 * f), 0)
    centre = lane == (L - 1) * H + row
    zeros = jnp.zeros((H, f), F32)
    for g in range(gb):
        bt_re, bt_im = bbt_ref[g, 0], bbt_ref[g, 1]
        c_re, c_im = c_ref[g, 0], c_ref[g, 1]
        for t in range(L):
            rows = slice(t * H, (t + 1) * H)
            w_re, w_im = _cmul(pw_ref[g, 0, t:t + 1, :], pw_ref[g, 1, t:t + 1, :], bt_re, bt_im)
            w_scr[rows, :half] = w_re
            w_scr[rows, half:] = w_im
            g_re, g_im = _cmul(pv_ref[g, 0, t:t + 1, :], pv_ref[g, 1, t:t + 1, :], c_re, c_im)
            vt_ref[g, rows, :half] = g_re.astype(vt_ref.dtype)
            vt_ref[g, rows, half:] = (-g_im).astype(vt_ref.dtype)
        wt = w_scr[...].T
        wt_ref[g] = wt.astype(wt_ref.dtype)
        kk = jnp.dot(csel_ref[g], wt, preferred_element_type=F32, precision=lax.Precision.HIGHEST)
        kf = jnp.concatenate([kk[:H], zeros], axis=1)
        kb = pltpu.roll(jnp.concatenate([kk[H:], zeros], axis=1), (L - 1) * H, 1)
        dmat = jnp.concatenate([dcol_ref[g]] * (2 * f // LANES), axis=1)
        k = kf + kb + jnp.where(centre, dmat, 0.0)
        for t in range(L):
            off = f - (t + 1) * H
            win = k if off == 0 else pltpu.roll(k, 2 * f - off, 1)
            mt_ref[g, t * H:(t + 1) * H, :] = win[:, :f].astype(mt_ref.dtype)
        ac_re, ac_im = pa_ref[g, 0:1, :], pa_ref[g, 1:2, :]
        as_re, as_im = _cpow(ac_re, ac_im, seg_steps)
        a_ref[g, 0:1, :] = ac_re
        a_ref[g, 1:2, :] = ac_im
        a_ref[g, 2:3, :] = as_re
        a_ref[g, 3:4, :] = as_im


def _ssm_operators(lam_re, lam_im, log_dt, b_re, b_im, c_re, c_im, d_skip, seg_steps):
    L = SSM_CHUNK
    H = SSM_GROUP
    lam_re = lam_re.astype(F32)
    lam_im = lam_im.astype(F32)
    depth, _, ng, _ = lam_re.shape
    dt = jnp.exp(log_dt.astype(F32))[..., None]
    mag = jnp.exp(lam_re * dt)
    a_re = mag * jnp.cos(lam_im * dt)
    a_im = mag * jnp.sin(lam_im * dt)
    nr, ni = a_re - 1.0, a_im
    den = lam_re * lam_re + lam_im * lam_im
    f_re = (nr * lam_re + ni * lam_im) / den
    f_im = (ni * lam_re - nr * lam_im) / den
    br, bi = b_re.astype(F32), b_im.astype(F32)
    bb_re = f_re[..., None] * br - f_im[..., None] * bi
    bb_im = f_re[..., None] * bi + f_im[..., None] * br
    k = jnp.arange(L + 1, dtype=F32)[:, None]
    pmag = jnp.exp(lam_re[..., None, :] * dt[..., None, :] * k)
    parg = lam_im[..., None, :] * dt[..., None, :] * k
    p_re = pmag * jnp.cos(parg)
    p_im = pmag * jnp.sin(parg)
    both = lambda fwd, bwd: jnp.concatenate([fwd, bwd], axis=-1)
    reim = lambda re, im: jnp.stack([re, im], axis=2)
    pw = reim(both(p_re[:, 0, :, L - 1::-1][:, :, :L], p_re[:, 1, :, :L]),
              both(p_im[:, 0, :, L - 1::-1][:, :, :L], p_im[:, 1, :, :L]))
    pv = reim(both(p_re[:, 0, :, 1:], p_re[:, 1, :, :0:-1]), both(p_im[:, 0, :, 1:], p_im[:, 1, :, :0:-1]))
    pa = jnp.stack([both(p_re[:, 0, :, L], p_re[:, 1, :, L]), both(p_im[:, 0, :, L], p_im[:, 1, :, L])], axis=2)
    sw = lambda z: jnp.swapaxes(z, -1, -2)
    bbt = reim(both(sw(bb_re[:, 0]), sw(bb_re[:, 1])), both(sw(bb_im[:, 0]), sw(bb_im[:, 1])))
    cr, ci = c_re.astype(F32), c_im.astype(F32)
    cc = reim(both(cr[:, 0], cr[:, 1]), both(ci[:, 0], ci[:, 1]))
    z = jnp.zeros_like(cr[:, 0])
    csel = jnp.concatenate([jnp.concatenate([cr[:, 0], z, -ci[:, 0], z], axis=-1),
                            jnp.concatenate([z, cr[:, 1], z, -ci[:, 1]], axis=-1)], axis=-2)
    dcol = jnp.broadcast_to(d_skip.astype(F32).reshape(depth, ng, H, 1), (depth, ng, H, LANES))
    f = L * H
    ns = 2 * N_DIR * STATE
    gb = 8
    blk = lambda *shape: pl.BlockSpec((None, gb) + shape, lambda l, j: (l, j) + (0,) * len(shape))
    return pl.pallas_call(
        functools.partial(_ops_kernel, seg_steps),
        grid=(depth, ng // gb),
        in_specs=[blk(2, L, ns // 2), blk(2, L, ns // 2), blk(2, ns // 2), blk(2, H, ns // 2),
                  blk(2, H, ns // 2), blk(2 * H, ns), blk(H, LANES)],
        out_specs=[blk(f, f), blk(ns, f), blk(f, ns), blk(4, ns // 2)],
        out_shape=[jax.ShapeDtypeStruct((depth, ng, f, f), BF16), jax.ShapeDtypeStruct((depth, ng, ns, f), BF16),
                   jax.ShapeDtypeStruct((depth, ng, f, ns), BF16), jax.ShapeDtypeStruct((depth, ng, 4, ns // 2), F32)],
        scratch_shapes=[pltpu.VMEM((f, ns), F32)],
        compiler_params=pltpu.CompilerParams(
            dimension_semantics=("parallel", "parallel"), vmem_limit_bytes=VMEM_LIMIT_BYTES),
        name="ssm_operators",
    )(pw, pv, pa, bbt, cc, csel, dcol)


def _gelu_tanh(x):
    return 0.5 * x * (1.0 + jnp.tanh(np.sqrt(2.0 / np.pi).astype(np.float32)
                                     * (x + np.float32(0.044715) * (x * x * x))))


def _mixer_kernel(tiles_per_seq, ctx_len, first, x_ref, *refs):
    if first:
        xc_ref, *refs = refs
    (ysl_ref, ysc_ref, mod_ref, g_ref, wa_ref, ba_ref, wg_ref, bg_ref, cw_ref,
     woa_ref, wglu_ref, bglu_ref, wo_ref, o_ref) = refs
    tm, d = x_ref.shape
    dc = woa_ref.shape[0]
    n = ysc_ref.shape[-1]
    is_ctx, row = _tile_kind(tiles_per_seq)
    shift, scale, gate = (_mod_vec(mod_ref, row, k, d) for k in range(3))
    nb = MIXER_ROW_BLOCKS
    rows = tm // nb
    assert rows % ctx_len == 0 and rows % SSM_CHUNK == 0
    t = lax.broadcasted_iota(jnp.int32, (rows, 1), 0)
    seg = jnp.where(is_ctx, ctx_len - 1, GRID_W - 1)
    pos = t & seg
    blocks = [slice(k * rows, (k + 1) * rows) for k in range(nb)]
    xs, za, zg = [], [], []
    for k, r in enumerate(blocks):
        x = jnp.where(is_ctx, xc_ref[r, :], x_ref[r, :]) if first else x_ref[r, :]
        h = _norm_mod(x, g_ref[...], shift, scale).astype(BF16)
        xs.append(x)
        za.append(jnp.dot(h, wa_ref[...], preferred_element_type=F32) + ba_ref[...])
        zg.append(jnp.dot(h, wg_ref[...], preferred_element_type=F32) + bg_ref[...])
    y_a, y_b = [], []
    for k, r in enumerate(blocks):
        g_b, g_c, x_in = za[k][:, :dc], za[k][:, dc:2 * dc], za[k][:, 2 * dc:]
        v = g_c * x_in
        v_prev = jnp.where(pos == 0, 0.0, pltpu.roll(v, 1, 0))
        v_next = jnp.where(pos == seg, 0.0, pltpu.roll(v, rows - 1, 0))
        cv = cw_ref[0:1, :] * v_prev + cw_ref[1:2, :] * v + cw_ref[2:3, :] * v_next
        y_a.append(jnp.dot((g_b * cv).astype(BF16), woa_ref[...], preferred_element_type=F32))
        ys = ysl_ref[k * rows // SSM_CHUNK:(k + 1) * rows // SSM_CHUNK, :SSM_CHUNK, :].reshape(rows, n)
        s = _gelu_tanh(jnp.where(is_ctx, ysc_ref[r, :], ys))
        gl = jnp.dot(s.astype(BF16), wglu_ref[...], preferred_element_type=F32) + bglu_ref[...]
        y_b.append(gl[:, :d] * jax.nn.sigmoid(gl[:, d:]))
    for k, r in enumerate(blocks):
        gate_a, gate_b = zg[k][:, :d], zg[k][:, d:]
        merged = jax.nn.sigmoid(gate_a) * y_a[k] + jax.nn.sigmoid(gate_b) * y_b[k]
        out = jnp.dot(merged.astype(BF16), wo_ref[...], preferred_element_type=F32)
        o_ref[r, :] = xs[k] + gate * out


def _mixer(tokens, ys_lat, ys_ctx, params, bsz, seq, ctx_len):
    first = len(tokens) == 2
    d = tokens[0].shape[-1]
    tiles_per_seq = seq // TOK_TILE
    arrays, specs = params
    return pl.pallas_call(
        functools.partial(_mixer_kernel, tiles_per_seq, ctx_len, first),
        grid=(1 + bsz * tiles_per_seq,),
        in_specs=(_input_token_specs(d, tiles_per_seq) if first else [_stream_spec(d)])
        + _chunk_specs(ys_ctx.shape[-1], tiles_per_seq) + specs,
        out_specs=_stream_spec(d),
        out_shape=_stream_shape(bsz, seq, d),
        compiler_params=pltpu.CompilerParams(
            dimension_semantics=("arbitrary",), vmem_limit_bytes=VMEM_LIMIT_BYTES),
        name="mixer",
    )(*tokens, ys_lat, ys_ctx, *arrays)


def _ffn_kernel(tiles_per_seq, final, x_ref, *refs):
    if final:
        mod_ref, g_ref, win_ref, wout_ref, fg_ref, o_ref = refs
    else:
        mod_ref, g_ref, win_ref, wout_ref, modn_ref, gn_ref, wu_ref, bu_ref, o_ref, ul_ref, uc_ref = refs
    tm, d = x_ref.shape
    dff = wout_ref.shape[0]
    is_ctx, row = _tile_kind(tiles_per_seq, first_step=1 if final else 0)
    shift, scale, gate_mod = (_mod_vec(mod_ref, row, k, d) for k in (3, 4, 5))
    rows = tm // FFN_ROW_BLOCKS
    blocks = [slice(k * rows, (k + 1) * rows) for k in range(FFN_ROW_BLOCKS)]
    xs, zs = [], []
    for r in blocks:
        x = x_ref[r, :]
        h = _norm_mod(x, g_ref[...], shift, scale)
        xs.append(x)
        zs.append(jnp.dot(h.astype(BF16), win_ref[...], preferred_element_type=F32))
    ys = []
    for x, z in zip(xs, zs):
        gate, up = z[:, :dff], z[:, dff:]
        act = gate * jax.nn.sigmoid(gate) * up
        out = jnp.dot(act.astype(BF16), wout_ref[...], preferred_element_type=F32)
        ys.append(x + gate_mod * out)
    if final:
        for r, y in zip(blocks, ys):
            ms = jnp.mean(y * y, axis=-1, keepdims=True)
            o_ref[r, :] = y * lax.rsqrt(ms + RMS_EPS) * fg_ref[...]
    else:
        for r, y in zip(blocks, ys):
            o_ref[r, :] = y
        _emit_ssm_inputs(jnp.concatenate(ys, axis=0), row, is_ctx, modn_ref, gn_ref, wu_ref, bu_ref, ul_ref, uc_ref)


def _ffn(xs, params, nxt, bsz, seq):
    d = xs.shape[-1]
    tiles_per_seq = seq // TOK_TILE
    arrays, specs = params
    return pl.pallas_call(
        functools.partial(_ffn_kernel, tiles_per_seq, False),
        grid=(1 + bsz * tiles_per_seq,),
        in_specs=[_stream_spec(d)] + specs + nxt[1],
        out_specs=[_stream_spec(d)] + _chunk_specs(nxt[2], tiles_per_seq),
        out_shape=[_stream_shape(bsz, seq, d)] + _ssm_input_shapes(bsz, seq, nxt[2]),
        compiler_params=pltpu.CompilerParams(
            dimension_semantics=("arbitrary",), vmem_limit_bytes=VMEM_LIMIT_BYTES),
        name="ffn",
    )(xs, *arrays, *nxt[0])


def _ffn_final(xs, params, final_g, bsz, seq):
    d = xs.shape[-1]
    tiles_per_seq = seq // TOK_TILE
    arrays, specs = params
    return pl.pallas_call(
        functools.partial(_ffn_kernel, tiles_per_seq, True),
        grid=(bsz * tiles_per_seq,),
        in_specs=[_stream_spec(d, first_step=1)] + specs + [_layer_spec(final_g, 0)],
        out_specs=pl.BlockSpec((None, TOK_TILE, d), lambda s: (s // tiles_per_seq, s % tiles_per_seq, 0)),
        out_shape=jax.ShapeDtypeStruct((bsz, seq, d), F32),
        compiler_params=pltpu.CompilerParams(
            dimension_semantics=("arbitrary",), vmem_limit_bytes=VMEM_LIMIT_BYTES),
        name="ffn_final",
    )(xs, *arrays, final_g)


def kernel(x, c, ctx, c_ctx, w_mod, b_mod, norm1_g, norm2_g, w_in, b_in, conv_w, w_out_a, lam_re, lam_im,
           log_dt, b_re, b_im, c_re, c_im, d_skip, w_glu, b_glu, w_o, w_ff_in, w_ff_out, final_g):
    bsz, seq, d = x.shape
    depth = w_mod.shape[0]
    ctx_len = ctx.shape[1]
    d_conv = conv_w.shape[-1]
    d_ssm = d_skip.shape[-1]
    u_lo, u_hi = 3 * d_conv, 3 * d_conv + d_ssm
    assert bsz * ctx_len == TOK_TILE and TOK_TILE % GRID_W == 0 and bsz <= 2
    assert ctx_len == SSM_CHUNK * SUBLANES and d_ssm % LANES == 0
    assert seq % (SUBLANES * TOK_TILE) == 0
    seg_steps = seq // (SSM_CHUNK * SUBLANES)

    cvec = jnp.zeros((8, d), F32).at[:bsz].set(c.astype(F32)).at[2].set(c_ctx.astype(F32))
    mod = _mod_all(cvec, w_mod, b_mod)

    mt_op, wt_op, vt_op, a_op = _ssm_operators(lam_re, lam_im, log_dt, b_re, b_im, c_re, c_im, d_skip,
                                               seg_steps)

    w_in, w_out_a, w_glu, w_o = (w.astype(BF16) for w in (w_in, w_out_a, w_glu, w_o))
    w_ff_in, w_ff_out = w_ff_in.astype(BF16), w_ff_out.astype(BF16)
    row = lambda v: v.reshape(v.shape[0], 1, v.shape[-1])
    b_in, b_glu, g1, g2, fg = row(b_in), row(b_glu), row(norm1_g), row(norm2_g), final_g.reshape(1, 1, d)
    gates_lo = u_hi

    def operands(l, *items):
        arrays = [arr for arr, _ in items]
        return arrays, [_layer_spec(arr, l, cols) for arr, cols in items]

    u_cols = (u_lo, d_ssm)
    ssm_in = lambda l: operands(l, (mod, None), (g1, None), (w_in, u_cols), (b_in, u_cols)) + (d_ssm,)
    mixer_in = lambda l: operands(
        l, (mod, None), (g1, None), (w_in, (0, u_lo)), (b_in, (0, u_lo)),
        (w_in, (gates_lo, 2 * d)), (b_in, (gates_lo, 2 * d)), (conv_w, None), (w_out_a, None),
        (w_glu, None), (b_glu, None), (w_o, None))
    ffn_in = lambda l: operands(l, (mod, None), (g2, None), (w_ff_in, None), (w_ff_out, None))

    tokens = (x, ctx.reshape(bsz * ctx_len, d))
    u_lat, u_ctx = _uproj(*tokens, ssm_in(0))
    for l in range(depth):
        ys_lat, ys_ctx = _ssm(u_lat, u_ctx, mt_op, wt_op, vt_op, a_op, l)
        xs = _mixer(tokens, ys_lat, ys_ctx, mixer_in(l), bsz, seq, ctx_len)
        if l + 1 < depth:
            xs, u_lat, u_ctx = _ffn(xs, ffn_in(l), ssm_in(l + 1), bsz, seq)
            tokens = (xs,)
    return _ffn_final(xs, ffn_in(depth - 1), fg, bsz, seq)
```

```python
import functools

import numpy as np
import jax
import jax.numpy as jnp
from jax import lax
from jax.experimental import pallas as pl
from jax.experimental.pallas import tpu as pltpu

GRID_W = 64
CONV_K = 3
SSM_GROUP = 16
STATE = 64
N_DIR = 2
RMS_EPS = 1e-6

TOK_TILE = 512
SSM_CHUNK = 32
PACKED_ROWS = SSM_CHUNK // 2
CHUNK_PITCH = 24
SUBLANES = 8
LANES = 128
SSM_LANE_GROUPS = LANES // SSM_GROUP
SSM_SCAN_GROUPS = 4
SSM_IO_PARTS = 4
MIXER_ROW_BLOCKS = 2
FFN_ROW_BLOCKS = 2
VMEM_LIMIT_BYTES = 56 * 1024 * 1024

F32 = jnp.float32
BF16 = jnp.bfloat16
U32 = jnp.uint32


def _pack_rows(lo, hi):
    bits = lambda v: lax.bitcast_convert_type(v.astype(BF16).astype(F32), U32)
    return bits(hi) | (bits(lo) >> 16)


def _unpack_rows(w):
    return (lax.bitcast_convert_type(w << 16, F32),
            lax.bitcast_convert_type(w & jnp.uint32(0xFFFF0000), F32))


def _layer_spec(arr, l, cols=None):
    _, rows, n = arr.shape
    start, width = (0, n) if cols is None else cols
    assert start % width == 0
    return pl.BlockSpec((None, rows, width), lambda *_: (l, 0, start // width), pipeline_mode=pl.Buffered(1))


def _tile_coords(s, tiles_per_seq):
    k = jnp.maximum(s - 1, 0)
    return k // tiles_per_seq, k % tiles_per_seq


def _input_token_specs(d, tiles_per_seq):
    return [pl.BlockSpec((None, TOK_TILE, d), lambda s: (*_tile_coords(s, tiles_per_seq), 0)),
            pl.BlockSpec((TOK_TILE, d), lambda s: (0, 0))]


def _stream_spec(d, first_step=0):
    return pl.BlockSpec((None, TOK_TILE, d), lambda s: (s + first_step, 0, 0))


def _stream_shape(bsz, seq, d):
    return jax.ShapeDtypeStruct((1 + bsz * (seq // TOK_TILE), TOK_TILE, d), F32)


def _chunk_specs(n, tiles_per_seq):
    cpt = TOK_TILE // SSM_CHUNK
    tiles_per_seg = tiles_per_seq // SUBLANES

    def index(s):
        b, i = _tile_coords(s, tiles_per_seq)
        return (b, i % tiles_per_seg, i // tiles_per_seg, 0, 0)

    return [pl.BlockSpec((None, cpt, None, CHUNK_PITCH, n), index),
            pl.BlockSpec((TOK_TILE, n), lambda s: (0, 0))]


def _tile_kind(tiles_per_seq, first_step=0):
    s = pl.program_id(0) + first_step
    is_ctx = s == 0
    return is_ctx, jnp.where(is_ctx, 2, _tile_coords(s, tiles_per_seq)[0])


def _norm_mod(x, g, shift, scale):
    ms = jnp.mean(x * x, axis=-1, keepdims=True)
    y = x * lax.rsqrt(ms + RMS_EPS) * g
    return y * (1.0 + scale) + shift


def _mod_vec(mod_ref, row, k, d):
    return mod_ref[pl.ds(row, 1), k * d:(k + 1) * d]


def _mod_kernel(c_ref, w_ref, b_ref, o_ref):
    c = c_ref[...]
    s = c * jax.nn.sigmoid(c)
    o_ref[...] = jnp.dot(s.astype(BF16), w_ref[...].astype(BF16),
                         preferred_element_type=F32) + b_ref[...]


def _mod_all(cvec, w_mod, b_mod):
    depth, d, n = w_mod.shape
    tn = 1536
    return pl.pallas_call(
        _mod_kernel,
        grid=(depth, n // tn),
        in_specs=[pl.BlockSpec((8, d), lambda l, j: (0, 0)),
                  pl.BlockSpec((None, d, tn), lambda l, j: (l, 0, j)),
                  pl.BlockSpec((None, 1, tn), lambda l, j: (l, 0, j))],
        out_specs=pl.BlockSpec((None, 8, tn), lambda l, j: (l, 0, j)),
        out_shape=jax.ShapeDtypeStruct((depth, 8, n), F32),
        compiler_params=pltpu.CompilerParams(
            dimension_semantics=("arbitrary", "arbitrary"), vmem_limit_bytes=VMEM_LIMIT_BYTES),
        name="adaln_mod",
    )(cvec, w_mod, b_mod.reshape(depth, 1, n))


def _emit_ssm_inputs(x, row, is_ctx, mod_ref, g_ref, w_ref, b_ref, ul_ref, uc_ref):
    d = x.shape[-1]
    h = _norm_mod(x, g_ref[...], _mod_vec(mod_ref, row, 0, d), _mod_vec(mod_ref, row, 1, d))
    u = jnp.dot(h.astype(BF16), w_ref[...], preferred_element_type=F32) + b_ref[...]
    cpt, pitch, n = ul_ref.shape
    u3 = u.reshape(cpt, SSM_CHUNK, n)
    ul_ref[:, :PACKED_ROWS, :] = _pack_rows(u3[:, :PACKED_ROWS], u3[:, PACKED_ROWS:])
    ul_ref[:, PACKED_ROWS:, :] = jnp.zeros((cpt, pitch - PACKED_ROWS, n), U32)

    @pl.when(is_ctx)
    def _():
        uc_ref[...] = u


def _uproj_kernel(tiles_per_seq, xl_ref, xc_ref, mod_ref, g_ref, w_ref, b_ref, ul_ref, uc_ref):
    is_ctx, row = _tile_kind(tiles_per_seq)
    x = jnp.where(is_ctx, xc_ref[...], xl_ref[...])
    _emit_ssm_inputs(x, row, is_ctx, mod_ref, g_ref, w_ref, b_ref, ul_ref, uc_ref)


def _ssm_input_shapes(bsz, seq, n):
    steps = seq // (SSM_CHUNK * SUBLANES)
    return [jax.ShapeDtypeStruct((bsz, steps, SUBLANES, CHUNK_PITCH, n), U32),
            jax.ShapeDtypeStruct((TOK_TILE, n), F32)]


def _uproj(xl, xc, ssm_in):
    bsz, seq, d = xl.shape
    tiles_per_seq = seq // TOK_TILE
    arrays, specs, n_u = ssm_in
    return pl.pallas_call(
        functools.partial(_uproj_kernel, tiles_per_seq),
        grid=(1 + bsz * tiles_per_seq,),
        in_specs=_input_token_specs(d, tiles_per_seq) + specs,
        out_specs=_chunk_specs(n_u, tiles_per_seq),
        out_shape=_ssm_input_shapes(bsz, seq, n_u),
        compiler_params=pltpu.CompilerParams(
            dimension_semantics=("arbitrary",), vmem_limit_bytes=VMEM_LIMIT_BYTES),
        name="ssm_uproj",
    )(xl, xc, *arrays)


def _cmul(a_re, a_im, b_re, b_im):
    return a_re * b_re - a_im * b_im, a_re * b_im + a_im * b_re


def _ssm_kernel(ctx_chunks, ul_ref, uc_ref, mt_ref, wt_ref, vt_ref, a_ref, yl_ref, yc_ref,
                ut_scr, yt_scr, x_scr, hf_scr, hb_scr):
    L = SSM_CHUNK
    H = SSM_GROUP
    parts = SSM_IO_PARTS
    n_part = ul_ref.shape[0] // CHUNK_PITCH
    n_lat = n_part * parts
    n_all = ut_scr.shape[-1]
    phase = pl.program_id(2)
    trow = lambda t: pl.ds(pl.multiple_of(t * H, H), H)

    def put(t, cols, at):
        for g in range(SSM_LANE_GROUPS):
            ut_scr[g, trow(t), cols] = at[g * H:(g + 1) * H, :].astype(BF16)

    for q in range(parts):
        @pl.when(phase == q)
        def _():
            cols = slice(q * n_part, (q + 1) * n_part)

            def body(r, carry):
                lo, hi = _unpack_rows(ul_ref[pl.ds(r, n_part, stride=CHUNK_PITCH), :].T)
                put(r, cols, lo)
                put(r + PACKED_ROWS, cols, hi)
                return carry
            lax.fori_loop(0, PACKED_ROWS, body, 0, unroll=8)

    @pl.when(phase == parts)
    def _():
        pad = jnp.zeros((n_all - n_lat - ctx_chunks, LANES), F32)

        def body(t, carry):
            put(t, slice(n_lat, n_all), jnp.concatenate([uc_ref[pl.ds(t, ctx_chunks, stride=L), :], pad], axis=0).T)
            return carry
        lax.fori_loop(0, L, body, 0, unroll=16)
        _ssm_compute(ctx_chunks, n_lat, mt_ref, wt_ref, vt_ref, a_ref, ut_scr, yt_scr, x_scr, hf_scr, hb_scr)

        def drain_ctx(t, carry):
            y = yt_scr[t, :, n_lat:].astype(F32).T
            yc_ref[pl.ds(t, ctx_chunks, stride=L), :] = y[:ctx_chunks]
            return carry
        lax.fori_loop(0, L, drain_ctx, 0, unroll=16)

    for q in range(parts):
        @pl.when(phase == parts + 1 + q)
        def _():
            cols = slice(q * n_part, (q + 1) * n_part)

            def drain(r, carry):
                w = _pack_rows(yt_scr[r, :, cols].astype(F32), yt_scr[r + PACKED_ROWS, :, cols].astype(F32))
                yl_ref[pl.ds(r, n_part, stride=CHUNK_PITCH), :] = w.T
                return carry
            lax.fori_loop(0, PACKED_ROWS, drain, 0, unroll=8)
            for r in range(PACKED_ROWS, CHUNK_PITCH):
                yl_ref[pl.ds(r, n_part, stride=CHUNK_PITCH), :] = jnp.zeros((n_part, LANES), U32)


def _ssm_compute(ctx_chunks, n_lat, mt_ref, wt_ref, vt_ref, a_ref, ut_scr, yt_scr, x_scr, hf_scr, hb_scr):
    L = SSM_CHUNK
    H = SSM_GROUP
    sub = SUBLANES
    n_all = ut_scr.shape[-1]
    steps = n_lat // sub
    half = STATE * N_DIR
    gl = SSM_LANE_GROUPS
    gs = SSM_SCAN_GROUPS

    is_fwd = lax.broadcasted_iota(jnp.int32, (sub, half), 1) < STATE
    is_fwd1 = is_fwd[:1]
    sl = lax.broadcasted_iota(jnp.int32, (sub, half), 0)
    blk = lambda j: pl.ds(pl.multiple_of(j * sub, sub), sub)
    zero = jnp.zeros((sub, half), F32)
    lane2 = lax.broadcasted_iota(jnp.int32, (n_all, 2 * half), 1)
    take_fwd = (lane2 % half) < STATE
    tail0 = n_lat + ctx_chunks
    assert tail0 % sub == 0 and ctx_chunks == sub

    for g0 in range(0, gl, gs):
        coef = lambda g, r, rows=sub: jnp.broadcast_to(a_ref[g0 + g, r:r + 1, :], (rows, half))
        a_re = [coef(g, 0) for g in range(gs)]
        a_im = [coef(g, 1) for g in range(gs)]

        for g in range(gs):
            xt = jnp.dot(wt_ref[g0 + g], ut_scr[g0 + g], preferred_element_type=F32)
            x_scr[g] = xt.T
            hf_scr[g, tail0:, :] = jnp.zeros((n_all - tail0, 2 * half), F32)
            hb_scr[g, tail0:, :] = jnp.zeros((n_all - tail0, 2 * half), F32)

        h0 = []
        for g in range(gs):
            xc = x_scr[g, n_lat:n_lat + ctx_chunks, :]
            ar, ai = a_re[g][:1], a_im[g][:1]
            h_re = h_im = jnp.zeros((1, half), F32)
            hf_rows, hb_rows = [], [None] * ctx_chunks
            for k in range(ctx_chunks):
                kb = ctx_chunks - 1 - k
                hf_rows.append((h_re, h_im))
                hb_rows[kb] = (h_re, h_im)
                x_re = jnp.where(is_fwd1, xc[k:k + 1, :half], xc[kb:kb + 1, :half])
                x_im = jnp.where(is_fwd1, xc[k:k + 1, half:], xc[kb:kb + 1, half:])
                p_re, p_im = _cmul(ar, ai, h_re, h_im)
                h_re, h_im = p_re + x_re, p_im + x_im
            cat = lambda rows, part: jnp.concatenate([r[part] for r in rows], axis=0)
            hf_scr[g, n_lat:n_lat + ctx_chunks, :half] = cat(hf_rows, 0)
            hf_scr[g, n_lat:n_lat + ctx_chunks, half:] = cat(hf_rows, 1)
            hb_scr[g, n_lat:n_lat + ctx_chunks, :half] = cat(hb_rows, 0)
            hb_scr[g, n_lat:n_lat + ctx_chunks, half:] = cat(hb_rows, 1)
            h0.append((jnp.broadcast_to(h_re, (sub, half)), jnp.broadcast_to(h_im, (sub, half))))

        def scan_step(j, carry):
            jb = steps - 1 - j
            new = []
            for g in range(gs):
                h_re, h_im = carry[g]
                x_re = jnp.where(is_fwd, x_scr[g, blk(j), :half], x_scr[g, blk(jb), :half])
                x_im = jnp.where(is_fwd, x_scr[g, blk(j), half:], x_scr[g, blk(jb), half:])
                hf_scr[g, blk(j), :half] = h_re
                hf_scr[g, blk(j), half:] = h_im
                hb_scr[g, blk(jb), :half] = h_re
                hb_scr[g, blk(jb), half:] = h_im
                p_re, p_im = _cmul(a_re[g], a_im[g], h_re, h_im)
                new.append((p_re + x_re, p_im + x_im))
            return tuple(new)

        ends = lax.fori_loop(0, steps, scan_step, tuple((zero, zero) for _ in range(gs)), unroll=2)

        carries = []
        for g in range(gs):
            def shift(t, h):
                return jnp.where(is_fwd, jnp.where(sl < 1, h, pltpu.roll(t, 1, 0)),
                                 jnp.where(sl >= sub - 1, h, pltpu.roll(t, sub - 1, 0)))
            e_re, e_im = ends[g]
            s_re, s_im = coef(g, 2), coef(g, 3)
            c_re, c_im = shift(zero, h0[g][0]), shift(zero, h0[g][1])
            for _ in range(sub - 1):
                p_re, p_im = _cmul(s_re, s_im, c_re, c_im)
                c_re, c_im = shift(e_re + p_re, h0[g][0]), shift(e_im + p_im, h0[g][1])
            carries.append((c_re, c_im))

        def fix_step(j, carry):
            jb = steps - 1 - j
            new = []
            for g in range(gs):
                d_re, d_im = carry[g]
                hf_scr[g, blk(j), :half] += d_re
                hf_scr[g, blk(j), half:] += d_im
                hb_scr[g, blk(jb), :half] += d_re
                hb_scr[g, blk(jb), half:] += d_im
                new.append(_cmul(a_re[g], a_im[g], d_re, d_im))
            return tuple(new)

        lax.fori_loop(0, steps, fix_step, tuple(carries), unroll=2)

        for g in range(gs):
            h_in = jnp.where(take_fwd, hf_scr[g], hb_scr[g]).T.astype(BF16)
            yt = (jnp.dot(mt_ref[g0 + g], ut_scr[g0 + g], preferred_element_type=F32)
                  + jnp.dot(vt_ref[g0 + g], h_in, preferred_element_type=F32))
            for t in range(L):
                yt_scr[t, (g0 + g) * H:(g0 + g + 1) * H, :] = yt[t * H:(t + 1) * H, :].astype(BF16)


def _ssm(u_lat, u_ctx, mt, wt, vt, a, l):
    bsz, steps, nseg, pitch, n = u_lat.shape
    L = SSM_CHUNK
    ctx_len = u_ctx.shape[0] // bsz
    ctx_chunks = ctx_len // L
    n_lat = steps * nseg
    n_all = -(-(n_lat + ctx_chunks) // LANES) * LANES
    gl = SSM_LANE_GROUPS
    f = L * SSM_GROUP
    ns = wt.shape[-2]
    parts = SSM_IO_PARTS
    n_blocks = n // LANES
    part_rows = n_lat * pitch // parts
    lat_in = pl.BlockSpec((None, part_rows, LANES), lambda b, j, p: (b, jnp.minimum(p, parts - 1), j))
    lat_out = pl.BlockSpec((None, part_rows, LANES), lambda b, j, p: (b, jnp.maximum(p - parts - 1, 0), j))
    ctx_spec = pl.BlockSpec((None, ctx_len, LANES), lambda b, j, p: (b, 0, j))
    grp = lambda shape: pl.BlockSpec(
        (None, gl) + shape, lambda b, j, p: (l, jnp.where(p > parts, (j + 1) % n_blocks, j), 0, 0))
    y_lat, y_ctx = pl.pallas_call(
        functools.partial(_ssm_kernel, ctx_chunks),
        grid=(bsz, n_blocks, 2 * parts + 1),
        in_specs=[lat_in, ctx_spec, grp((f, f)), grp((ns, f)), grp((f, ns)), grp((4, ns // 2))],
        out_specs=[lat_out, ctx_spec],
        out_shape=[jax.ShapeDtypeStruct((bsz, n_lat * pitch, n), U32),
                   jax.ShapeDtypeStruct((bsz, ctx_len, n), F32)],
        scratch_shapes=[pltpu.VMEM((gl, f, n_all), BF16), pltpu.VMEM((L, LANES, n_all), BF16)]
        + [pltpu.VMEM((SSM_SCAN_GROUPS, n_all, ns), F32)] * 3,
        compiler_params=pltpu.CompilerParams(
            dimension_semantics=("arbitrary", "arbitrary", "arbitrary"), vmem_limit_bytes=VMEM_LIMIT_BYTES),
        name="ssm_chunked",
    )(u_lat.reshape(bsz, n_lat * pitch, n), u_ctx.reshape(bsz, ctx_len, n), mt, wt, vt, a)
    return y_lat.reshape(u_lat.shape), y_ctx.reshape(u_ctx.shape)


def _cpow(z_re, z_im, n):
    out = None
    while n:
        if n & 1:
            out = (z_re, z_im) if out is None else _cmul(out[0], out[1], z_re, z_im)
        n >>= 1
        if n:
            z_re, z_im = _cmul(z_re, z_im, z_re, z_im)
    return out


def _ops_kernel(seg_steps, pw_ref, pv_ref, pa_ref, bbt_ref, c_ref, csel_ref, dcol_ref,
                mt_ref, wt_ref, vt_ref, a_ref, w_scr):
    gb = mt_ref.shape[0]
    f = mt_ref.shape[-1]
    H = SSM_GROUP
    L = f // H
    half = STATE * N_DIR
    lane = lax.broadcasted_iota(jnp.int32, (H, 2 * f), 1)
    row = lax.broadcasted_iota(jnp.int32, (H, 2 * f), 0)
    centre = lane == (L - 1) * H + row
    zeros = jnp.zeros((H, f), F32)
    for g in range(gb):
        bt_re, bt_im = bbt_ref[g, 0], bbt_ref[g, 1]
        c_re, c_im = c_ref[g, 0], c_ref[g, 1]
        for t in range(L):
            rows = slice(t * H, (t + 1) * H)
            w_re, w_im = _cmul(pw_ref[g, 0, t:t + 1, :], pw_ref[g, 1, t:t + 1, :], bt_re, bt_im)
            w_scr[rows, :half] = w_re
            w_scr[rows, half:] = w_im
            g_re, g_im = _cmul(pv_ref[g, 0, t:t + 1, :], pv_ref[g, 1, t:t + 1, :], c_re, c_im)
            vt_ref[g, rows, :half] = g_re.astype(vt_ref.dtype)
            vt_ref[g, rows, half:] = (-g_im).astype(vt_ref.dtype)
        wt = w_scr[...].T
        wt_ref[g] = wt.astype(wt_ref.dtype)
        kk = jnp.dot(csel_ref[g], wt, preferred_element_type=F32, precision=lax.Precision.HIGHEST)
        kf = jnp.concatenate([kk[:H], zeros], axis=1)
        kb = pltpu.roll(jnp.concatenate([kk[H:], zeros], axis=1), (L - 1) * H, 1)
        dmat = jnp.concatenate([dcol_ref[g]] * (2 * f // LANES), axis=1)
        k = kf + kb + jnp.where(centre, dmat, 0.0)
        for t in range(L):
            off = f - (t + 1) * H
            win = k if off == 0 else pltpu.roll(k, 2 * f - off, 1)
            mt_ref[g, t * H:(t + 1) * H, :] = win[:, :f].astype(mt_ref.dtype)
        ac_re, ac_im = pa_ref[g, 0:1, :], pa_ref[g, 1:2, :]
        as_re, as_im = _cpow(ac_re, ac_im, seg_steps)
        a_ref[g, 0:1, :] = ac_re
        a_ref[g, 1:2, :] = ac_im
        a_ref[g, 2:3, :] = as_re
        a_ref[g, 3:4, :] = as_im


def _ssm_operators(lam_re, lam_im, log_dt, b_re, b_im, c_re, c_im, d_skip, seg_steps):
    L = SSM_CHUNK
    H = SSM_GROUP
    lam_re = lam_re.astype(F32)
    lam_im = lam_im.astype(F32)
    depth, _, ng, _ = lam_re.shape
    dt = jnp.exp(log_dt.astype(F32))[..., None]
    mag = jnp.exp(lam_re * dt)
    a_re = mag * jnp.cos(lam_im * dt)
    a_im = mag * jnp.sin(lam_im * dt)
    nr, ni = a_re - 1.0, a_im
    den = lam_re * lam_re + lam_im * lam_im
    f_re = (nr * lam_re + ni * lam_im) / den
    f_im = (ni * lam_re - nr * lam_im) / den
    br, bi = b_re.astype(F32), b_im.astype(F32)
    bb_re = f_re[..., None] * br - f_im[..., None] * bi
    bb_im = f_re[..., None] * bi + f_im[..., None] * br
    k = jnp.arange(L + 1, dtype=F32)[:, None]
    pmag = jnp.exp(lam_re[..., None, :] * dt[..., None, :] * k)
    parg = lam_im[..., None, :] * dt[..., None, :] * k
    p_re = pmag * jnp.cos(parg)
    p_im = pmag * jnp.sin(parg)
    both = lambda fwd, bwd: jnp.concatenate([fwd, bwd], axis=-1)
    reim = lambda re, im: jnp.stack([re, im], axis=2)
    pw = reim(both(p_re[:, 0, :, L - 1::-1][:, :, :L], p_re[:, 1, :, :L]),
              both(p_im[:, 0, :, L - 1::-1][:, :, :L], p_im[:, 1, :, :L]))
    pv = reim(both(p_re[:, 0, :, 1:], p_re[:, 1, :, :0:-1]), both(p_im[:, 0, :, 1:], p_im[:, 1, :, :0:-1]))
    pa = jnp.stack([both(p_re[:, 0, :, L], p_re[:, 1, :, L]), both(p_im[:, 0, :, L], p_im[:, 1, :, L])], axis=2)
    sw = lambda z: jnp.swapaxes(z, -1, -2)
    bbt = reim(both(sw(bb_re[:, 0]), sw(bb_re[:, 1])), both(sw(bb_im[:, 0]), sw(bb_im[:, 1])))
    cr, ci = c_re.astype(F32), c_im.astype(F32)
    cc = reim(both(cr[:, 0], cr[:, 1]), both(ci[:, 0], ci[:, 1]))
    z = jnp.zeros_like(cr[:, 0])
    csel = jnp.concatenate([jnp.concatenate([cr[:, 0], z, -ci[:, 0], z], axis=-1),
                            jnp.concatenate([z, cr[:, 1], z, -ci[:, 1]], axis=-1)], axis=-2)
    dcol = jnp.broadcast_to(d_skip.astype(F32).reshape(depth, ng, H, 1), (depth, ng, H, LANES))
    f = L * H
    ns = 2 * N_DIR * STATE
    gb = 8
    blk = lambda *shape: pl.BlockSpec((None, gb) + shape, lambda l, j: (l, j) + (0,) * len(shape))
    return pl.pallas_call(
        functools.partial(_ops_kernel, seg_steps),
        grid=(depth, ng // gb),
        in_specs=[blk(2, L, ns // 2), blk(2, L, ns // 2), blk(2, ns // 2), blk(2, H, ns // 2),
                  blk(2, H, ns // 2), blk(2 * H, ns), blk(H, LANES)],
        out_specs=[blk(f, f), blk(ns, f), blk(f, ns), blk(4, ns // 2)],
        out_shape=[jax.ShapeDtypeStruct((depth, ng, f, f), BF16), jax.ShapeDtypeStruct((depth, ng, ns, f), BF16),
                   jax.ShapeDtypeStruct((depth, ng, f, ns), BF16), jax.ShapeDtypeStruct((depth, ng, 4, ns // 2), F32)],
        scratch_shapes=[pltpu.VMEM((f, ns), F32)],
        compiler_params=pltpu.CompilerParams(
            dimension_semantics=("parallel", "parallel"), vmem_limit_bytes=VMEM_LIMIT_BYTES),
        name="ssm_operators",
    )(pw, pv, pa, bbt, cc, csel, dcol)


def _gelu_tanh(x):
    return 0.5 * x * (1.0 + jnp.tanh(np.sqrt(2.0 / np.pi).astype(np.float32)
                                     * (x + np.float32(0.044715) * (x * x * x))))


def _mixer_kernel(tiles_per_seq, ctx_len, first, x_ref, *refs):
    if first:
        xc_ref, *refs = refs
    (ysl_ref, ysc_ref, mod_ref, g_ref, wa_ref, ba_ref, wg_ref, bg_ref, cw_ref,
     woa_ref, wglu_ref, bglu_ref, wo_ref, o_ref) = refs
    tm, d = x_ref.shape
    dc = woa_ref.shape[0]
    n = ysc_ref.shape[-1]
    is_ctx, row = _tile_kind(tiles_per_seq)
    shift, scale, gate = (_mod_vec(mod_ref, row, k, d) for k in range(3))
    nb = MIXER_ROW_BLOCKS
    rows = tm // nb
    assert rows % ctx_len == 0 and rows % SSM_CHUNK == 0
    t = lax.broadcasted_iota(jnp.int32, (rows, 1), 0)
    seg = jnp.where(is_ctx, ctx_len - 1, GRID_W - 1)
    pos = t & seg
    blocks = [slice(k * rows, (k + 1) * rows) for k in range(nb)]
    xs, za, zg = [], [], []
    for k, r in enumerate(blocks):
        x = jnp.where(is_ctx, xc_ref[r, :], x_ref[r, :]) if first else x_ref[r, :]
        h = _norm_mod(x, g_ref[...], shift, scale).astype(BF16)
        xs.append(x)
        za.append(jnp.dot(h, wa_ref[...], preferred_element_type=F32) + ba_ref[...])
        zg.append(jnp.dot(h, wg_ref[...], preferred_element_type=F32) + bg_ref[...])
    y_a, y_b = [], []
    for k, r in enumerate(blocks):
        g_b, g_c, x_in = za[k][:, :dc], za[k][:, dc:2 * dc], za[k][:, 2 * dc:]
        v = g_c * x_in
        v_prev = jnp.where(pos == 0, 0.0, pltpu.roll(v, 1, 0))
        v_next = jnp.where(pos == seg, 0.0, pltpu.roll(v, rows - 1, 0))
        cv = cw_ref[0:1, :] * v_prev + cw_ref[1:2, :] * v + cw_ref[2:3, :] * v_next
        y_a.append(jnp.dot((g_b * cv).astype(BF16), woa_ref[...], preferred_element_type=F32))
        lo, hi = _unpack_rows(ysl_ref[k * rows // SSM_CHUNK:(k + 1) * rows // SSM_CHUNK, :PACKED_ROWS, :])
        ys = jnp.concatenate([lo, hi], axis=1).reshape(rows, n)
        s = _gelu_tanh(jnp.where(is_ctx, ysc_ref[r, :], ys))
        gl = jnp.dot(s.astype(BF16), wglu_ref[...], preferred_element_type=F32) + bglu_ref[...]
        y_b.append(gl[:, :d] * jax.nn.sigmoid(gl[:, d:]))
    for k, r in enumerate(blocks):
        gate_a, gate_b = zg[k][:, :d], zg[k][:, d:]
        merged = jax.nn.sigmoid(gate_a) * y_a[k] + jax.nn.sigmoid(gate_b) * y_b[k]
        out = jnp.dot(merged.astype(BF16), wo_ref[...], preferred_element_type=F32)
        o_ref[r, :] = xs[k] + gate * out


def _mixer(tokens, ys_lat, ys_ctx, params, bsz, seq, ctx_len):
    first = len(tokens) == 2
    d = tokens[0].shape[-1]
    tiles_per_seq = seq // TOK_TILE
    arrays, specs = params
    return pl.pallas_call(
        functools.partial(_mixer_kernel, tiles_per_seq, ctx_len, first),
        grid=(1 + bsz * tiles_per_seq,),
        in_specs=(_input_token_specs(d, tiles_per_seq) if first else [_stream_spec(d)])
        + _chunk_specs(ys_ctx.shape[-1], tiles_per_seq) + specs,
        out_specs=_stream_spec(d),
        out_shape=_stream_shape(bsz, seq, d),
        compiler_params=pltpu.CompilerParams(
            dimension_semantics=("arbitrary",), vmem_limit_bytes=VMEM_LIMIT_BYTES),
        name="mixer",
    )(*tokens, ys_lat, ys_ctx, *arrays)


def _ffn_kernel(tiles_per_seq, final, x_ref, *refs):
    if final:
        mod_ref, g_ref, win_ref, wout_ref, fg_ref, o_ref = refs
    else:
        mod_ref, g_ref, win_ref, wout_ref, modn_ref, gn_ref, wu_ref, bu_ref, o_ref, ul_ref, uc_ref = refs
    tm, d = x_ref.shape
    dff = wout_ref.shape[0]
    is_ctx, row = _tile_kind(tiles_per_seq, first_step=1 if final else 0)
    shift, scale, gate_mod = (_mod_vec(mod_ref, row, k, d) for k in (3, 4, 5))
    rows = tm // FFN_ROW_BLOCKS
    blocks = [slice(k * rows, (k + 1) * rows) for k in range(FFN_ROW_BLOCKS)]
    xs, zs = [], []
    for r in blocks:
        x = x_ref[r, :]
        h = _norm_mod(x, g_ref[...], shift, scale)
        xs.append(x)
        zs.append(jnp.dot(h.astype(BF16), win_ref[...], preferred_element_type=F32))
    ys = []
    for x, z in zip(xs, zs):
        gate, up = z[:, :dff], z[:, dff:]
        act = gate * jax.nn.sigmoid(gate) * up
        out = jnp.dot(act.astype(BF16), wout_ref[...], preferred_element_type=F32)
        ys.append(x + gate_mod * out)
    if final:
        for r, y in zip(blocks, ys):
            ms = jnp.mean(y * y, axis=-1, keepdims=True)
            o_ref[r, :] = y * lax.rsqrt(ms + RMS_EPS) * fg_ref[...]
    else:
        for r, y in zip(blocks, ys):
            o_ref[r, :] = y
        _emit_ssm_inputs(jnp.concatenate(ys, axis=0), row, is_ctx, modn_ref, gn_ref, wu_ref, bu_ref, ul_ref, uc_ref)


def _ffn(xs, params, nxt, bsz, seq):
    d = xs.shape[-1]
    tiles_per_seq = seq // TOK_TILE
    arrays, specs = params
    return pl.pallas_call(
        functools.partial(_ffn_kernel, tiles_per_seq, False),
        grid=(1 + bsz * tiles_per_seq,),
        in_specs=[_stream_spec(d)] + specs + nxt[1],
        out_specs=[_stream_spec(d)] + _chunk_specs(nxt[2], tiles_per_seq),
        out_shape=[_stream_shape(bsz, seq, d)] + _ssm_input_shapes(bsz, seq, nxt[2]),
        compiler_params=pltpu.CompilerParams(
            dimension_semantics=("arbitrary",), vmem_limit_bytes=VMEM_LIMIT_BYTES),
        name="ffn",
    )(xs, *arrays, *nxt[0])


def _ffn_final(xs, params, final_g, bsz, seq):
    d = xs.shape[-1]
    tiles_per_seq = seq // TOK_TILE
    arrays, specs = params
    return pl.pallas_call(
        functools.partial(_ffn_kernel, tiles_per_seq, True),
        grid=(bsz * tiles_per_seq,),
        in_specs=[_stream_spec(d, first_step=1)] + specs + [_layer_spec(final_g, 0)],
        out_specs=pl.BlockSpec((None, TOK_TILE, d), lambda s: (s // tiles_per_seq, s % tiles_per_seq, 0)),
        out_shape=jax.ShapeDtypeStruct((bsz, seq, d), F32),
        compiler_params=pltpu.CompilerParams(
            dimension_semantics=("arbitrary",), vmem_limit_bytes=VMEM_LIMIT_BYTES),
        name="ffn_final",
    )(xs, *arrays, final_g)


def kernel(x, c, ctx, c_ctx, w_mod, b_mod, norm1_g, norm2_g, w_in, b_in, conv_w, w_out_a, lam_re, lam_im,
           log_dt, b_re, b_im, c_re, c_im, d_skip, w_glu, b_glu, w_o, w_ff_in, w_ff_out, final_g):
    bsz, seq, d = x.shape
    depth = w_mod.shape[0]
    ctx_len = ctx.shape[1]
    d_conv = conv_w.shape[-1]
    d_ssm = d_skip.shape[-1]
    u_lo, u_hi = 3 * d_conv, 3 * d_conv + d_ssm
    assert bsz * ctx_len == TOK_TILE and TOK_TILE % GRID_W == 0 and bsz <= 2
    assert ctx_len == SSM_CHUNK * SUBLANES and d_ssm % LANES == 0
    assert seq % (SUBLANES * TOK_TILE) == 0
    seg_steps = seq // (SSM_CHUNK * SUBLANES)

    cvec = jnp.zeros((8, d), F32).at[:bsz].set(c.astype(F32)).at[2].set(c_ctx.astype(F32))
    mod = _mod_all(cvec, w_mod, b_mod)

    mt_op, wt_op, vt_op, a_op = _ssm_operators(lam_re, lam_im, log_dt, b_re, b_im, c_re, c_im, d_skip,
                                               seg_steps)

    w_in, w_out_a, w_glu, w_o = (w.astype(BF16) for w in (w_in, w_out_a, w_glu, w_o))
    w_ff_in, w_ff_out = w_ff_in.astype(BF16), w_ff_out.astype(BF16)
    row = lambda v: v.reshape(v.shape[0], 1, v.shape[-1])
    b_in, b_glu, g1, g2, fg = row(b_in), row(b_glu), row(norm1_g), row(norm2_g), final_g.reshape(1, 1, d)
    gates_lo = u_hi

    def operands(l, *items):
        arrays = [arr for arr, _ in items]
        return arrays, [_layer_spec(arr, l, cols) for arr, cols in items]

    u_cols = (u_lo, d_ssm)
    ssm_in = lambda l: operands(l, (mod, None), (g1, None), (w_in, u_cols), (b_in, u_cols)) + (d_ssm,)
    mixer_in = lambda l: operands(
        l, (mod, None), (g1, None), (w_in, (0, u_lo)), (b_in, (0, u_lo)),
        (w_in, (gates_lo, 2 * d)), (b_in, (gates_lo, 2 * d)), (conv_w, None), (w_out_a, None),
        (w_glu, None), (b_glu, None), (w_o, None))
    ffn_in = lambda l: operands(l, (mod, None), (g2, None), (w_ff_in, None), (w_ff_out, None))

    tokens = (x, ctx.reshape(bsz * ctx_len, d))
    u_lat, u_ctx = _uproj(*tokens, ssm_in(0))
    for l in range(depth):
        ys_lat, ys_ctx = _ssm(u_lat, u_ctx, mt_op, wt_op, vt_op, a_op, l)
        xs = _mixer(tokens, ys_lat, ys_ctx, mixer_in(l), bsz, seq, ctx_len)
        if l + 1 < depth:
            xs, u_lat, u_ctx = _ffn(xs, ffn_in(l), ssm_in(l + 1), bsz, seq)
            tokens = (xs,)
    return _ffn_final(xs, ffn_in(depth - 1), fg, bsz, seq)
```

```python
import functools

import numpy as np
import jax
import jax.numpy as jnp
from jax import lax
from jax.experimental import pallas as pl
from jax.experimental.pallas import tpu as pltpu

GRID_W = 64
CONV_K = 3
SSM_GROUP = 16
STATE = 64
N_DIR = 2
RMS_EPS = 1e-6

TOK_TILE = 512
SSM_CHUNK = 32
PACKED_ROWS = SSM_CHUNK // 2
CHUNK_PITCH = 24
SUBLANES = 8
LANES = 128
SSM_LANE_GROUPS = LANES // SSM_GROUP
SSM_SCAN_GROUPS = 4
SSM_IO_PARTS = 2
MIXER_ROW_BLOCKS = 2
FFN_ROW_BLOCKS = 2
VMEM_LIMIT_BYTES = 56 * 1024 * 1024

F32 = jnp.float32
BF16 = jnp.bfloat16
U32 = jnp.uint32


def _pack_rows(lo, hi):
    bits = lambda v: lax.bitcast_convert_type(v.astype(BF16).astype(F32), U32)
    return bits(hi) | (bits(lo) >> 16)


def _unpack_rows(w):
    return (lax.bitcast_convert_type(w << 16, F32),
            lax.bitcast_convert_type(w & jnp.uint32(0xFFFF0000), F32))


def _layer_spec(arr, l, cols=None):
    _, rows, n = arr.shape
    start, width = (0, n) if cols is None else cols
    assert start % width == 0
    return pl.BlockSpec((None, rows, width), lambda *_: (l, 0, start // width), pipeline_mode=pl.Buffered(1))


def _tile_coords(s, tiles_per_seq):
    k = jnp.maximum(s - 1, 0)
    return k // tiles_per_seq, k % tiles_per_seq


def _input_token_specs(d, tiles_per_seq):
    return [pl.BlockSpec((None, TOK_TILE, d), lambda s: (*_tile_coords(s, tiles_per_seq), 0)),
            pl.BlockSpec((TOK_TILE, d), lambda s: (0, 0))]


def _stream_spec(d, first_step=0):
    return pl.BlockSpec((None, TOK_TILE, d), lambda s: (s + first_step, 0, 0))


def _stream_shape(bsz, seq, d):
    return jax.ShapeDtypeStruct((1 + bsz * (seq // TOK_TILE), TOK_TILE, d), F32)


def _chunk_specs(n, tiles_per_seq):
    cpt = TOK_TILE // SSM_CHUNK
    tiles_per_seg = tiles_per_seq // SUBLANES

    def index(s):
        b, i = _tile_coords(s, tiles_per_seq)
        return (b, i % tiles_per_seg, i // tiles_per_seg, 0, 0)

    return [pl.BlockSpec((None, cpt, None, CHUNK_PITCH, n), index),
            pl.BlockSpec((TOK_TILE, n), lambda s: (0, 0))]


def _tile_kind(tiles_per_seq, first_step=0):
    s = pl.program_id(0) + first_step
    is_ctx = s == 0
    return is_ctx, jnp.where(is_ctx, 2, _tile_coords(s, tiles_per_seq)[0])


def _norm_mod(x, g, shift, scale):
    ms = jnp.mean(x * x, axis=-1, keepdims=True)
    y = x * lax.rsqrt(ms + RMS_EPS) * g
    return y * (1.0 + scale) + shift


def _mod_vec(mod_ref, row, k, d):
    return mod_ref[pl.ds(row, 1), k * d:(k + 1) * d]


def _mod_kernel(c_ref, w_ref, b_ref, o_ref):
    c = c_ref[...]
    s = c * jax.nn.sigmoid(c)
    o_ref[...] = jnp.dot(s.astype(BF16), w_ref[...].astype(BF16),
                         preferred_element_type=F32) + b_ref[...]


def _mod_all(cvec, w_mod, b_mod):
    depth, d, n = w_mod.shape
    tn = 1536
    return pl.pallas_call(
        _mod_kernel,
        grid=(depth, n // tn),
        in_specs=[pl.BlockSpec((8, d), lambda l, j: (0, 0)),
                  pl.BlockSpec((None, d, tn), lambda l, j: (l, 0, j)),
                  pl.BlockSpec((None, 1, tn), lambda l, j: (l, 0, j))],
        out_specs=pl.BlockSpec((None, 8, tn), lambda l, j: (l, 0, j)),
        out_shape=jax.ShapeDtypeStruct((depth, 8, n), F32),
        compiler_params=pltpu.CompilerParams(
            dimension_semantics=("arbitrary", "arbitrary"), vmem_limit_bytes=VMEM_LIMIT_BYTES),
        name="adaln_mod",
    )(cvec, w_mod, b_mod.reshape(depth, 1, n))


def _emit_ssm_inputs(x, row, is_ctx, mod_ref, g_ref, w_ref, b_ref, ul_ref, uc_ref):
    d = x.shape[-1]
    h = _norm_mod(x, g_ref[...], _mod_vec(mod_ref, row, 0, d), _mod_vec(mod_ref, row, 1, d))
    u = jnp.dot(h.astype(BF16), w_ref[...], preferred_element_type=F32) + b_ref[...]
    cpt, pitch, n = ul_ref.shape
    u3 = u.reshape(cpt, SSM_CHUNK, n)
    ul_ref[:, :PACKED_ROWS, :] = _pack_rows(u3[:, :PACKED_ROWS], u3[:, PACKED_ROWS:])
    ul_ref[:, PACKED_ROWS:, :] = jnp.zeros((cpt, pitch - PACKED_ROWS, n), U32)

    @pl.when(is_ctx)
    def _():
        uc_ref[...] = u


def _uproj_kernel(tiles_per_seq, xl_ref, xc_ref, mod_ref, g_ref, w_ref, b_ref, ul_ref, uc_ref):
    is_ctx, row = _tile_kind(tiles_per_seq)
    x = jnp.where(is_ctx, xc_ref[...], xl_ref[...])
    _emit_ssm_inputs(x, row, is_ctx, mod_ref, g_ref, w_ref, b_ref, ul_ref, uc_ref)


def _ssm_input_shapes(bsz, seq, n):
    steps = seq // (SSM_CHUNK * SUBLANES)
    return [jax.ShapeDtypeStruct((bsz, steps, SUBLANES, CHUNK_PITCH, n), U32),
            jax.ShapeDtypeStruct((TOK_TILE, n), F32)]


def _uproj(xl, xc, ssm_in):
    bsz, seq, d = xl.shape
    tiles_per_seq = seq // TOK_TILE
    arrays, specs, n_u = ssm_in
    return pl.pallas_call(
        functools.partial(_uproj_kernel, tiles_per_seq),
        grid=(1 + bsz * tiles_per_seq,),
        in_specs=_input_token_specs(d, tiles_per_seq) + specs,
        out_specs=_chunk_specs(n_u, tiles_per_seq),
        out_shape=_ssm_input_shapes(bsz, seq, n_u),
        compiler_params=pltpu.CompilerParams(
            dimension_semantics=("arbitrary",), vmem_limit_bytes=VMEM_LIMIT_BYTES),
        name="ssm_uproj",
    )(xl, xc, *arrays)


def _cmul(a_re, a_im, b_re, b_im):
    return a_re * b_re - a_im * b_im, a_re * b_im + a_im * b_re


def _ssm_kernel(ctx_chunks, ul_ref, uc_ref, mt_ref, wt_ref, vt_ref, a_ref, yl_ref, yc_ref,
                ut_scr, yt_scr, x_scr, hf_scr, hb_scr):
    L = SSM_CHUNK
    H = SSM_GROUP
    parts = SSM_IO_PARTS
    n_part = ul_ref.shape[0] // CHUNK_PITCH
    n_lat = n_part * parts
    n_all = ut_scr.shape[-1]
    phase = pl.program_id(2)
    trow = lambda t: pl.ds(pl.multiple_of(t * H, H), H)

    def put(t, cols, at):
        for g in range(SSM_LANE_GROUPS):
            ut_scr[g, trow(t), cols] = at[g * H:(g + 1) * H, :].astype(BF16)

    for q in range(parts):
        @pl.when(phase == q)
        def _():
            cols = slice(q * n_part, (q + 1) * n_part)

            def body(r, carry):
                lo, hi = _unpack_rows(ul_ref[pl.ds(r, n_part, stride=CHUNK_PITCH), :].T)
                put(r, cols, lo)
                put(r + PACKED_ROWS, cols, hi)
                return carry
            lax.fori_loop(0, PACKED_ROWS, body, 0, unroll=8)

    @pl.when(phase == parts)
    def _():
        pad = jnp.zeros((n_all - n_lat - ctx_chunks, LANES), F32)

        def body(t, carry):
            put(t, slice(n_lat, n_all), jnp.concatenate([uc_ref[pl.ds(t, ctx_chunks, stride=L), :], pad], axis=0).T)
            return carry
        lax.fori_loop(0, L, body, 0, unroll=16)
        _ssm_compute(ctx_chunks, n_lat, mt_ref, wt_ref, vt_ref, a_ref, ut_scr, yt_scr, x_scr, hf_scr, hb_scr)

        def drain_ctx(t, carry):
            y = yt_scr[t, :, n_lat:].astype(F32).T
            yc_ref[pl.ds(t, ctx_chunks, stride=L), :] = y[:ctx_chunks]
            return carry
        lax.fori_loop(0, L, drain_ctx, 0, unroll=16)

    for q in range(parts):
        @pl.when(phase == parts + 1 + q)
        def _():
            cols = slice(q * n_part, (q + 1) * n_part)

            def drain(r, carry):
                w = _pack_rows(yt_scr[r, :, cols].astype(F32), yt_scr[r + PACKED_ROWS, :, cols].astype(F32))
                yl_ref[pl.ds(r, n_part, stride=CHUNK_PITCH), :] = w.T
                return carry
            lax.fori_loop(0, PACKED_ROWS, drain, 0, unroll=8)
            for r in range(PACKED_ROWS, CHUNK_PITCH):
                yl_ref[pl.ds(r, n_part, stride=CHUNK_PITCH), :] = jnp.zeros((n_part, LANES), U32)


def _ssm_compute(ctx_chunks, n_lat, mt_ref, wt_ref, vt_ref, a_ref, ut_scr, yt_scr, x_scr, hf_scr, hb_scr):
    L = SSM_CHUNK
    H = SSM_GROUP
    sub = SUBLANES
    n_all = ut_scr.shape[-1]
    steps = n_lat // sub
    half = STATE * N_DIR
    gl = SSM_LANE_GROUPS
    gs = SSM_SCAN_GROUPS

    is_fwd = lax.broadcasted_iota(jnp.int32, (sub, half), 1) < STATE
    is_fwd1 = is_fwd[:1]
    sl = lax.broadcasted_iota(jnp.int32, (sub, half), 0)
    blk = lambda j: pl.ds(pl.multiple_of(j * sub, sub), sub)
    zero = jnp.zeros((sub, half), F32)
    lane2 = lax.broadcasted_iota(jnp.int32, (n_all, 2 * half), 1)
    take_fwd = (lane2 % half) < STATE
    tail0 = n_lat + ctx_chunks
    assert tail0 % sub == 0 and ctx_chunks == sub

    for g0 in range(0, gl, gs):
        coef = lambda g, r, rows=sub: jnp.broadcast_to(a_ref[g0 + g, r:r + 1, :], (rows, half))
        a_re = [coef(g, 0) for g in range(gs)]
        a_im = [coef(g, 1) for g in range(gs)]

        for g in range(gs):
            xt = jnp.dot(wt_ref[g0 + g], ut_scr[g0 + g], preferred_element_type=F32)
            x_scr[g] = xt.T
            hf_scr[g, tail0:, :] = jnp.zeros((n_all - tail0, 2 * half), F32)
            hb_scr[g, tail0:, :] = jnp.zeros((n_all - tail0, 2 * half), F32)

        h0 = []
        for g in range(gs):
            xc = x_scr[g, n_lat:n_lat + ctx_chunks, :]
            ar, ai = a_re[g][:1], a_im[g][:1]
            h_re = h_im = jnp.zeros((1, half), F32)
            hf_rows, hb_rows = [], [None] * ctx_chunks
            for k in range(ctx_chunks):
                kb = ctx_chunks - 1 - k
                hf_rows.append((h_re, h_im))
                hb_rows[kb] = (h_re, h_im)
                x_re = jnp.where(is_fwd1, xc[k:k + 1, :half], xc[kb:kb + 1, :half])
                x_im = jnp.where(is_fwd1, xc[k:k + 1, half:], xc[kb:kb + 1, half:])
                p_re, p_im = _cmul(ar, ai, h_re, h_im)
                h_re, h_im = p_re + x_re, p_im + x_im
            cat = lambda rows, part: jnp.concatenate([r[part] for r in rows], axis=0)
            hf_scr[g, n_lat:n_lat + ctx_chunks, :half] = cat(hf_rows, 0)
            hf_scr[g, n_lat:n_lat + ctx_chunks, half:] = cat(hf_rows, 1)
            hb_scr[g, n_lat:n_lat + ctx_chunks, :half] = cat(hb_rows, 0)
            hb_scr[g, n_lat:n_lat + ctx_chunks, half:] = cat(hb_rows, 1)
            h0.append((jnp.broadcast_to(h_re, (sub, half)), jnp.broadcast_to(h_im, (sub, half))))

        def scan_step(j, carry):
            jb = steps - 1 - j
            new = []
            for g in range(gs):
                h_re, h_im = carry[g]
                x_re = jnp.where(is_fwd, x_scr[g, blk(j), :half], x_scr[g, blk(jb), :half])
                x_im = jnp.where(is_fwd, x_scr[g, blk(j), half:], x_scr[g, blk(jb), half:])
                hf_scr[g, blk(j), :half] = h_re
                hf_scr[g, blk(j), half:] = h_im
                hb_scr[g, blk(jb), :half] = h_re
                hb_scr[g, blk(jb), half:] = h_im
                p_re, p_im = _cmul(a_re[g], a_im[g], h_re, h_im)
                new.append((p_re + x_re, p_im + x_im))
            return tuple(new)

        ends = lax.fori_loop(0, steps, scan_step, tuple((zero, zero) for _ in range(gs)), unroll=2)

        carries = []
        for g in range(gs):
            def shift(t, h):
                return jnp.where(is_fwd, jnp.where(sl < 1, h, pltpu.roll(t, 1, 0)),
                                 jnp.where(sl >= sub - 1, h, pltpu.roll(t, sub - 1, 0)))
            e_re, e_im = ends[g]
            s_re, s_im = coef(g, 2), coef(g, 3)
            c_re, c_im = shift(zero, h0[g][0]), shift(zero, h0[g][1])
            for _ in range(sub - 1):
                p_re, p_im = _cmul(s_re, s_im, c_re, c_im)
                c_re, c_im = shift(e_re + p_re, h0[g][0]), shift(e_im + p_im, h0[g][1])
            carries.append((c_re, c_im))

        def fix_step(j, carry):
            jb = steps - 1 - j
            new = []
            for g in range(gs):
                d_re, d_im = carry[g]
                hf_scr[g, blk(j), :half] += d_re
                hf_scr[g, blk(j), half:] += d_im
                hb_scr[g, blk(jb), :half] += d_re
                hb_scr[g, blk(jb), half:] += d_im
                new.append(_cmul(a_re[g], a_im[g], d_re, d_im))
            return tuple(new)

        lax.fori_loop(0, steps, fix_step, tuple(carries), unroll=2)

        for g in range(gs):
            h_in = jnp.where(take_fwd, hf_scr[g], hb_scr[g]).T.astype(BF16)
            yt = (jnp.dot(mt_ref[g0 + g], ut_scr[g0 + g], preferred_element_type=F32)
                  + jnp.dot(vt_ref[g0 + g], h_in, preferred_element_type=F32))
            for t in range(L):
                yt_scr[t, (g0 + g) * H:(g0 + g + 1) * H, :] = yt[t * H:(t + 1) * H, :].astype(BF16)


def _ssm(u_lat, u_ctx, mt, wt, vt, a, l):
    bsz, steps, nseg, pitch, n = u_lat.shape
    L = SSM_CHUNK
    ctx_len = u_ctx.shape[0] // bsz
    ctx_chunks = ctx_len // L
    n_lat = steps * nseg
    n_all = -(-(n_lat + ctx_chunks) // LANES) * LANES
    gl = SSM_LANE_GROUPS
    f = L * SSM_GROUP
    ns = wt.shape[-2]
    parts = SSM_IO_PARTS
    n_blocks = n // LANES
    part_rows = n_lat * pitch // parts
    lat_in = pl.BlockSpec((None, part_rows, LANES), lambda b, j, p: (b, jnp.minimum(p, parts - 1), j))
    lat_out = pl.BlockSpec((None, part_rows, LANES), lambda b, j, p: (b, jnp.maximum(p - parts - 1, 0), j))
    ctx_spec = pl.BlockSpec((None, ctx_len, LANES), lambda b, j, p: (b, 0, j))
    grp = lambda shape: pl.BlockSpec(
        (None, gl) + shape, lambda b, j, p: (l, jnp.where(p > parts, (j + 1) % n_blocks, j), 0, 0))
    y_lat, y_ctx = pl.pallas_call(
        functools.partial(_ssm_kernel, ctx_chunks),
        grid=(bsz, n_blocks, 2 * parts + 1),
        in_specs=[lat_in, ctx_spec, grp((f, f)), grp((ns, f)), grp((f, ns)), grp((4, ns // 2))],
        out_specs=[lat_out, ctx_spec],
        out_shape=[jax.ShapeDtypeStruct((bsz, n_lat * pitch, n), U32),
                   jax.ShapeDtypeStruct((bsz, ctx_len, n), F32)],
        scratch_shapes=[pltpu.VMEM((gl, f, n_all), BF16), pltpu.VMEM((L, LANES, n_all), BF16)]
        + [pltpu.VMEM((SSM_SCAN_GROUPS, n_all, ns), F32)] * 3,
        compiler_params=pltpu.CompilerParams(
            dimension_semantics=("arbitrary", "arbitrary", "arbitrary"), vmem_limit_bytes=VMEM_LIMIT_BYTES),
        name="ssm_chunked",
    )(u_lat.reshape(bsz, n_lat * pitch, n), u_ctx.reshape(bsz, ctx_len, n), mt, wt, vt, a)
    return y_lat.reshape(u_lat.shape), y_ctx.reshape(u_ctx.shape)


def _cpow(z_re, z_im, n):
    out = None
    while n:
        if n & 1:
            out = (z_re, z_im) if out is None else _cmul(out[0], out[1], z_re, z_im)
        n >>= 1
        if n:
            z_re, z_im = _cmul(z_re, z_im, z_re, z_im)
    return out


def _ops_kernel(seg_steps, pw_ref, pv_ref, pa_ref, bbt_ref, c_ref, csel_ref, dcol_ref,
                mt_ref, wt_ref, vt_ref, a_ref, w_scr):
    gb = mt_ref.shape[0]
    f = mt_ref.shape[-1]
    H = SSM_GROUP
    L = f // H
    half = STATE * N_DIR
    lane = lax.broadcasted_iota(jnp.int32, (H, 2 * f), 1)
    row = lax.broadcasted_iota(jnp.int32, (H, 2 * f), 0)
    centre = lane == (L - 1) * H + row
    zeros = jnp.zeros((H, f), F32)
    for g in range(gb):
        bt_re, bt_im = bbt_ref[g, 0], bbt_ref[g, 1]
        c_re, c_im = c_ref[g, 0], c_ref[g, 1]
        for t in range(L):
            rows = slice(t * H, (t + 1) * H)
            w_re, w_im = _cmul(pw_ref[g, 0, t:t + 1, :], pw_ref[g, 1, t:t + 1, :], bt_re, bt_im)
            w_scr[rows, :half] = w_re
            w_scr[rows, half:] = w_im
            g_re, g_im = _cmul(pv_ref[g, 0, t:t + 1, :], pv_ref[g, 1, t:t + 1, :], c_re, c_im)
            vt_ref[g, rows, :half] = g_re.astype(vt_ref.dtype)
            vt_ref[g, rows, half:] = (-g_im).astype(vt_ref.dtype)
        wt = w_scr[...].T
        wt_ref[g] = wt.astype(wt_ref.dtype)
        kk = jnp.dot(csel_ref[g], wt, preferred_element_type=F32, precision=lax.Precision.HIGHEST)
        kf = jnp.concatenate([kk[:H], zeros], axis=1)
        kb = pltpu.roll(jnp.concatenate([kk[H:], zeros], axis=1), (L - 1) * H, 1)
        dmat = jnp.concatenate([dcol_ref[g]] * (2 * f // LANES), axis=1)
        k = kf + kb + jnp.where(centre, dmat, 0.0)
        for t in range(L):
            off = f - (t + 1) * H
            win = k if off == 0 else pltpu.roll(k, 2 * f - off, 1)
            mt_ref[g, t * H:(t + 1) * H, :] = win[:, :f].astype(mt_ref.dtype)
        ac_re, ac_im = pa_ref[g, 0:1, :], pa_ref[g, 1:2, :]
        as_re, as_im = _cpow(ac_re, ac_im, seg_steps)
        a_ref[g, 0:1, :] = ac_re
        a_ref[g, 1:2, :] = ac_im
        a_ref[g, 2:3, :] = as_re
        a_ref[g, 3:4, :] = as_im


def _ssm_operators(lam_re, lam_im, log_dt, b_re, b_im, c_re, c_im, d_skip, seg_steps):
    L = SSM_CHUNK
    H = SSM_GROUP
    lam_re = lam_re.astype(F32)
    lam_im = lam_im.astype(F32)
    depth, _, ng, _ = lam_re.shape
    dt = jnp.exp(log_dt.astype(F32))[..., None]
    mag = jnp.exp(lam_re * dt)
    a_re = mag * jnp.cos(lam_im * dt)
    a_im = mag * jnp.sin(lam_im * dt)
    nr, ni = a_re - 1.0, a_im
    den = lam_re * lam_re + lam_im * lam_im
    f_re = (nr * lam_re + ni * lam_im) / den
    f_im = (ni * lam_re - nr * lam_im) / den
    br, bi = b_re.astype(F32), b_im.astype(F32)
    bb_re = f_re[..., None] * br - f_im[..., None] * bi
    bb_im = f_re[..., None] * bi + f_im[..., None] * br
    k = jnp.arange(L + 1, dtype=F32)[:, None]
    pmag = jnp.exp(lam_re[..., None, :] * dt[..., None, :] * k)
    parg = lam_im[..., None, :] * dt[..., None, :] * k
    p_re = pmag * jnp.cos(parg)
    p_im = pmag * jnp.sin(parg)
    both = lambda fwd, bwd: jnp.concatenate([fwd, bwd], axis=-1)
    reim = lambda re, im: jnp.stack([re, im], axis=2)
    pw = reim(both(p_re[:, 0, :, L - 1::-1][:, :, :L], p_re[:, 1, :, :L]),
              both(p_im[:, 0, :, L - 1::-1][:, :, :L], p_im[:, 1, :, :L]))
    pv = reim(both(p_re[:, 0, :, 1:], p_re[:, 1, :, :0:-1]), both(p_im[:, 0, :, 1:], p_im[:, 1, :, :0:-1]))
    pa = jnp.stack([both(p_re[:, 0, :, L], p_re[:, 1, :, L]), both(p_im[:, 0, :, L], p_im[:, 1, :, L])], axis=2)
    sw = lambda z: jnp.swapaxes(z, -1, -2)
    bbt = reim(both(sw(bb_re[:, 0]), sw(bb_re[:, 1])), both(sw(bb_im[:, 0]), sw(bb_im[:, 1])))
    cr, ci = c_re.astype(F32), c_im.astype(F32)
    cc = reim(both(cr[:, 0], cr[:, 1]), both(ci[:, 0], ci[:, 1]))
    z = jnp.zeros_like(cr[:, 0])
    csel = jnp.concatenate([jnp.concatenate([cr[:, 0], z, -ci[:, 0], z], axis=-1),
                            jnp.concatenate([z, cr[:, 1], z, -ci[:, 1]], axis=-1)], axis=-2)
    dcol = jnp.broadcast_to(d_skip.astype(F32).reshape(depth, ng, H, 1), (depth, ng, H, LANES))
    f = L * H
    ns = 2 * N_DIR * STATE
    gb = 8
    blk = lambda *shape: pl.BlockSpec((None, gb) + shape, lambda l, j: (l, j) + (0,) * len(shape))
    return pl.pallas_call(
        functools.partial(_ops_kernel, seg_steps),
        grid=(depth, ng // gb),
        in_specs=[blk(2, L, ns // 2), blk(2, L, ns // 2), blk(2, ns // 2), blk(2, H, ns // 2),
                  blk(2, H, ns // 2), blk(2 * H, ns), blk(H, LANES)],
        out_specs=[blk(f, f), blk(ns, f), blk(f, ns), blk(4, ns // 2)],
        out_shape=[jax.ShapeDtypeStruct((depth, ng, f, f), BF16), jax.ShapeDtypeStruct((depth, ng, ns, f), BF16),
                   jax.ShapeDtypeStruct((depth, ng, f, ns), BF16), jax.ShapeDtypeStruct((depth, ng, 4, ns // 2), F32)],
        scratch_shapes=[pltpu.VMEM((f, ns), F32)],
        compiler_params=pltpu.CompilerParams(
            dimension_semantics=("parallel", "parallel"), vmem_limit_bytes=VMEM_LIMIT_BYTES),
        name="ssm_operators",
    )(pw, pv, pa, bbt, cc, csel, dcol)


def _gelu_tanh(x):
    return 0.5 * x * (1.0 + jnp.tanh(np.sqrt(2.0 / np.pi).astype(np.float32)
                                     * (x + np.float32(0.044715) * (x * x * x))))


def _mixer_kernel(tiles_per_seq, ctx_len, first, x_ref, *refs):
    if first:
        xc_ref, *refs = refs
    (ysl_ref, ysc_ref, mod_ref, g_ref, wa_ref, ba_ref, wg_ref, bg_ref, cw_ref,
     woa_ref, wglu_ref, bglu_ref, wo_ref, o_ref) = refs
    tm, d = x_ref.shape
    dc = woa_ref.shape[0]
    n = ysc_ref.shape[-1]
    is_ctx, row = _tile_kind(tiles_per_seq)
    shift, scale, gate = (_mod_vec(mod_ref, row, k, d) for k in range(3))
    nb = MIXER_ROW_BLOCKS
    rows = tm // nb
    assert rows % ctx_len == 0 and rows % SSM_CHUNK == 0
    t = lax.broadcasted_iota(jnp.int32, (rows, 1), 0)
    seg = jnp.where(is_ctx, ctx_len - 1, GRID_W - 1)
    pos = t & seg
    blocks = [slice(k * rows, (k + 1) * rows) for k in range(nb)]
    xs, za, zg = [], [], []
    for k, r in enumerate(blocks):
        x = jnp.where(is_ctx, xc_ref[r, :], x_ref[r, :]) if first else x_ref[r, :]
        h = _norm_mod(x, g_ref[...], shift, scale).astype(BF16)
        xs.append(x)
        za.append(jnp.dot(h, wa_ref[...], preferred_element_type=F32) + ba_ref[...])
        zg.append(jnp.dot(h, wg_ref[...], preferred_element_type=F32) + bg_ref[...])
    y_a, y_b = [], []
    for k, r in enumerate(blocks):
        g_b, g_c, x_in = za[k][:, :dc], za[k][:, dc:2 * dc], za[k][:, 2 * dc:]
        v = g_c * x_in
        v_prev = jnp.where(pos == 0, 0.0, pltpu.roll(v, 1, 0))
        v_next = jnp.where(pos == seg, 0.0, pltpu.roll(v, rows - 1, 0))
        cv = cw_ref[0:1, :] * v_prev + cw_ref[1:2, :] * v + cw_ref[2:3, :] * v_next
        y_a.append(jnp.dot((g_b * cv).astype(BF16), woa_ref[...], preferred_element_type=F32))
        lo, hi = _unpack_rows(ysl_ref[k * rows // SSM_CHUNK:(k + 1) * rows // SSM_CHUNK, :PACKED_ROWS, :])
        ys = jnp.concatenate([lo, hi], axis=1).reshape(rows, n)
        s = _gelu_tanh(jnp.where(is_ctx, ysc_ref[r, :], ys))
        gl = jnp.dot(s.astype(BF16), wglu_ref[...], preferred_element_type=F32) + bglu_ref[...]
        y_b.append(gl[:, :d] * jax.nn.sigmoid(gl[:, d:]))
    for k, r in enumerate(blocks):
        gate_a, gate_b = zg[k][:, :d], zg[k][:, d:]
        merged = jax.nn.sigmoid(gate_a) * y_a[k] + jax.nn.sigmoid(gate_b) * y_b[k]
        out = jnp.dot(merged.astype(BF16), wo_ref[...], preferred_element_type=F32)
        o_ref[r, :] = xs[k] + gate * out


def _mixer(tokens, ys_lat, ys_ctx, params, bsz, seq, ctx_len):
    first = len(tokens) == 2
    d = tokens[0].shape[-1]
    tiles_per_seq = seq // TOK_TILE
    arrays, specs = params
    return pl.pallas_call(
        functools.partial(_mixer_kernel, tiles_per_seq, ctx_len, first),
        grid=(1 + bsz * tiles_per_seq,),
        in_specs=(_input_token_specs(d, tiles_per_seq) if first else [_stream_spec(d)])
        + _chunk_specs(ys_ctx.shape[-1], tiles_per_seq) + specs,
        out_specs=_stream_spec(d),
        out_shape=_stream_shape(bsz, seq, d),
        compiler_params=pltpu.CompilerParams(
            dimension_semantics=("arbitrary",), vmem_limit_bytes=VMEM_LIMIT_BYTES),
        name="mixer",
    )(*tokens, ys_lat, ys_ctx, *arrays)


def _ffn_kernel(tiles_per_seq, final, x_ref, *refs):
    if final:
        mod_ref, g_ref, win_ref, wout_ref, fg_ref, o_ref = refs
    else:
        mod_ref, g_ref, win_ref, wout_ref, modn_ref, gn_ref, wu_ref, bu_ref, o_ref, ul_ref, uc_ref = refs
    tm, d = x_ref.shape
    dff = wout_ref.shape[0]
    is_ctx, row = _tile_kind(tiles_per_seq, first_step=1 if final else 0)
    shift, scale, gate_mod = (_mod_vec(mod_ref, row, k, d) for k in (3, 4, 5))
    rows = tm // FFN_ROW_BLOCKS
    blocks = [slice(k * rows, (k + 1) * rows) for k in range(FFN_ROW_BLOCKS)]
    xs, zs = [], []
    for r in blocks:
        x = x_ref[r, :]
        h = _norm_mod(x, g_ref[...], shift, scale)
        xs.append(x)
        zs.append(jnp.dot(h.astype(BF16), win_ref[...], preferred_element_type=F32))
    ys = []
    for x, z in zip(xs, zs):
        gate, up = z[:, :dff], z[:, dff:]
        act = gate * jax.nn.sigmoid(gate) * up
        out = jnp.dot(act.astype(BF16), wout_ref[...], preferred_element_type=F32)
        ys.append(x + gate_mod * out)
    if final:
        for r, y in zip(blocks, ys):
            ms = jnp.mean(y * y, axis=-1, keepdims=True)
            o_ref[r, :] = y * lax.rsqrt(ms + RMS_EPS) * fg_ref[...]
    else:
        for r, y in zip(blocks, ys):
            o_ref[r, :] = y
        _emit_ssm_inputs(jnp.concatenate(ys, axis=0), row, is_ctx, modn_ref, gn_ref, wu_ref, bu_ref, ul_ref, uc_ref)


def _ffn(xs, params, nxt, bsz, seq):
    d = xs.shape[-1]
    tiles_per_seq = seq // TOK_TILE
    arrays, specs = params
    return pl.pallas_call(
        functools.partial(_ffn_kernel, tiles_per_seq, False),
        grid=(1 + bsz * tiles_per_seq,),
        in_specs=[_stream_spec(d)] + specs + nxt[1],
        out_specs=[_stream_spec(d)] + _chunk_specs(nxt[2], tiles_per_seq),
        out_shape=[_stream_shape(bsz, seq, d)] + _ssm_input_shapes(bsz, seq, nxt[2]),
        compiler_params=pltpu.CompilerParams(
            dimension_semantics=("arbitrary",), vmem_limit_bytes=VMEM_LIMIT_BYTES),
        name="ffn",
    )(xs, *arrays, *nxt[0])


def _ffn_final(xs, params, final_g, bsz, seq):
    d = xs.shape[-1]
    tiles_per_seq = seq // TOK_TILE
    arrays, specs = params
    return pl.pallas_call(
        functools.partial(_ffn_kernel, tiles_per_seq, True),
        grid=(bsz * tiles_per_seq,),
        in_specs=[_stream_spec(d, first_step=1)] + specs + [_layer_spec(final_g, 0)],
        out_specs=pl.BlockSpec((None, TOK_TILE, d), lambda s: (s // tiles_per_seq, s % tiles_per_seq, 0)),
        out_shape=jax.ShapeDtypeStruct((bsz, seq, d), F32),
        compiler_params=pltpu.CompilerParams(
            dimension_semantics=("arbitrary",), vmem_limit_bytes=VMEM_LIMIT_BYTES),
        name="ffn_final",
    )(xs, *arrays, final_g)


def kernel(x, c, ctx, c_ctx, w_mod, b_mod, norm1_g, norm2_g, w_in, b_in, conv_w, w_out_a, lam_re, lam_im,
           log_dt, b_re, b_im, c_re, c_im, d_skip, w_glu, b_glu, w_o, w_ff_in, w_ff_out, final_g):
    bsz, seq, d = x.shape
    depth = w_mod.shape[0]
    ctx_len = ctx.shape[1]
    d_conv = conv_w.shape[-1]
    d_ssm = d_skip.shape[-1]
    u_lo, u_hi = 3 * d_conv, 3 * d_conv + d_ssm
    assert bsz * ctx_len == TOK_TILE and TOK_TILE % GRID_W == 0 and bsz <= 2
    assert ctx_len == SSM_CHUNK * SUBLANES and d_ssm % LANES == 0 and conv_w.shape[1] == CONV_K
    assert seq % (SUBLANES * TOK_TILE) == 0
    seg_steps = seq // (SSM_CHUNK * SUBLANES)

    cvec = jnp.zeros((8, d), F32).at[:bsz].set(c.astype(F32)).at[2].set(c_ctx.astype(F32))
    mod = _mod_all(cvec, w_mod, b_mod)

    mt_op, wt_op, vt_op, a_op = _ssm_operators(lam_re, lam_im, log_dt, b_re, b_im, c_re, c_im, d_skip,
                                               seg_steps)

    w_in, w_out_a, w_glu, w_o = (w.astype(BF16) for w in (w_in, w_out_a, w_glu, w_o))
    w_ff_in, w_ff_out = w_ff_in.astype(BF16), w_ff_out.astype(BF16)
    row = lambda v: v.reshape(v.shape[0], 1, v.shape[-1])
    b_in, b_glu, g1, g2, fg = row(b_in), row(b_glu), row(norm1_g), row(norm2_g), final_g.reshape(1, 1, d)
    gates_lo = u_hi

    def operands(l, *items):
        arrays = [arr for arr, _ in items]
        return arrays, [_layer_spec(arr, l, cols) for arr, cols in items]

    u_cols = (u_lo, d_ssm)
    ssm_in = lambda l: operands(l, (mod, None), (g1, None), (w_in, u_cols), (b_in, u_cols)) + (d_ssm,)
    mixer_in = lambda l: operands(
        l, (mod, None), (g1, None), (w_in, (0, u_lo)), (b_in, (0, u_lo)),
        (w_in, (gates_lo, 2 * d)), (b_in, (gates_lo, 2 * d)), (conv_w, None), (w_out_a, None),
        (w_glu, None), (b_glu, None), (w_o, None))
    ffn_in = lambda l: operands(l, (mod, None), (g2, None), (w_ff_in, None), (w_ff_out, None))

    tokens = (x, ctx.reshape(bsz * ctx_len, d))
    u_lat, u_ctx = _uproj(*tokens, ssm_in(0))
    for l in range(depth):
        ys_lat, ys_ctx = _ssm(u_lat, u_ctx, mt_op, wt_op, vt_op, a_op, l)
        xs = _mixer(tokens, ys_lat, ys_ctx, mixer_in(l), bsz, seq, ctx_len)
        if l + 1 < depth:
            xs, u_lat, u_ctx = _ffn(xs, ffn_in(l), ssm_in(l + 1), bsz, seq)
            tokens = (xs,)
    return _ffn_final(xs, ffn_in(depth - 1), fg, bsz, seq)
```

```python
import functools

import numpy as np
import jax
import jax.numpy as jnp
from jax import lax
from jax.experimental import pallas as pl
from jax.experimental.pallas import tpu as pltpu

GRID_W = 64
CONV_K = 3
SSM_GROUP = 16
STATE = 64
N_DIR = 2
RMS_EPS = 1e-6

TOK_TILE = 512
SSM_CHUNK = 32
CHUNK_PITCH = 40
SUBLANES = 8
LANES = 128
SSM_LANE_GROUPS = LANES // SSM_GROUP
SSM_SCAN_GROUPS = 4
SSM_IO_PARTS = 2
MIXER_ROW_BLOCKS = 2
FFN_ROW_BLOCKS = 2
VMEM_LIMIT_BYTES = 56 * 1024 * 1024

F32 = jnp.float32
BF16 = jnp.bfloat16


def _layer_spec(arr, l, cols=None):
    _, rows, n = arr.shape
    start, width = (0, n) if cols is None else cols
    assert start % width == 0
    return pl.BlockSpec((None, rows, width), lambda *_: (l, 0, start // width), pipeline_mode=pl.Buffered(1))


def _tile_coords(s, tiles_per_seq):
    k = jnp.maximum(s - 1, 0)
    return k // tiles_per_seq, k % tiles_per_seq


def _input_token_specs(d, tiles_per_seq):
    return [pl.BlockSpec((None, TOK_TILE, d), lambda s: (*_tile_coords(s, tiles_per_seq), 0)),
            pl.BlockSpec((TOK_TILE, d), lambda s: (0, 0))]


def _stream_spec(d, first_step=0):
    return pl.BlockSpec((None, TOK_TILE, d), lambda s: (s + first_step, 0, 0))


def _stream_shape(bsz, seq, d):
    return jax.ShapeDtypeStruct((1 + bsz * (seq // TOK_TILE), TOK_TILE, d), F32)


def _chunk_specs(n, tiles_per_seq):
    cpt = TOK_TILE // SSM_CHUNK
    tiles_per_seg = tiles_per_seq // SUBLANES

    def index(s):
        b, i = _tile_coords(s, tiles_per_seq)
        return (b, i % tiles_per_seg, i // tiles_per_seg, 0, 0)

    return [pl.BlockSpec((None, cpt, None, CHUNK_PITCH, n), index),
            pl.BlockSpec((TOK_TILE, n), lambda s: (0, 0))]


def _tile_kind(tiles_per_seq, first_step=0):
    s = pl.program_id(0) + first_step
    is_ctx = s == 0
    return is_ctx, jnp.where(is_ctx, 2, _tile_coords(s, tiles_per_seq)[0])


def _norm_mod(x, g, shift, scale):
    ms = jnp.mean(x * x, axis=-1, keepdims=True)
    y = x * lax.rsqrt(ms + RMS_EPS) * g
    return y * (1.0 + scale) + shift


def _mod_vec(mod_ref, row, k, d):
    return mod_ref[pl.ds(row, 1), k * d:(k + 1) * d]


def _mod_kernel(c_ref, w_ref, b_ref, o_ref):
    c = c_ref[...]
    s = c * jax.nn.sigmoid(c)
    o_ref[...] = jnp.dot(s.astype(BF16), w_ref[...].astype(BF16),
                         preferred_element_type=F32) + b_ref[...]


def _mod_all(cvec, w_mod, b_mod):
    depth, d, n = w_mod.shape
    tn = 1536
    return pl.pallas_call(
        _mod_kernel,
        grid=(depth, n // tn),
        in_specs=[pl.BlockSpec((8, d), lambda l, j: (0, 0)),
                  pl.BlockSpec((None, d, tn), lambda l, j: (l, 0, j)),
                  pl.BlockSpec((None, 1, tn), lambda l, j: (l, 0, j))],
        out_specs=pl.BlockSpec((None, 8, tn), lambda l, j: (l, 0, j)),
        out_shape=jax.ShapeDtypeStruct((depth, 8, n), F32),
        compiler_params=pltpu.CompilerParams(
            dimension_semantics=("arbitrary", "arbitrary"), vmem_limit_bytes=VMEM_LIMIT_BYTES),
        name="adaln_mod",
    )(cvec, w_mod, b_mod.reshape(depth, 1, n))


def _emit_ssm_inputs(x, row, is_ctx, mod_ref, g_ref, w_ref, b_ref, ul_ref, uc_ref):
    d = x.shape[-1]
    h = _norm_mod(x, g_ref[...], _mod_vec(mod_ref, row, 0, d), _mod_vec(mod_ref, row, 1, d))
    u = jnp.dot(h.astype(BF16), w_ref[...], preferred_element_type=F32) + b_ref[...]
    cpt, pitch, n = ul_ref.shape
    ul_ref[:, :SSM_CHUNK, :] = u.reshape(cpt, SSM_CHUNK, n)
    ul_ref[:, SSM_CHUNK:, :] = jnp.zeros((cpt, pitch - SSM_CHUNK, n), F32)

    @pl.when(is_ctx)
    def _():
        uc_ref[...] = u


def _uproj_kernel(tiles_per_seq, xl_ref, xc_ref, mod_ref, g_ref, w_ref, b_ref, ul_ref, uc_ref):
    is_ctx, row = _tile_kind(tiles_per_seq)
    x = jnp.where(is_ctx, xc_ref[...], xl_ref[...])
    _emit_ssm_inputs(x, row, is_ctx, mod_ref, g_ref, w_ref, b_ref, ul_ref, uc_ref)


def _ssm_input_shapes(bsz, seq, n):
    steps = seq // (SSM_CHUNK * SUBLANES)
    return [jax.ShapeDtypeStruct((bsz, steps, SUBLANES, CHUNK_PITCH, n), F32),
            jax.ShapeDtypeStruct((TOK_TILE, n), F32)]


def _uproj(xl, xc, ssm_in):
    bsz, seq, d = xl.shape
    tiles_per_seq = seq // TOK_TILE
    arrays, specs, n_u = ssm_in
    return pl.pallas_call(
        functools.partial(_uproj_kernel, tiles_per_seq),
        grid=(1 + bsz * tiles_per_seq,),
        in_specs=_input_token_specs(d, tiles_per_seq) + specs,
        out_specs=_chunk_specs(n_u, tiles_per_seq),
        out_shape=_ssm_input_shapes(bsz, seq, n_u),
        compiler_params=pltpu.CompilerParams(
            dimension_semantics=("arbitrary",), vmem_limit_bytes=VMEM_LIMIT_BYTES),
        name="ssm_uproj",
    )(xl, xc, *arrays)


def _cmul(a_re, a_im, b_re, b_im):
    return a_re * b_re - a_im * b_im, a_re * b_im + a_im * b_re


def _ssm_kernel(ctx_chunks, ul_ref, uc_ref, mt_ref, wt_ref, vt_ref, a_ref, yl_ref, yc_ref,
                ut_scr, yt_scr, x_scr, hf_scr, hb_scr):
    L = SSM_CHUNK
    H = SSM_GROUP
    parts = SSM_IO_PARTS
    n_part = ul_ref.shape[0] // CHUNK_PITCH
    n_lat = n_part * parts
    n_all = ut_scr.shape[-1]
    phase = pl.program_id(2)
    trow = lambda t: pl.ds(pl.multiple_of(t * H, H), H)

    def fill(cols, rows_of_t, unroll):
        def body(t, carry):
            at = rows_of_t(t).T
            for g in range(SSM_LANE_GROUPS):
                ut_scr[g, trow(t), cols] = at[g * H:(g + 1) * H, :].astype(BF16)
            return carry
        lax.fori_loop(0, L, body, 0, unroll=unroll)

    for q in range(parts):
        @pl.when(phase == q)
        def _():
            fill(slice(q * n_part, (q + 1) * n_part), lambda t: ul_ref[pl.ds(t, n_part, stride=CHUNK_PITCH), :], 16)

    @pl.when(phase == parts)
    def _():
        pad = jnp.zeros((n_all - n_lat - ctx_chunks, LANES), F32)
        fill(slice(n_lat, n_all),
             lambda t: jnp.concatenate([uc_ref[pl.ds(t, ctx_chunks, stride=L), :], pad], axis=0), True)
        _ssm_compute(ctx_chunks, n_lat, mt_ref, wt_ref, vt_ref, a_ref, ut_scr, yt_scr, x_scr, hf_scr, hb_scr)

        def drain_ctx(t, carry):
            y = yt_scr[t, :, n_lat:].astype(F32).T
            yc_ref[pl.ds(t, ctx_chunks, stride=L), :] = y[:ctx_chunks]
            return carry
        lax.fori_loop(0, L, drain_ctx, 0, unroll=True)

    for q in range(parts):
        @pl.when(phase == parts + 1 + q)
        def _():
            def drain(t, carry):
                y = yt_scr[t, :, q * n_part:(q + 1) * n_part].astype(F32).T
                yl_ref[pl.ds(t, n_part, stride=CHUNK_PITCH), :] = y
                return carry
            lax.fori_loop(0, L, drain, 0, unroll=16)
            for t in range(L, CHUNK_PITCH):
                yl_ref[pl.ds(t, n_part, stride=CHUNK_PITCH), :] = jnp.zeros((n_part, LANES), F32)


def _ssm_compute(ctx_chunks, n_lat, mt_ref, wt_ref, vt_ref, a_ref, ut_scr, yt_scr, x_scr, hf_scr, hb_scr):
    L = SSM_CHUNK
    H = SSM_GROUP
    sub = SUBLANES
    n_all = ut_scr.shape[-1]
    steps = n_lat // sub
    half = STATE * N_DIR
    gl = SSM_LANE_GROUPS
    gs = SSM_SCAN_GROUPS

    is_fwd = lax.broadcasted_iota(jnp.int32, (sub, half), 1) < STATE
    is_fwd1 = is_fwd[:1]
    sl = lax.broadcasted_iota(jnp.int32, (sub, half), 0)
    blk = lambda j: pl.ds(pl.multiple_of(j * sub, sub), sub)
    zero = jnp.zeros((sub, half), F32)
    lane2 = lax.broadcasted_iota(jnp.int32, (n_all, 2 * half), 1)
    take_fwd = (lane2 % half) < STATE
    tail0 = n_lat + ctx_chunks
    assert tail0 % sub == 0 and ctx_chunks == sub

    for g0 in range(0, gl, gs):
        coef = lambda g, r, rows=sub: jnp.broadcast_to(a_ref[g0 + g, r:r + 1, :], (rows, half))
        a_re = [coef(g, 0) for g in range(gs)]
        a_im = [coef(g, 1) for g in range(gs)]

        for g in range(gs):
            xt = jnp.dot(wt_ref[g0 + g], ut_scr[g0 + g], preferred_element_type=F32)
            x_scr[g] = xt.T
            hf_scr[g, tail0:, :] = jnp.zeros((n_all - tail0, 2 * half), F32)
            hb_scr[g, tail0:, :] = jnp.zeros((n_all - tail0, 2 * half), F32)

        h0 = []
        for g in range(gs):
            xc = x_scr[g, n_lat:n_lat + ctx_chunks, :]
            ar, ai = a_re[g][:1], a_im[g][:1]
            h_re = h_im = jnp.zeros((1, half), F32)
            hf_rows, hb_rows = [], [None] * ctx_chunks
            for k in range(ctx_chunks):
                kb = ctx_chunks - 1 - k
                hf_rows.append((h_re, h_im))
                hb_rows[kb] = (h_re, h_im)
                x_re = jnp.where(is_fwd1, xc[k:k + 1, :half], xc[kb:kb + 1, :half])
                x_im = jnp.where(is_fwd1, xc[k:k + 1, half:], xc[kb:kb + 1, half:])
                p_re, p_im = _cmul(ar, ai, h_re, h_im)
                h_re, h_im = p_re + x_re, p_im + x_im
            cat = lambda rows, part: jnp.concatenate([r[part] for r in rows], axis=0)
            hf_scr[g, n_lat:n_lat + ctx_chunks, :half] = cat(hf_rows, 0)
            hf_scr[g, n_lat:n_lat + ctx_chunks, half:] = cat(hf_rows, 1)
            hb_scr[g, n_lat:n_lat + ctx_chunks, :half] = cat(hb_rows, 0)
            hb_scr[g, n_lat:n_lat + ctx_chunks, half:] = cat(hb_rows, 1)
            h0.append((jnp.broadcast_to(h_re, (sub, half)), jnp.broadcast_to(h_im, (sub, half))))

        def scan_step(j, carry):
            jb = steps - 1 - j
            new = []
            for g in range(gs):
                h_re, h_im = carry[g]
                x_re = jnp.where(is_fwd, x_scr[g, blk(j), :half], x_scr[g, blk(jb), :half])
                x_im = jnp.where(is_fwd, x_scr[g, blk(j), half:], x_scr[g, blk(jb), half:])
                hf_scr[g, blk(j), :half] = h_re
                hf_scr[g, blk(j), half:] = h_im
                hb_scr[g, blk(jb), :half] = h_re
                hb_scr[g, blk(jb), half:] = h_im
                p_re, p_im = _cmul(a_re[g], a_im[g], h_re, h_im)
                new.append((p_re + x_re, p_im + x_im))
            return tuple(new)

        ends = lax.fori_loop(0, steps, scan_step, tuple((zero, zero) for _ in range(gs)), unroll=True)

        carries = []
        for g in range(gs):
            def shift(t, h):
                return jnp.where(is_fwd, jnp.where(sl < 1, h, pltpu.roll(t, 1, 0)),
                                 jnp.where(sl >= sub - 1, h, pltpu.roll(t, sub - 1, 0)))
            e_re, e_im = ends[g]
            s_re, s_im = coef(g, 2), coef(g, 3)
            c_re, c_im = shift(zero, h0[g][0]), shift(zero, h0[g][1])
            for _ in range(sub - 1):
                p_re, p_im = _cmul(s_re, s_im, c_re, c_im)
                c_re, c_im = shift(e_re + p_re, h0[g][0]), shift(e_im + p_im, h0[g][1])
            carries.append((c_re, c_im))

        def fix_step(j, carry):
            jb = steps - 1 - j
            new = []
            for g in range(gs):
                d_re, d_im = carry[g]
                hf_scr[g, blk(j), :half] += d_re
                hf_scr[g, blk(j), half:] += d_im
                hb_scr[g, blk(jb), :half] += d_re
                hb_scr[g, blk(jb), half:] += d_im
                new.append(_cmul(a_re[g], a_im[g], d_re, d_im))
            return tuple(new)

        lax.fori_loop(0, steps, fix_step, tuple(carries), unroll=True)

        for g in range(gs):
            h_in = jnp.where(take_fwd, hf_scr[g], hb_scr[g]).T.astype(BF16)
            yt = (jnp.dot(mt_ref[g0 + g], ut_scr[g0 + g], preferred_element_type=F32)
                  + jnp.dot(vt_ref[g0 + g], h_in, preferred_element_type=F32))
            for t in range(L):
                yt_scr[t, (g0 + g) * H:(g0 + g + 1) * H, :] = yt[t * H:(t + 1) * H, :].astype(BF16)


def _ssm(u_lat, u_ctx, mt, wt, vt, a, l):
    bsz, steps, nseg, pitch, n = u_lat.shape
    L = SSM_CHUNK
    ctx_len = u_ctx.shape[0] // bsz
    ctx_chunks = ctx_len // L
    n_lat = steps * nseg
    n_all = -(-(n_lat + ctx_chunks) // LANES) * LANES
    gl = SSM_LANE_GROUPS
    f = L * SSM_GROUP
    ns = wt.shape[-2]
    parts = SSM_IO_PARTS
    n_blocks = n // LANES
    part_rows = n_lat * pitch // parts
    lat_in = pl.BlockSpec((None, part_rows, LANES), lambda b, j, p: (b, jnp.minimum(p, parts - 1), j))
    lat_out = pl.BlockSpec((None, part_rows, LANES), lambda b, j, p: (b, jnp.maximum(p - parts - 1, 0), j))
    ctx_spec = pl.BlockSpec((None, ctx_len, LANES), lambda b, j, p: (b, 0, j))
    grp = lambda shape: pl.BlockSpec(
        (None, gl) + shape, lambda b, j, p: (l, jnp.where(p > parts, (j + 1) % n_blocks, j), 0, 0))
    y_lat, y_ctx = pl.pallas_call(
        functools.partial(_ssm_kernel, ctx_chunks),
        grid=(bsz, n_blocks, 2 * parts + 1),
        in_specs=[lat_in, ctx_spec, grp((f, f)), grp((ns, f)), grp((f, ns)), grp((4, ns // 2))],
        out_specs=[lat_out, ctx_spec],
        out_shape=[jax.ShapeDtypeStruct((bsz, n_lat * pitch, n), F32),
                   jax.ShapeDtypeStruct((bsz, ctx_len, n), F32)],
        scratch_shapes=[pltpu.VMEM((gl, f, n_all), BF16), pltpu.VMEM((L, LANES, n_all), BF16)]
        + [pltpu.VMEM((SSM_SCAN_GROUPS, n_all, ns), F32)] * 3,
        compiler_params=pltpu.CompilerParams(
            dimension_semantics=("arbitrary", "arbitrary", "arbitrary"), vmem_limit_bytes=VMEM_LIMIT_BYTES),
        name="ssm_chunked",
    )(u_lat.reshape(bsz, n_lat * pitch, n), u_ctx.reshape(bsz, ctx_len, n), mt, wt, vt, a)
    return y_lat.reshape(u_lat.shape), y_ctx.reshape(u_ctx.shape)


def _cpow(z_re, z_im, n):
    out = None
    while n:
        if n & 1:
            out = (z_re, z_im) if out is None else _cmul(out[0], out[1], z_re, z_im)
        n >>= 1
        if n:
            z_re, z_im = _cmul(z_re, z_im, z_re, z_im)
    return out


def _ops_kernel(seg_steps, pw_ref, pv_ref, pa_ref, bbt_ref, c_ref, csel_ref, dcol_ref,
                mt_ref, wt_ref, vt_ref, a_ref, w_scr):
    gb = mt_ref.shape[0]
    f = mt_ref.shape[-1]
    H = SSM_GROUP
    L = f // H
    half = STATE * N_DIR
    lane = lax.broadcasted_iota(jnp.int32, (H, 2 * f), 1)
    row = lax.broadcasted_iota(jnp.int32, (H, 2 * f), 0)
    centre = lane == (L - 1) * H + row
    zeros = jnp.zeros((H, f), F32)
    for g in range(gb):
        bt_re, bt_im = bbt_ref[g, 0], bbt_ref[g, 1]
        c_re, c_im = c_ref[g, 0], c_ref[g, 1]
        for t in range(L):
            rows = slice(t * H, (t + 1) * H)
            w_re, w_im = _cmul(pw_ref[g, 0, t:t + 1, :], pw_ref[g, 1, t:t + 1, :], bt_re, bt_im)
            w_scr[rows, :half] = w_re
            w_scr[rows, half:] = w_im
            g_re, g_im = _cmul(pv_ref[g, 0, t:t + 1, :], pv_ref[g, 1, t:t + 1, :], c_re, c_im)
            vt_ref[g, rows, :half] = g_re.astype(vt_ref.dtype)
            vt_ref[g, rows, half:] = (-g_im).astype(vt_ref.dtype)
        wt = w_scr[...].T
        wt_ref[g] = wt.astype(wt_ref.dtype)
        kk = jnp.dot(csel_ref[g], wt, preferred_element_type=F32, precision=lax.Precision.HIGHEST)
        kf = jnp.concatenate([kk[:H], zeros], axis=1)
        kb = pltpu.roll(jnp.concatenate([kk[H:], zeros], axis=1), (L - 1) * H, 1)
        dmat = jnp.concatenate([dcol_ref[g]] * (2 * f // LANES), axis=1)
        k = kf + kb + jnp.where(centre, dmat, 0.0)
        for t in range(L):
            off = f - (t + 1) * H
            win = k if off == 0 else pltpu.roll(k, 2 * f - off, 1)
            mt_ref[g, t * H:(t + 1) * H, :] = win[:, :f].astype(mt_ref.dtype)
        ac_re, ac_im = pa_ref[g, 0:1, :], pa_ref[g, 1:2, :]
        as_re, as_im = _cpow(ac_re, ac_im, seg_steps)
        a_ref[g, 0:1, :] = ac_re
        a_ref[g, 1:2, :] = ac_im
        a_ref[g, 2:3, :] = as_re
        a_ref[g, 3:4, :] = as_im


def _ssm_operators(lam_re, lam_im, log_dt, b_re, b_im, c_re, c_im, d_skip, seg_steps):
    L = SSM_CHUNK
    H = SSM_GROUP
    lam_re = lam_re.astype(F32)
    lam_im = lam_im.astype(F32)
    depth, _, ng, _ = lam_re.shape
    dt = jnp.exp(log_dt.astype(F32))[..., None]
    mag = jnp.exp(lam_re * dt)
    a_re = mag * jnp.cos(lam_im * dt)
    a_im = mag * jnp.sin(lam_im * dt)
    nr, ni = a_re - 1.0, a_im
    den = lam_re * lam_re + lam_im * lam_im
    f_re = (nr * lam_re + ni * lam_im) / den
    f_im = (ni * lam_re - nr * lam_im) / den
    br, bi = b_re.astype(F32), b_im.astype(F32)
    bb_re = f_re[..., None] * br - f_im[..., None] * bi
    bb_im = f_re[..., None] * bi + f_im[..., None] * br
    k = jnp.arange(L + 1, dtype=F32)[:, None]
    pmag = jnp.exp(lam_re[..., None, :] * dt[..., None, :] * k)
    parg = lam_im[..., None, :] * dt[..., None, :] * k
    p_re = pmag * jnp.cos(parg)
    p_im = pmag * jnp.sin(parg)
    both = lambda fwd, bwd: jnp.concatenate([fwd, bwd], axis=-1)
    reim = lambda re, im: jnp.stack([re, im], axis=2)
    pw = reim(both(p_re[:, 0, :, L - 1::-1][:, :, :L], p_re[:, 1, :, :L]),
              both(p_im[:, 0, :, L - 1::-1][:, :, :L], p_im[:, 1, :, :L]))
    pv = reim(both(p_re[:, 0, :, 1:], p_re[:, 1, :, :0:-1]), both(p_im[:, 0, :, 1:], p_im[:, 1, :, :0:-1]))
    pa = jnp.stack([both(p_re[:, 0, :, L], p_re[:, 1, :, L]), both(p_im[:, 0, :, L], p_im[:, 1, :, L])], axis=2)
    sw = lambda z: jnp.swapaxes(z, -1, -2)
    bbt = reim(both(sw(bb_re[:, 0]), sw(bb_re[:, 1])), both(sw(bb_im[:, 0]), sw(bb_im[:, 1])))
    cr, ci = c_re.astype(F32), c_im.astype(F32)
    cc = reim(both(cr[:, 0], cr[:, 1]), both(ci[:, 0], ci[:, 1]))
    z = jnp.zeros_like(cr[:, 0])
    csel = jnp.concatenate([jnp.concatenate([cr[:, 0], z, -ci[:, 0], z], axis=-1),
                            jnp.concatenate([z, cr[:, 1], z, -ci[:, 1]], axis=-1)], axis=-2)
    dcol = jnp.broadcast_to(d_skip.astype(F32).reshape(depth, ng, H, 1), (depth, ng, H, LANES))
    f = L * H
    ns = 2 * N_DIR * STATE
    gb = 8
    blk = lambda *shape: pl.BlockSpec((None, gb) + shape, lambda l, j: (l, j) + (0,) * len(shape))
    return pl.pallas_call(
        functools.partial(_ops_kernel, seg_steps),
        grid=(depth, ng // gb),
        in_specs=[blk(2, L, ns // 2), blk(2, L, ns // 2), blk(2, ns // 2), blk(2, H, ns // 2),
                  blk(2, H, ns // 2), blk(2 * H, ns), blk(H, LANES)],
        out_specs=[blk(f, f), blk(ns, f), blk(f, ns), blk(4, ns // 2)],
        out_shape=[jax.ShapeDtypeStruct((depth, ng, f, f), BF16), jax.ShapeDtypeStruct((depth, ng, ns, f), BF16),
                   jax.ShapeDtypeStruct((depth, ng, f, ns), BF16), jax.ShapeDtypeStruct((depth, ng, 4, ns // 2), F32)],
        scratch_shapes=[pltpu.VMEM((f, ns), F32)],
        compiler_params=pltpu.CompilerParams(
            dimension_semantics=("parallel", "parallel"), vmem_limit_bytes=VMEM_LIMIT_BYTES),
        name="ssm_operators",
    )(pw, pv, pa, bbt, cc, csel, dcol)


def _gelu_tanh(x):
    return 0.5 * x * (1.0 + jnp.tanh(np.sqrt(2.0 / np.pi).astype(np.float32)
                                     * (x + np.float32(0.044715) * (x * x * x))))


def _mixer_kernel(tiles_per_seq, ctx_len, first, x_ref, *refs):
    if first:
        xc_ref, *refs = refs
    (ysl_ref, ysc_ref, mod_ref, g_ref, wa_ref, ba_ref, wg_ref, bg_ref, cw_ref,
     woa_ref, wglu_ref, bglu_ref, wo_ref, o_ref) = refs
    tm, d = x_ref.shape
    dc = woa_ref.shape[0]
    n = ysc_ref.shape[-1]
    is_ctx, row = _tile_kind(tiles_per_seq)
    shift, scale, gate = (_mod_vec(mod_ref, row, k, d) for k in range(3))
    nb = MIXER_ROW_BLOCKS
    rows = tm // nb
    assert rows % ctx_len == 0 and rows % SSM_CHUNK == 0
    t = lax.broadcasted_iota(jnp.int32, (rows, 1), 0)
    seg = jnp.where(is_ctx, ctx_len - 1, GRID_W - 1)
    pos = t & seg
    blocks = [slice(k * rows, (k + 1) * rows) for k in range(nb)]
    xs, za, zg = [], [], []
    for k, r in enumerate(blocks):
        x = jnp.where(is_ctx, xc_ref[r, :], x_ref[r, :]) if first else x_ref[r, :]
        h = _norm_mod(x, g_ref[...], shift, scale).astype(BF16)
        xs.append(x)
        za.append(jnp.dot(h, wa_ref[...], preferred_element_type=F32) + ba_ref[...])
        zg.append(jnp.dot(h, wg_ref[...], preferred_element_type=F32) + bg_ref[...])
    y_a, y_b = [], []
    for k, r in enumerate(blocks):
        g_b, g_c, x_in = za[k][:, :dc], za[k][:, dc:2 * dc], za[k][:, 2 * dc:]
        v = g_c * x_in
        v_prev = jnp.where(pos == 0, 0.0, pltpu.roll(v, 1, 0))
        v_next = jnp.where(pos == seg, 0.0, pltpu.roll(v, rows - 1, 0))
        cv = cw_ref[0:1, :] * v_prev + cw_ref[1:2, :] * v + cw_ref[2:3, :] * v_next
        y_a.append(jnp.dot((g_b * cv).astype(BF16), woa_ref[...], preferred_element_type=F32))
        ys = ysl_ref[k * rows // SSM_CHUNK:(k + 1) * rows // SSM_CHUNK, :SSM_CHUNK, :].reshape(rows, n)
        s = _gelu_tanh(jnp.where(is_ctx, ysc_ref[r, :], ys))
        gl = jnp.dot(s.astype(BF16), wglu_ref[...], preferred_element_type=F32) + bglu_ref[...]
        y_b.append(gl[:, :d] * jax.nn.sigmoid(gl[:, d:]))
    for k, r in enumerate(blocks):
        gate_a, gate_b = zg[k][:, :d], zg[k][:, d:]
        merged = jax.nn.sigmoid(gate_a) * y_a[k] + jax.nn.sigmoid(gate_b) * y_b[k]
        out = jnp.dot(merged.astype(BF16), wo_ref[...], preferred_element_type=F32)
        o_ref[r, :] = xs[k] + gate * out


def _mixer(tokens, ys_lat, ys_ctx, params, bsz, seq, ctx_len):
    first = len(tokens) == 2
    d = tokens[0].shape[-1]
    tiles_per_seq = seq // TOK_TILE
    arrays, specs = params
    return pl.pallas_call(
        functools.partial(_mixer_kernel, tiles_per_seq, ctx_len, first),
        grid=(1 + bsz * tiles_per_seq,),
        in_specs=(_input_token_specs(d, tiles_per_seq) if first else [_stream_spec(d)])
        + _chunk_specs(ys_ctx.shape[-1], tiles_per_seq) + specs,
        out_specs=_stream_spec(d),
        out_shape=_stream_shape(bsz, seq, d),
        compiler_params=pltpu.CompilerParams(
            dimension_semantics=("arbitrary",), vmem_limit_bytes=VMEM_LIMIT_BYTES),
        name="mixer",
    )(*tokens, ys_lat, ys_ctx, *arrays)


def _ffn_kernel(tiles_per_seq, final, x_ref, *refs):
    if final:
        mod_ref, g_ref, win_ref, wout_ref, fg_ref, o_ref = refs
    else:
        mod_ref, g_ref, win_ref, wout_ref, modn_ref, gn_ref, wu_ref, bu_ref, o_ref, ul_ref, uc_ref = refs
    tm, d = x_ref.shape
    dff = wout_ref.shape[0]
    is_ctx, row = _tile_kind(tiles_per_seq, first_step=1 if final else 0)
    shift, scale, gate_mod = (_mod_vec(mod_ref, row, k, d) for k in (3, 4, 5))
    rows = tm // FFN_ROW_BLOCKS
    blocks = [slice(k * rows, (k + 1) * rows) for k in range(FFN_ROW_BLOCKS)]
    xs, zs = [], []
    for r in blocks:
        x = x_ref[r, :]
        h = _norm_mod(x, g_ref[...], shift, scale)
        xs.append(x)
        zs.append(jnp.dot(h.astype(BF16), win_ref[...], preferred_element_type=F32))
    ys = []
    for x, z in zip(xs, zs):
        gate, up = z[:, :dff], z[:, dff:]
        act = gate * jax.nn.sigmoid(gate) * up
        out = jnp.dot(act.astype(BF16), wout_ref[...], preferred_element_type=F32)
        ys.append(x + gate_mod * out)
    if final:
        for r, y in zip(blocks, ys):
            ms = jnp.mean(y * y, axis=-1, keepdims=True)
            o_ref[r, :] = y * lax.rsqrt(ms + RMS_EPS) * fg_ref[...]
    else:
        for r, y in zip(blocks, ys):
            o_ref[r, :] = y
        _emit_ssm_inputs(jnp.concatenate(ys, axis=0), row, is_ctx, modn_ref, gn_ref, wu_ref, bu_ref, ul_ref, uc_ref)


def _ffn(xs, params, nxt, bsz, seq):
    d = xs.shape[-1]
    tiles_per_seq = seq // TOK_TILE
    arrays, specs = params
    return pl.pallas_call(
        functools.partial(_ffn_kernel, tiles_per_seq, False),
        grid=(1 + bsz * tiles_per_seq,),
        in_specs=[_stream_spec(d)] + specs + nxt[1],
        out_specs=[_stream_spec(d)] + _chunk_specs(nxt[2], tiles_per_seq),
        out_shape=[_stream_shape(bsz, seq, d)] + _ssm_input_shapes(bsz, seq, nxt[2]),
        compiler_params=pltpu.CompilerParams(
            dimension_semantics=("arbitrary",), vmem_limit_bytes=VMEM_LIMIT_BYTES),
        name="ffn",
    )(xs, *arrays, *nxt[0])


def _ffn_final(xs, params, final_g, bsz, seq):
    d = xs.shape[-1]
    tiles_per_seq = seq // TOK_TILE
    arrays, specs = params
    return pl.pallas_call(
        functools.partial(_ffn_kernel, tiles_per_seq, True),
        grid=(bsz * tiles_per_seq,),
        in_specs=[_stream_spec(d, first_step=1)] + specs + [_layer_spec(final_g, 0)],
        out_specs=pl.BlockSpec((None, TOK_TILE, d), lambda s: (s // tiles_per_seq, s % tiles_per_seq, 0)),
        out_shape=jax.ShapeDtypeStruct((bsz, seq, d), F32),
        compiler_params=pltpu.CompilerParams(
            dimension_semantics=("arbitrary",), vmem_limit_bytes=VMEM_LIMIT_BYTES),
        name="ffn_final",
    )(xs, *arrays, final_g)


def kernel(x, c, ctx, c_ctx, w_mod, b_mod, norm1_g, norm2_g, w_in, b_in, conv_w, w_out_a, lam_re, lam_im,
           log_dt, b_re, b_im, c_re, c_im, d_skip, w_glu, b_glu, w_o, w_ff_in, w_ff_out, final_g):
    bsz, seq, d = x.shape
    depth = w_mod.shape[0]
    ctx_len = ctx.shape[1]
    d_conv = conv_w.shape[-1]
    d_ssm = d_skip.shape[-1]
    u_lo, u_hi = 3 * d_conv, 3 * d_conv + d_ssm
    assert bsz * ctx_len == TOK_TILE and TOK_TILE % GRID_W == 0 and bsz <= 2
    assert ctx_len == SSM_CHUNK * SUBLANES and d_ssm % LANES == 0 and conv_w.shape[1] == CONV_K
    assert seq % (SUBLANES * TOK_TILE) == 0
    seg_steps = seq // (SSM_CHUNK * SUBLANES)

    cvec = jnp.zeros((8, d), F32).at[:bsz].set(c.astype(F32)).at[2].set(c_ctx.astype(F32))
    mod = _mod_all(cvec, w_mod, b_mod)

    mt_op, wt_op, vt_op, a_op = _ssm_operators(lam_re, lam_im, log_dt, b_re, b_im, c_re, c_im, d_skip,
                                               seg_steps)

    w_in, w_out_a, w_glu, w_o = (w.astype(BF16) for w in (w_in, w_out_a, w_glu, w_o))
    w_ff_in, w_ff_out = w_ff_in.astype(BF16), w_ff_out.astype(BF16)
    row = lambda v: v.reshape(v.shape[0], 1, v.shape[-1])
    b_in, b_glu, g1, g2, fg = row(b_in), row(b_glu), row(norm1_g), row(norm2_g), final_g.reshape(1, 1, d)
    gates_lo = u_hi

    def operands(l, *items):
        arrays = [arr for arr, _ in items]
        return arrays, [_layer_spec(arr, l, cols) for arr, cols in items]

    u_cols = (u_lo, d_ssm)
    ssm_in = lambda l: operands(l, (mod, None), (g1, None), (w_in, u_cols), (b_in, u_cols)) + (d_ssm,)
    mixer_in = lambda l: operands(
        l, (mod, None), (g1, None), (w_in, (0, u_lo)), (b_in, (0, u_lo)),
        (w_in, (gates_lo, 2 * d)), (b_in, (gates_lo, 2 * d)), (conv_w, None), (w_out_a, None),
        (w_glu, None), (b_glu, None), (w_o, None))
    ffn_in = lambda l: operands(l, (mod, None), (g2, None), (w_ff_in, None), (w_ff_out, None))

    tokens = (x, ctx.reshape(bsz * ctx_len, d))
    u_lat, u_ctx = _uproj(*tokens, ssm_in(0))
    for l in range(depth):
        ys_lat, ys_ctx = _ssm(u_lat, u_ctx, mt_op, wt_op, vt_op, a_op, l)
        xs = _mixer(tokens, ys_lat, ys_ctx, mixer_in(l), bsz, seq, ctx_len)
        if l + 1 < depth:
            xs, u_lat, u_ctx = _ffn(xs, ffn_in(l), ssm_in(l + 1), bsz, seq)
            tokens = (xs,)
    return _ffn_final(xs, ffn_in(depth - 1), fg, bsz, seq)
```

```python
import functools

import numpy as np
import jax
import jax.numpy as jnp
from jax import lax
from jax.experimental import pallas as pl
from jax.experimental.pallas import tpu as pltpu

GRID_W = 64
CONV_K = 3
SSM_GROUP = 16
STATE = 64
N_DIR = 2
RMS_EPS = 1e-6

TOK_TILE = 512
SSM_CHUNK = 32
CHUNK_PITCH = 40
SUBLANES = 8
LANES = 128
SSM_LANE_GROUPS = LANES // SSM_GROUP
SSM_SCAN_GROUPS = 4
SSM_IO_PARTS = 2
MIXER_ROW_BLOCKS = 2
FFN_ROW_BLOCKS = 2
VMEM_LIMIT_BYTES = 56 * 1024 * 1024

F32 = jnp.float32
BF16 = jnp.bfloat16


def _layer_spec(arr, l, cols=None):
    _, rows, n = arr.shape
    start, width = (0, n) if cols is None else cols
    assert start % width == 0
    return pl.BlockSpec((None, rows, width), lambda *_: (l, 0, start // width), pipeline_mode=pl.Buffered(1))


def _tile_coords(s, tiles_per_seq):
    k = jnp.maximum(s - 1, 0)
    return k // tiles_per_seq, k % tiles_per_seq


def _input_token_specs(d, tiles_per_seq):
    return [pl.BlockSpec((None, TOK_TILE, d), lambda s: (*_tile_coords(s, tiles_per_seq), 0)),
            pl.BlockSpec((TOK_TILE, d), lambda s: (0, 0))]


def _stream_spec(d, first_step=0):
    return pl.BlockSpec((None, TOK_TILE, d), lambda s: (s + first_step, 0, 0))


def _stream_shape(bsz, seq, d):
    return jax.ShapeDtypeStruct((1 + bsz * (seq // TOK_TILE), TOK_TILE, d), F32)


def _chunk_specs(n, tiles_per_seq):
    cpt = TOK_TILE // SSM_CHUNK
    tiles_per_seg = tiles_per_seq // SUBLANES

    def index(s):
        b, i = _tile_coords(s, tiles_per_seq)
        return (b, i % tiles_per_seg, i // tiles_per_seg, 0, 0)

    return [pl.BlockSpec((None, cpt, None, CHUNK_PITCH, n), index),
            pl.BlockSpec((TOK_TILE, n), lambda s: (0, 0))]


def _tile_kind(tiles_per_seq, first_step=0):
    s = pl.program_id(0) + first_step
    is_ctx = s == 0
    return is_ctx, jnp.where(is_ctx, 2, _tile_coords(s, tiles_per_seq)[0])


def _norm_mod(x, g, shift, scale):
    ms = jnp.mean(x * x, axis=-1, keepdims=True)
    y = x * lax.rsqrt(ms + RMS_EPS) * g
    return y * (1.0 + scale) + shift


def _mod_vec(mod_ref, row, k, d):
    return mod_ref[pl.ds(row, 1), k * d:(k + 1) * d]


def _mod_kernel(c_ref, w_ref, b_ref, o_ref):
    c = c_ref[...]
    s = c * jax.nn.sigmoid(c)
    o_ref[...] = jnp.dot(s.astype(BF16), w_ref[...].astype(BF16),
                         preferred_element_type=F32) + b_ref[...]


def _mod_all(cvec, w_mod, b_mod):
    depth, d, n = w_mod.shape
    tn = 1536
    return pl.pallas_call(
        _mod_kernel,
        grid=(depth, n // tn),
        in_specs=[pl.BlockSpec((8, d), lambda l, j: (0, 0)),
                  pl.BlockSpec((None, d, tn), lambda l, j: (l, 0, j)),
                  pl.BlockSpec((None, 1, tn), lambda l, j: (l, 0, j))],
        out_specs=pl.BlockSpec((None, 8, tn), lambda l, j: (l, 0, j)),
        out_shape=jax.ShapeDtypeStruct((depth, 8, n), F32),
        compiler_params=pltpu.CompilerParams(
            dimension_semantics=("arbitrary", "arbitrary"), vmem_limit_bytes=VMEM_LIMIT_BYTES),
        name="adaln_mod",
    )(cvec, w_mod, b_mod.reshape(depth, 1, n))


def _emit_ssm_inputs(x, row, is_ctx, mod_ref, g_ref, w_ref, b_ref, ul_ref, uc_ref):
    d = x.shape[-1]
    h = _norm_mod(x, g_ref[...], _mod_vec(mod_ref, row, 0, d), _mod_vec(mod_ref, row, 1, d))
    u = jnp.dot(h.astype(BF16), w_ref[...], preferred_element_type=F32) + b_ref[...]
    cpt, pitch, n = ul_ref.shape
    ul_ref[:, :SSM_CHUNK, :] = u.reshape(cpt, SSM_CHUNK, n)
    ul_ref[:, SSM_CHUNK:, :] = jnp.zeros((cpt, pitch - SSM_CHUNK, n), F32)

    @pl.when(is_ctx)
    def _():
        uc_ref[...] = u


def _uproj_kernel(tiles_per_seq, xl_ref, xc_ref, mod_ref, g_ref, w_ref, b_ref, ul_ref, uc_ref):
    is_ctx, row = _tile_kind(tiles_per_seq)
    x = jnp.where(is_ctx, xc_ref[...], xl_ref[...])
    _emit_ssm_inputs(x, row, is_ctx, mod_ref, g_ref, w_ref, b_ref, ul_ref, uc_ref)


def _ssm_input_shapes(bsz, seq, n):
    steps = seq // (SSM_CHUNK * SUBLANES)
    return [jax.ShapeDtypeStruct((bsz, steps, SUBLANES, CHUNK_PITCH, n), F32),
            jax.ShapeDtypeStruct((TOK_TILE, n), F32)]


def _uproj(xl, xc, ssm_in):
    bsz, seq, d = xl.shape
    tiles_per_seq = seq // TOK_TILE
    arrays, specs, n_u = ssm_in
    return pl.pallas_call(
        functools.partial(_uproj_kernel, tiles_per_seq),
        grid=(1 + bsz * tiles_per_seq,),
        in_specs=_input_token_specs(d, tiles_per_seq) + specs,
        out_specs=_chunk_specs(n_u, tiles_per_seq),
        out_shape=_ssm_input_shapes(bsz, seq, n_u),
        compiler_params=pltpu.CompilerParams(
            dimension_semantics=("arbitrary",), vmem_limit_bytes=VMEM_LIMIT_BYTES),
        name="ssm_uproj",
    )(xl, xc, *arrays)


def _cmul(a_re, a_im, b_re, b_im):
    return a_re * b_re - a_im * b_im, a_re * b_im + a_im * b_re


def _ssm_kernel(ctx_chunks, n_work, ul_ref, uc_ref, mt_ref, wt_ref, vt_ref, a_ref, yl_ref, yc_ref,
                ut_scr, yt_scr, x_scr, hf_scr, hb_scr):
    L = SSM_CHUNK
    H = SSM_GROUP
    parts = SSM_IO_PARTS
    n_part = ul_ref.shape[0] // CHUNK_PITCH
    n_lat = n_part * parts
    n_all = ut_scr.shape[-1]
    w, phase = pl.program_id(0), pl.program_id(1)
    has_prev, has_next = w > 0, w < n_work
    trow = lambda t: pl.ds(pl.multiple_of(t * H, H), H)

    def fill(cols, rows_of_t, unroll):
        def body(t, carry):
            at = rows_of_t(t).T
            for g in range(SSM_LANE_GROUPS):
                ut_scr[g, trow(t), cols] = at[g * H:(g + 1) * H, :].astype(BF16)
            return carry
        lax.fori_loop(0, L, body, 0, unroll=unroll)

    for q in range(parts):
        @pl.when(jnp.logical_and(phase == 1 + parts + q, has_next))
        def _():
            fill(slice(q * n_part, (q + 1) * n_part), lambda t: ul_ref[pl.ds(t, n_part, stride=CHUNK_PITCH), :], 16)

    @pl.when(jnp.logical_and(phase == 0, has_prev))
    def _():
        pad = jnp.zeros((n_all - n_lat - ctx_chunks, LANES), F32)
        fill(slice(n_lat, n_all),
             lambda t: jnp.concatenate([uc_ref[pl.ds(t, ctx_chunks, stride=L), :], pad], axis=0), True)
        _ssm_compute(ctx_chunks, n_lat, mt_ref, wt_ref, vt_ref, a_ref, ut_scr, yt_scr, x_scr, hf_scr, hb_scr)

        def drain_ctx(t, carry):
            y = yt_scr[t, :, n_lat:].astype(F32).T
            yc_ref[pl.ds(t, ctx_chunks, stride=L), :] = y[:ctx_chunks]
            return carry
        lax.fori_loop(0, L, drain_ctx, 0, unroll=True)

    for q in range(parts):
        @pl.when(jnp.logical_and(phase == 1 + q, has_prev))
        def _():
            def drain(t, carry):
                y = yt_scr[t, :, q * n_part:(q + 1) * n_part].astype(F32).T
                yl_ref[pl.ds(t, n_part, stride=CHUNK_PITCH), :] = y
                return carry
            lax.fori_loop(0, L, drain, 0, unroll=16)
            for t in range(L, CHUNK_PITCH):
                yl_ref[pl.ds(t, n_part, stride=CHUNK_PITCH), :] = jnp.zeros((n_part, LANES), F32)


def _ssm_compute(ctx_chunks, n_lat, mt_ref, wt_ref, vt_ref, a_ref, ut_scr, yt_scr, x_scr, hf_scr, hb_scr):
    L = SSM_CHUNK
    H = SSM_GROUP
    sub = SUBLANES
    n_all = ut_scr.shape[-1]
    steps = n_lat // sub
    half = STATE * N_DIR
    gl = SSM_LANE_GROUPS
    gs = SSM_SCAN_GROUPS

    is_fwd = lax.broadcasted_iota(jnp.int32, (sub, half), 1) < STATE
    is_fwd1 = is_fwd[:1]
    sl = lax.broadcasted_iota(jnp.int32, (sub, half), 0)
    blk = lambda j: pl.ds(pl.multiple_of(j * sub, sub), sub)
    zero = jnp.zeros((sub, half), F32)
    lane2 = lax.broadcasted_iota(jnp.int32, (n_all, 2 * half), 1)
    take_fwd = (lane2 % half) < STATE
    tail0 = n_lat + ctx_chunks
    assert tail0 % sub == 0 and ctx_chunks == sub

    for g0 in range(0, gl, gs):
        coef = lambda g, r, rows=sub: jnp.broadcast_to(a_ref[g0 + g, r:r + 1, :], (rows, half))
        a_re = [coef(g, 0) for g in range(gs)]
        a_im = [coef(g, 1) for g in range(gs)]

        for g in range(gs):
            xt = jnp.dot(wt_ref[g0 + g], ut_scr[g0 + g], preferred_element_type=F32)
            x_scr[g] = xt.T
            hf_scr[g, tail0:, :] = jnp.zeros((n_all - tail0, 2 * half), F32)
            hb_scr[g, tail0:, :] = jnp.zeros((n_all - tail0, 2 * half), F32)

        h0 = []
        for g in range(gs):
            xc = x_scr[g, n_lat:n_lat + ctx_chunks, :]
            ar, ai = a_re[g][:1], a_im[g][:1]
            h_re = h_im = jnp.zeros((1, half), F32)
            hf_rows, hb_rows = [], [None] * ctx_chunks
            for k in range(ctx_chunks):
                kb = ctx_chunks - 1 - k
                hf_rows.append((h_re, h_im))
                hb_rows[kb] = (h_re, h_im)
                x_re = jnp.where(is_fwd1, xc[k:k + 1, :half], xc[kb:kb + 1, :half])
                x_im = jnp.where(is_fwd1, xc[k:k + 1, half:], xc[kb:kb + 1, half:])
                p_re, p_im = _cmul(ar, ai, h_re, h_im)
                h_re, h_im = p_re + x_re, p_im + x_im
            cat = lambda rows, part: jnp.concatenate([r[part] for r in rows], axis=0)
            hf_scr[g, n_lat:n_lat + ctx_chunks, :half] = cat(hf_rows, 0)
            hf_scr[g, n_lat:n_lat + ctx_chunks, half:] = cat(hf_rows, 1)
            hb_scr[g, n_lat:n_lat + ctx_chunks, :half] = cat(hb_rows, 0)
            hb_scr[g, n_lat:n_lat + ctx_chunks, half:] = cat(hb_rows, 1)
            h0.append((jnp.broadcast_to(h_re, (sub, half)), jnp.broadcast_to(h_im, (sub, half))))

        def scan_step(j, carry):
            jb = steps - 1 - j
            new = []
            for g in range(gs):
                h_re, h_im = carry[g]
                x_re = jnp.where(is_fwd, x_scr[g, blk(j), :half], x_scr[g, blk(jb), :half])
                x_im = jnp.where(is_fwd, x_scr[g, blk(j), half:], x_scr[g, blk(jb), half:])
                hf_scr[g, blk(j), :half] = h_re
                hf_scr[g, blk(j), half:] = h_im
                hb_scr[g, blk(jb), :half] = h_re
                hb_scr[g, blk(jb), half:] = h_im
                p_re, p_im = _cmul(a_re[g], a_im[g], h_re, h_im)
                new.append((p_re + x_re, p_im + x_im))
            return tuple(new)

        ends = lax.fori_loop(0, steps, scan_step, tuple((zero, zero) for _ in range(gs)), unroll=True)

        carries = []
        for g in range(gs):
            def shift(t, h):
                return jnp.where(is_fwd, jnp.where(sl < 1, h, pltpu.roll(t, 1, 0)),
                                 jnp.where(sl >= sub - 1, h, pltpu.roll(t, sub - 1, 0)))
            e_re, e_im = ends[g]
            s_re, s_im = coef(g, 2), coef(g, 3)
            c_re, c_im = shift(zero, h0[g][0]), shift(zero, h0[g][1])
            for _ in range(sub - 1):
                p_re, p_im = _cmul(s_re, s_im, c_re, c_im)
                c_re, c_im = shift(e_re + p_re, h0[g][0]), shift(e_im + p_im, h0[g][1])
            carries.append((c_re, c_im))

        def fix_step(j, carry):
            jb = steps - 1 - j
            new = []
            for g in range(gs):
                d_re, d_im = carry[g]
                hf_scr[g, blk(j), :half] += d_re
                hf_scr[g, blk(j), half:] += d_im
                hb_scr[g, blk(jb), :half] += d_re
                hb_scr[g, blk(jb), half:] += d_im
                new.append(_cmul(a_re[g], a_im[g], d_re, d_im))
            return tuple(new)

        lax.fori_loop(0, steps, fix_step, tuple(carries), unroll=True)

        for g in range(gs):
            h_in = jnp.where(take_fwd, hf_scr[g], hb_scr[g]).T.astype(BF16)
            yt = (jnp.dot(mt_ref[g0 + g], ut_scr[g0 + g], preferred_element_type=F32)
                  + jnp.dot(vt_ref[g0 + g], h_in, preferred_element_type=F32))
            for t in range(L):
                yt_scr[t, (g0 + g) * H:(g0 + g + 1) * H, :] = yt[t * H:(t + 1) * H, :].astype(BF16)


def _ssm(u_lat, u_ctx, mt, wt, vt, a, l):
    bsz, steps, nseg, pitch, n = u_lat.shape
    L = SSM_CHUNK
    ctx_len = u_ctx.shape[0] // bsz
    ctx_chunks = ctx_len // L
    n_lat = steps * nseg
    n_all = -(-(n_lat + ctx_chunks) // LANES) * LANES
    gl = SSM_LANE_GROUPS
    f = L * SSM_GROUP
    ns = wt.shape[-2]
    parts = SSM_IO_PARTS
    n_blocks = n // LANES
    part_rows = n_lat * pitch // parts
    n_work = bsz * n_blocks
    item = lambda w: (jnp.clip(w, 0, n_work - 1) // n_blocks, jnp.clip(w, 0, n_work - 1) % n_blocks)

    def lat_in_index(w, p):
        b, j = item(jnp.where(p == 0, w - 1, w))
        return (b, jnp.where(p == 0, parts - 1, jnp.maximum(p - parts - 1, 0)), j)

    def lat_out_index(w, p):
        b, j = item(w - 1)
        return (b, jnp.where(w == 0, 0, jnp.clip(p - 1, 0, parts - 1)), j)

    def ctx_index(w, p):
        b, j = item(w - 1)
        return (b, 0, j)

    def op_index(w, p):
        return (l, item(jnp.where(p == 0, w - 1, w))[1], 0, 0)

    lat_in = pl.BlockSpec((None, part_rows, LANES), lat_in_index)
    lat_out = pl.BlockSpec((None, part_rows, LANES), lat_out_index)
    ctx_spec = pl.BlockSpec((None, ctx_len, LANES), ctx_index)
    grp = lambda shape: pl.BlockSpec((None, gl) + shape, op_index)
    y_lat, y_ctx = pl.pallas_call(
        functools.partial(_ssm_kernel, ctx_chunks, n_work),
        grid=(n_work + 1, 2 * parts + 1),
        in_specs=[lat_in, ctx_spec, grp((f, f)), grp((ns, f)), grp((f, ns)), grp((4, ns // 2))],
        out_specs=[lat_out, ctx_spec],
        out_shape=[jax.ShapeDtypeStruct((bsz, n_lat * pitch, n), F32),
                   jax.ShapeDtypeStruct((bsz, ctx_len, n), F32)],
        scratch_shapes=[pltpu.VMEM((gl, f, n_all), BF16), pltpu.VMEM((L, LANES, n_all), BF16)]
        + [pltpu.VMEM((SSM_SCAN_GROUPS, n_all, ns), F32)] * 3,
        compiler_params=pltpu.CompilerParams(
            dimension_semantics=("arbitrary", "arbitrary"), vmem_limit_bytes=VMEM_LIMIT_BYTES),
        name="ssm_chunked",
    )(u_lat.reshape(bsz, n_lat * pitch, n), u_ctx.reshape(bsz, ctx_len, n), mt, wt, vt, a)
    return y_lat.reshape(u_lat.shape), y_ctx.reshape(u_ctx.shape)


def _cpow(z_re, z_im, n):
    out = None
    while n:
        if n & 1:
            out = (z_re, z_im) if out is None else _cmul(out[0], out[1], z_re, z_im)
        n >>= 1
        if n:
            z_re, z_im = _cmul(z_re, z_im, z_re, z_im)
    return out


def _ops_kernel(seg_steps, pw_ref, pv_ref, pa_ref, bbt_ref, c_ref, csel_ref, dcol_ref,
                mt_ref, wt_ref, vt_ref, a_ref, w_scr):
    gb = mt_ref.shape[0]
    f = mt_ref.shape[-1]
    H = SSM_GROUP
    L = f // H
    half = STATE * N_DIR
    lane = lax.broadcasted_iota(jnp.int32, (H, 2 * f), 1)
    row = lax.broadcasted_iota(jnp.int32, (H, 2 * f), 0)
    centre = lane == (L - 1) * H + row
    zeros = jnp.zeros((H, f), F32)
    for g in range(gb):
        bt_re, bt_im = bbt_ref[g, 0], bbt_ref[g, 1]
        c_re, c_im = c_ref[g, 0], c_ref[g, 1]
        for t in range(L):
            rows = slice(t * H, (t + 1) * H)
            w_re, w_im = _cmul(pw_ref[g, 0, t:t + 1, :], pw_ref[g, 1, t:t + 1, :], bt_re, bt_im)
            w_scr[rows, :half] = w_re
            w_scr[rows, half:] = w_im
            g_re, g_im = _cmul(pv_ref[g, 0, t:t + 1, :], pv_ref[g, 1, t:t + 1, :], c_re, c_im)
            vt_ref[g, rows, :half] = g_re.astype(vt_ref.dtype)
            vt_ref[g, rows, half:] = (-g_im).astype(vt_ref.dtype)
        wt = w_scr[...].T
        wt_ref[g] = wt.astype(wt_ref.dtype)
        kk = jnp.dot(csel_ref[g], wt, preferred_element_type=F32, precision=lax.Precision.HIGHEST)
        kf = jnp.concatenate([kk[:H], zeros], axis=1)
        kb = pltpu.roll(jnp.concatenate([kk[H:], zeros], axis=1), (L - 1) * H, 1)
        dmat = jnp.concatenate([dcol_ref[g]] * (2 * f // LANES), axis=1)
        k = kf + kb + jnp.where(centre, dmat, 0.0)
        for t in range(L):
            off = f - (t + 1) * H
            win = k if off == 0 else pltpu.roll(k, 2 * f - off, 1)
            mt_ref[g, t * H:(t + 1) * H, :] = win[:, :f].astype(mt_ref.dtype)
        ac_re, ac_im = pa_ref[g, 0:1, :], pa_ref[g, 1:2, :]
        as_re, as_im = _cpow(ac_re, ac_im, seg_steps)
        a_ref[g, 0:1, :] = ac_re
        a_ref[g, 1:2, :] = ac_im
        a_ref[g, 2:3, :] = as_re
        a_ref[g, 3:4, :] = as_im


def _ssm_operators(lam_re, lam_im, log_dt, b_re, b_im, c_re, c_im, d_skip, seg_steps):
    L = SSM_CHUNK
    H = SSM_GROUP
    lam_re = lam_re.astype(F32)
    lam_im = lam_im.astype(F32)
    depth, _, ng, _ = lam_re.shape
    dt = jnp.exp(log_dt.astype(F32))[..., None]
    mag = jnp.exp(lam_re * dt)
    a_re = mag * jnp.cos(lam_im * dt)
    a_im = mag * jnp.sin(lam_im * dt)
    nr, ni = a_re - 1.0, a_im
    den = lam_re * lam_re + lam_im * lam_im
    f_re = (nr * lam_re + ni * lam_im) / den
    f_im = (ni * lam_re - nr * lam_im) / den
    br, bi = b_re.astype(F32), b_im.astype(F32)
    bb_re = f_re[..., None] * br - f_im[..., None] * bi
    bb_im = f_re[..., None] * bi + f_im[..., None] * br
    k = jnp.arange(L + 1, dtype=F32)[:, None]
    pmag = jnp.exp(lam_re[..., None, :] * dt[..., None, :] * k)
    parg = lam_im[..., None, :] * dt[..., None, :] * k
    p_re = pmag * jnp.cos(parg)
    p_im = pmag * jnp.sin(parg)
    both = lambda fwd, bwd: jnp.concatenate([fwd, bwd], axis=-1)
    reim = lambda re, im: jnp.stack([re, im], axis=2)
    pw = reim(both(p_re[:, 0, :, L - 1::-1][:, :, :L], p_re[:, 1, :, :L]),
              both(p_im[:, 0, :, L - 1::-1][:, :, :L], p_im[:, 1, :, :L]))
    pv = reim(both(p_re[:, 0, :, 1:], p_re[:, 1, :, :0:-1]), both(p_im[:, 0, :, 1:], p_im[:, 1, :, :0:-1]))
    pa = jnp.stack([both(p_re[:, 0, :, L], p_re[:, 1, :, L]), both(p_im[:, 0, :, L], p_im[:, 1, :, L])], axis=2)
    sw = lambda z: jnp.swapaxes(z, -1, -2)
    bbt = reim(both(sw(bb_re[:, 0]), sw(bb_re[:, 1])), both(sw(bb_im[:, 0]), sw(bb_im[:, 1])))
    cr, ci = c_re.astype(F32), c_im.astype(F32)
    cc = reim(both(cr[:, 0], cr[:, 1]), both(ci[:, 0], ci[:, 1]))
    z = jnp.zeros_like(cr[:, 0])
    csel = jnp.concatenate([jnp.concatenate([cr[:, 0], z, -ci[:, 0], z], axis=-1),
                            jnp.concatenate([z, cr[:, 1], z, -ci[:, 1]], axis=-1)], axis=-2)
    dcol = jnp.broadcast_to(d_skip.astype(F32).reshape(depth, ng, H, 1), (depth, ng, H, LANES))
    f = L * H
    ns = 2 * N_DIR * STATE
    gb = 8
    blk = lambda *shape: pl.BlockSpec((None, gb) + shape, lambda l, j: (l, j) + (0,) * len(shape))
    return pl.pallas_call(
        functools.partial(_ops_kernel, seg_steps),
        grid=(depth, ng // gb),
        in_specs=[blk(2, L, ns // 2), blk(2, L, ns // 2), blk(2, ns // 2), blk(2, H, ns // 2),
                  blk(2, H, ns // 2), blk(2 * H, ns), blk(H, LANES)],
        out_specs=[blk(f, f), blk(ns, f), blk(f, ns), blk(4, ns // 2)],
        out_shape=[jax.ShapeDtypeStruct((depth, ng, f, f), BF16), jax.ShapeDtypeStruct((depth, ng, ns, f), BF16),
                   jax.ShapeDtypeStruct((depth, ng, f, ns), BF16), jax.ShapeDtypeStruct((depth, ng, 4, ns // 2), F32)],
        scratch_shapes=[pltpu.VMEM((f, ns), F32)],
        compiler_params=pltpu.CompilerParams(
            dimension_semantics=("parallel", "parallel"), vmem_limit_bytes=VMEM_LIMIT_BYTES),
        name="ssm_operators",
    )(pw, pv, pa, bbt, cc, csel, dcol)


def _gelu_tanh(x):
    return 0.5 * x * (1.0 + jnp.tanh(np.sqrt(2.0 / np.pi).astype(np.float32)
                                     * (x + np.float32(0.044715) * (x * x * x))))


def _mixer_kernel(tiles_per_seq, ctx_len, first, x_ref, *refs):
    if first:
        xc_ref, *refs = refs
    (ysl_ref, ysc_ref, mod_ref, g_ref, wa_ref, ba_ref, wg_ref, bg_ref, cw_ref,
     woa_ref, wglu_ref, bglu_ref, wo_ref, o_ref) = refs
    tm, d = x_ref.shape
    dc = woa_ref.shape[0]
    n = ysc_ref.shape[-1]
    is_ctx, row = _tile_kind(tiles_per_seq)
    shift, scale, gate = (_mod_vec(mod_ref, row, k, d) for k in range(3))
    nb = MIXER_ROW_BLOCKS
    rows = tm // nb
    assert rows % ctx_len == 0 and rows % SSM_CHUNK == 0
    t = lax.broadcasted_iota(jnp.int32, (rows, 1), 0)
    seg = jnp.where(is_ctx, ctx_len - 1, GRID_W - 1)
    pos = t & seg
    blocks = [slice(k * rows, (k + 1) * rows) for k in range(nb)]
    xs, za, zg = [], [], []
    for k, r in enumerate(blocks):
        x = jnp.where(is_ctx, xc_ref[r, :], x_ref[r, :]) if first else x_ref[r, :]
        h = _norm_mod(x, g_ref[...], shift, scale).astype(BF16)
        xs.append(x)
        za.append(jnp.dot(h, wa_ref[...], preferred_element_type=F32) + ba_ref[...])
        zg.append(jnp.dot(h, wg_ref[...], preferred_element_type=F32) + bg_ref[...])
    y_a, y_b = [], []
    for k, r in enumerate(blocks):
        g_b, g_c, x_in = za[k][:, :dc], za[k][:, dc:2 * dc], za[k][:, 2 * dc:]
        v = g_c * x_in
        v_prev = jnp.where(pos == 0, 0.0, pltpu.roll(v, 1, 0))
        v_next = jnp.where(pos == seg, 0.0, pltpu.roll(v, rows - 1, 0))
        cv = cw_ref[0:1, :] * v_prev + cw_ref[1:2, :] * v + cw_ref[2:3, :] * v_next
        y_a.append(jnp.dot((g_b * cv).astype(BF16), woa_ref[...], preferred_element_type=F32))
        ys = ysl_ref[k * rows // SSM_CHUNK:(k + 1) * rows // SSM_CHUNK, :SSM_CHUNK, :].reshape(rows, n)
        s = _gelu_tanh(jnp.where(is_ctx, ysc_ref[r, :], ys))
        gl = jnp.dot(s.astype(BF16), wglu_ref[...], preferred_element_type=F32) + bglu_ref[...]
        y_b.append(gl[:, :d] * jax.nn.sigmoid(gl[:, d:]))
    for k, r in enumerate(blocks):
        gate_a, gate_b = zg[k][:, :d], zg[k][:, d:]
        merged = jax.nn.sigmoid(gate_a) * y_a[k] + jax.nn.sigmoid(gate_b) * y_b[k]
        out = jnp.dot(merged.astype(BF16), wo_ref[...], preferred_element_type=F32)
        o_ref[r, :] = xs[k] + gate * out


def _mixer(tokens, ys_lat, ys_ctx, params, bsz, seq, ctx_len):
    first = len(tokens) == 2
    d = tokens[0].shape[-1]
    tiles_per_seq = seq // TOK_TILE
    arrays, specs = params
    return pl.pallas_call(
        functools.partial(_mixer_kernel, tiles_per_seq, ctx_len, first),
        grid=(1 + bsz * tiles_per_seq,),
        in_specs=(_input_token_specs(d, tiles_per_seq) if first else [_stream_spec(d)])
        + _chunk_specs(ys_ctx.shape[-1], tiles_per_seq) + specs,
        out_specs=_stream_spec(d),
        out_shape=_stream_shape(bsz, seq, d),
        compiler_params=pltpu.CompilerParams(
            dimension_semantics=("arbitrary",), vmem_limit_bytes=VMEM_LIMIT_BYTES),
        name="mixer",
    )(*tokens, ys_lat, ys_ctx, *arrays)


def _ffn_kernel(tiles_per_seq, final, x_ref, *refs):
    if final:
        mod_ref, g_ref, win_ref, wout_ref, fg_ref, o_ref = refs
    else:
        mod_ref, g_ref, win_ref, wout_ref, modn_ref, gn_ref, wu_ref, bu_ref, o_ref, ul_ref, uc_ref = refs
    tm, d = x_ref.shape
    dff = wout_ref.shape[0]
    is_ctx, row = _tile_kind(tiles_per_seq, first_step=1 if final else 0)
    shift, scale, gate_mod = (_mod_vec(mod_ref, row, k, d) for k in (3, 4, 5))
    rows = tm // FFN_ROW_BLOCKS
    blocks = [slice(k * rows, (k + 1) * rows) for k in range(FFN_ROW_BLOCKS)]
    xs, zs = [], []
    for r in blocks:
        x = x_ref[r, :]
        h = _norm_mod(x, g_ref[...], shift, scale)
        xs.append(x)
        zs.append(jnp.dot(h.astype(BF16), win_ref[...], preferred_element_type=F32))
    ys = []
    for x, z in zip(xs, zs):
        gate, up = z[:, :dff], z[:, dff:]
        act = gate * jax.nn.sigmoid(gate) * up
        out = jnp.dot(act.astype(BF16), wout_ref[...], preferred_element_type=F32)
        ys.append(x + gate_mod * out)
    if final:
        for r, y in zip(blocks, ys):
            ms = jnp.mean(y * y, axis=-1, keepdims=True)
            o_ref[r, :] = y * lax.rsqrt(ms + RMS_EPS) * fg_ref[...]
    else:
        for r, y in zip(blocks, ys):
            o_ref[r, :] = y
        _emit_ssm_inputs(jnp.concatenate(ys, axis=0), row, is_ctx, modn_ref, gn_ref, wu_ref, bu_ref, ul_ref, uc_ref)


def _ffn(xs, params, nxt, bsz, seq):
    d = xs.shape[-1]
    tiles_per_seq = seq // TOK_TILE
    arrays, specs = params
    return pl.pallas_call(
        functools.partial(_ffn_kernel, tiles_per_seq, False),
        grid=(1 + bsz * tiles_per_seq,),
        in_specs=[_stream_spec(d)] + specs + nxt[1],
        out_specs=[_stream_spec(d)] + _chunk_specs(nxt[2], tiles_per_seq),
        out_shape=[_stream_shape(bsz, seq, d)] + _ssm_input_shapes(bsz, seq, nxt[2]),
        compiler_params=pltpu.CompilerParams(
            dimension_semantics=("arbitrary",), vmem_limit_bytes=VMEM_LIMIT_BYTES),
        name="ffn",
    )(xs, *arrays, *nxt[0])


def _ffn_final(xs, params, final_g, bsz, seq):
    d = xs.shape[-1]
    tiles_per_seq = seq // TOK_TILE
    arrays, specs = params
    return pl.pallas_call(
        functools.partial(_ffn_kernel, tiles_per_seq, True),
        grid=(bsz * tiles_per_seq,),
        in_specs=[_stream_spec(d, first_step=1)] + specs + [_layer_spec(final_g, 0)],
        out_specs=pl.BlockSpec((None, TOK_TILE, d), lambda s: (s // tiles_per_seq, s % tiles_per_seq, 0)),
        out_shape=jax.ShapeDtypeStruct((bsz, seq, d), F32),
        compiler_params=pltpu.CompilerParams(
            dimension_semantics=("arbitrary",), vmem_limit_bytes=VMEM_LIMIT_BYTES),
        name="ffn_final",
    )(xs, *arrays, final_g)


def kernel(x, c, ctx, c_ctx, w_mod, b_mod, norm1_g, norm2_g, w_in, b_in, conv_w, w_out_a, lam_re, lam_im,
           log_dt, b_re, b_im, c_re, c_im, d_skip, w_glu, b_glu, w_o, w_ff_in, w_ff_out, final_g):
    bsz, seq, d = x.shape
    depth = w_mod.shape[0]
    ctx_len = ctx.shape[1]
    d_conv = conv_w.shape[-1]
    d_ssm = d_skip.shape[-1]
    u_lo, u_hi = 3 * d_conv, 3 * d_conv + d_ssm
    assert bsz * ctx_len == TOK_TILE and TOK_TILE % GRID_W == 0 and bsz <= 2
    assert ctx_len == SSM_CHUNK * SUBLANES and d_ssm % LANES == 0 and conv_w.shape[1] == CONV_K
    assert seq % (SUBLANES * TOK_TILE) == 0
    seg_steps = seq // (SSM_CHUNK * SUBLANES)

    cvec = jnp.zeros((8, d), F32).at[:bsz].set(c.astype(F32)).at[2].set(c_ctx.astype(F32))
    mod = _mod_all(cvec, w_mod, b_mod)

    mt_op, wt_op, vt_op, a_op = _ssm_operators(lam_re, lam_im, log_dt, b_re, b_im, c_re, c_im, d_skip,
                                               seg_steps)

    w_in, w_out_a, w_glu, w_o = (w.astype(BF16) for w in (w_in, w_out_a, w_glu, w_o))
    w_ff_in, w_ff_out = w_ff_in.astype(BF16), w_ff_out.astype(BF16)
    row = lambda v: v.reshape(v.shape[0], 1, v.shape[-1])
    b_in, b_glu, g1, g2, fg = row(b_in), row(b_glu), row(norm1_g), row(norm2_g), final_g.reshape(1, 1, d)
    gates_lo = u_hi

    def operands(l, *items):
        arrays = [arr for arr, _ in items]
        return arrays, [_layer_spec(arr, l, cols) for arr, cols in items]

    u_cols = (u_lo, d_ssm)
    ssm_in = lambda l: operands(l, (mod, None), (g1, None), (w_in, u_cols), (b_in, u_cols)) + (d_ssm,)
    mixer_in = lambda l: operands(
        l, (mod, None), (g1, None), (w_in, (0, u_lo)), (b_in, (0, u_lo)),
        (w_in, (gates_lo, 2 * d)), (b_in, (gates_lo, 2 * d)), (conv_w, None), (w_out_a, None),
        (w_glu, None), (b_glu, None), (w_o, None))
    ffn_in = lambda l: operands(l, (mod, None), (g2, None), (w_ff_in, None), (w_ff_out, None))

    tokens = (x, ctx.reshape(bsz * ctx_len, d))
    u_lat, u_ctx = _uproj(*tokens, ssm_in(0))
    for l in range(depth):
        ys_lat, ys_ctx = _ssm(u_lat, u_ctx, mt_op, wt_op, vt_op, a_op, l)
        xs = _mixer(tokens, ys_lat, ys_ctx, mixer_in(l), bsz, seq, ctx_len)
        if l + 1 < depth:
            xs, u_lat, u_ctx = _ffn(xs, ffn_in(l), ssm_in(l + 1), bsz, seq)
            tokens = (xs,)
    return _ffn_final(xs, ffn_in(depth - 1), fg, bsz, seq)
```

```python
import functools

import numpy as np
import jax
import jax.numpy as jnp
from jax import lax
from jax.experimental import pallas as pl
from jax.experimental.pallas import tpu as pltpu

GRID_W = 64
CONV_K = 3
SSM_GROUP = 16
STATE = 64
N_DIR = 2
RMS_EPS = 1e-6

TOK_TILE = 512
SSM_CHUNK = 32
CHUNK_PITCH = 40
SUBLANES = 8
LANES = 128
SSM_LANE_GROUPS = LANES // SSM_GROUP
SSM_SCAN_GROUPS = 4
SSM_IO_PARTS = 2
MIXER_ROW_BLOCKS = 2
FFN_ROW_BLOCKS = 4
MOD_COL_BLOCK = 1536
VMEM_LIMIT_BYTES = 56 * 1024 * 1024

F32 = jnp.float32
BF16 = jnp.bfloat16


def _layer_spec(arr, l, cols=None):
    _, rows, n = arr.shape
    start, width = (0, n) if cols is None else cols
    assert start % width == 0
    return pl.BlockSpec((None, rows, width), lambda *_: (l, 0, start // width), pipeline_mode=pl.Buffered(1))


def _tile_coords(s, tiles_per_seq):
    k = jnp.maximum(s - 1, 0)
    return k // tiles_per_seq, k % tiles_per_seq


def _input_token_specs(d, tiles_per_seq):
    return [pl.BlockSpec((None, TOK_TILE, d), lambda s: (*_tile_coords(s, tiles_per_seq), 0)),
            pl.BlockSpec((TOK_TILE, d), lambda s: (0, 0))]


def _stream_spec(d, first_step=0):
    return pl.BlockSpec((None, TOK_TILE, d), lambda s: (s + first_step, 0, 0))


def _stream_shape(bsz, seq, d):
    return jax.ShapeDtypeStruct((1 + bsz * (seq // TOK_TILE), TOK_TILE, d), F32)


def _chunk_specs(n, tiles_per_seq):
    cpt = TOK_TILE // SSM_CHUNK
    tiles_per_seg = tiles_per_seq // SUBLANES

    def index(s):
        b, i = _tile_coords(s, tiles_per_seq)
        return (b, i % tiles_per_seg, i // tiles_per_seg, 0, 0)

    return [pl.BlockSpec((None, cpt, None, CHUNK_PITCH, n), index),
            pl.BlockSpec((TOK_TILE, n), lambda s: (0, 0))]


def _tile_kind(tiles_per_seq, first_step=0):
    s = pl.program_id(0) + first_step
    is_ctx = s == 0
    return is_ctx, jnp.where(is_ctx, 2, _tile_coords(s, tiles_per_seq)[0])


def _norm_mod(x, g, shift, scale):
    ms = jnp.mean(x * x, axis=-1, keepdims=True)
    y = x * lax.rsqrt(ms + RMS_EPS) * g
    return y * (1.0 + scale) + shift


def _mod_vec(mod_ref, row, k, d):
    return mod_ref[pl.ds(row, 1), k * d:(k + 1) * d]


def _mod_kernel(c_ref, w_ref, b_ref, o_ref):
    c = c_ref[...]
    s = c * jax.nn.sigmoid(c)
    o_ref[...] = jnp.dot(s.astype(BF16), w_ref[...].astype(BF16),
                         preferred_element_type=F32) + b_ref[...]


def _mod_all(cvec, w_mod, b_mod):
    depth, d, n = w_mod.shape
    tn = MOD_COL_BLOCK
    return pl.pallas_call(
        _mod_kernel,
        grid=(depth, n // tn),
        in_specs=[pl.BlockSpec((8, d), lambda l, j: (0, 0)),
                  pl.BlockSpec((None, d, tn), lambda l, j: (l, 0, j)),
                  pl.BlockSpec((None, 1, tn), lambda l, j: (l, 0, j))],
        out_specs=pl.BlockSpec((None, 8, tn), lambda l, j: (l, 0, j)),
        out_shape=jax.ShapeDtypeStruct((depth, 8, n), F32),
        compiler_params=pltpu.CompilerParams(
            dimension_semantics=("arbitrary", "arbitrary"), vmem_limit_bytes=VMEM_LIMIT_BYTES),
        name="adaln_mod",
    )(cvec, w_mod, b_mod.reshape(depth, 1, n))


def _emit_ssm_inputs(x, row, is_ctx, mod_ref, g_ref, w_ref, b_ref, ul_ref, uc_ref):
    d = x.shape[-1]
    h = _norm_mod(x, g_ref[...], _mod_vec(mod_ref, row, 0, d), _mod_vec(mod_ref, row, 1, d))
    u = jnp.dot(h.astype(BF16), w_ref[...], preferred_element_type=F32) + b_ref[...]
    cpt, pitch, n = ul_ref.shape
    ul_ref[:, :SSM_CHUNK, :] = u.reshape(cpt, SSM_CHUNK, n)
    ul_ref[:, SSM_CHUNK:, :] = jnp.zeros((cpt, pitch - SSM_CHUNK, n), F32)

    @pl.when(is_ctx)
    def _():
        uc_ref[...] = u


def _uproj_kernel(tiles_per_seq, xl_ref, xc_ref, mod_ref, g_ref, w_ref, b_ref, ul_ref, uc_ref):
    is_ctx, row = _tile_kind(tiles_per_seq)
    x = jnp.where(is_ctx, xc_ref[...], xl_ref[...])
    _emit_ssm_inputs(x, row, is_ctx, mod_ref, g_ref, w_ref, b_ref, ul_ref, uc_ref)


def _ssm_input_shapes(bsz, seq, n):
    steps = seq // (SSM_CHUNK * SUBLANES)
    return [jax.ShapeDtypeStruct((bsz, steps, SUBLANES, CHUNK_PITCH, n), F32),
            jax.ShapeDtypeStruct((TOK_TILE, n), F32)]


def _uproj(xl, xc, ssm_in):
    bsz, seq, d = xl.shape
    tiles_per_seq = seq // TOK_TILE
    arrays, specs, n_u = ssm_in
    return pl.pallas_call(
        functools.partial(_uproj_kernel, tiles_per_seq),
        grid=(1 + bsz * tiles_per_seq,),
        in_specs=_input_token_specs(d, tiles_per_seq) + specs,
        out_specs=_chunk_specs(n_u, tiles_per_seq),
        out_shape=_ssm_input_shapes(bsz, seq, n_u),
        compiler_params=pltpu.CompilerParams(
            dimension_semantics=("arbitrary",), vmem_limit_bytes=VMEM_LIMIT_BYTES),
        name="ssm_uproj",
    )(xl, xc, *arrays)


def _cmul(a_re, a_im, b_re, b_im):
    return a_re * b_re - a_im * b_im, a_re * b_im + a_im * b_re


def _ssm_kernel(ctx_chunks, n_work, ul_ref, uc_ref, mt_ref, wt_ref, vt_ref, a_ref, yl_ref, yc_ref,
                ut_scr, yt_scr, x_scr, hf_scr, hb_scr):
    L = SSM_CHUNK
    H = SSM_GROUP
    parts = SSM_IO_PARTS
    n_part = ul_ref.shape[0] // CHUNK_PITCH
    n_lat = n_part * parts
    n_all = ut_scr.shape[-1]
    w, phase = pl.program_id(0), pl.program_id(1)
    has_prev, has_next = w > 0, w < n_work
    trow = lambda t: pl.ds(pl.multiple_of(t * H, H), H)

    def fill(cols, rows_of_t, unroll):
        def body(t, carry):
            at = rows_of_t(t).T
            for g in range(SSM_LANE_GROUPS):
                ut_scr[g, trow(t), cols] = at[g * H:(g + 1) * H, :].astype(BF16)
            return carry
        lax.fori_loop(0, L, body, 0, unroll=unroll)

    for q in range(parts):
        @pl.when(jnp.logical_and(phase == 1 + parts + q, has_next))
        def _():
            fill(slice(q * n_part, (q + 1) * n_part), lambda t: ul_ref[pl.ds(t, n_part, stride=CHUNK_PITCH), :], 16)

    @pl.when(jnp.logical_and(phase == 0, has_prev))
    def _():
        pad = jnp.zeros((n_all - n_lat - ctx_chunks, LANES), F32)
        fill(slice(n_lat, n_all),
             lambda t: jnp.concatenate([uc_ref[pl.ds(t, ctx_chunks, stride=L), :], pad], axis=0), True)
        _ssm_compute(ctx_chunks, n_lat, mt_ref, wt_ref, vt_ref, a_ref, ut_scr, yt_scr, x_scr, hf_scr, hb_scr)

        def drain_ctx(t, carry):
            y = yt_scr[t, :, n_lat:].astype(F32).T
            yc_ref[pl.ds(t, ctx_chunks, stride=L), :] = y[:ctx_chunks]
            return carry
        lax.fori_loop(0, L, drain_ctx, 0, unroll=True)

    for q in range(parts):
        @pl.when(jnp.logical_and(phase == 1 + q, has_prev))
        def _():
            def drain(t, carry):
                y = yt_scr[t, :, q * n_part:(q + 1) * n_part].astype(F32).T
                yl_ref[pl.ds(t, n_part, stride=CHUNK_PITCH), :] = y
                return carry
            lax.fori_loop(0, L, drain, 0, unroll=16)
            for t in range(L, CHUNK_PITCH):
                yl_ref[pl.ds(t, n_part, stride=CHUNK_PITCH), :] = jnp.zeros((n_part, LANES), F32)


def _ssm_compute(ctx_chunks, n_lat, mt_ref, wt_ref, vt_ref, a_ref, ut_scr, yt_scr, x_scr, hf_scr, hb_scr):
    L = SSM_CHUNK
    H = SSM_GROUP
    sub = SUBLANES
    n_all = ut_scr.shape[-1]
    steps = n_lat // sub
    half = STATE * N_DIR
    gl = SSM_LANE_GROUPS
    gs = SSM_SCAN_GROUPS

    is_fwd = lax.broadcasted_iota(jnp.int32, (sub, half), 1) < STATE
    is_fwd1 = is_fwd[:1]
    sl = lax.broadcasted_iota(jnp.int32, (sub, half), 0)
    blk = lambda j: pl.ds(pl.multiple_of(j * sub, sub), sub)
    zero = jnp.zeros((sub, half), F32)
    lane2 = lax.broadcasted_iota(jnp.int32, (n_all, 2 * half), 1)
    take_fwd = (lane2 % half) < STATE
    tail0 = n_lat + ctx_chunks
    assert tail0 % sub == 0 and ctx_chunks == sub

    for g0 in range(0, gl, gs):
        coef = lambda g, r, rows=sub: jnp.broadcast_to(a_ref[g0 + g, r:r + 1, :], (rows, half))
        a_re = [coef(g, 0) for g in range(gs)]
        a_im = [coef(g, 1) for g in range(gs)]

        for g in range(gs):
            xt = jnp.dot(wt_ref[g0 + g], ut_scr[g0 + g], preferred_element_type=F32)
            x_scr[g] = xt.T
            hf_scr[g, tail0:, :] = jnp.zeros((n_all - tail0, 2 * half), F32)
            hb_scr[g, tail0:, :] = jnp.zeros((n_all - tail0, 2 * half), F32)

        h0 = []
        for g in range(gs):
            xc = x_scr[g, n_lat:n_lat + ctx_chunks, :]
            ar, ai = a_re[g][:1], a_im[g][:1]
            h_re = h_im = jnp.zeros((1, half), F32)
            hf_rows, hb_rows = [], [None] * ctx_chunks
            for k in range(ctx_chunks):
                kb = ctx_chunks - 1 - k
                hf_rows.append((h_re, h_im))
                hb_rows[kb] = (h_re, h_im)
                x_re = jnp.where(is_fwd1, xc[k:k + 1, :half], xc[kb:kb + 1, :half])
                x_im = jnp.where(is_fwd1, xc[k:k + 1, half:], xc[kb:kb + 1, half:])
                p_re, p_im = _cmul(ar, ai, h_re, h_im)
                h_re, h_im = p_re + x_re, p_im + x_im
            cat = lambda rows, part: jnp.concatenate([r[part] for r in rows], axis=0)
            hf_scr[g, n_lat:n_lat + ctx_chunks, :half] = cat(hf_rows, 0)
            hf_scr[g, n_lat:n_lat + ctx_chunks, half:] = cat(hf_rows, 1)
            hb_scr[g, n_lat:n_lat + ctx_chunks, :half] = cat(hb_rows, 0)
            hb_scr[g, n_lat:n_lat + ctx_chunks, half:] = cat(hb_rows, 1)
            h0.append((jnp.broadcast_to(h_re, (sub, half)), jnp.broadcast_to(h_im, (sub, half))))

        def scan_step(j, carry):
            jb = steps - 1 - j
            new = []
            for g in range(gs):
                h_re, h_im = carry[g]
                x_re = jnp.where(is_fwd, x_scr[g, blk(j), :half], x_scr[g, blk(jb), :half])
                x_im = jnp.where(is_fwd, x_scr[g, blk(j), half:], x_scr[g, blk(jb), half:])
                hf_scr[g, blk(j), :half] = h_re
                hf_scr[g, blk(j), half:] = h_im
                hb_scr[g, blk(jb), :half] = h_re
                hb_scr[g, blk(jb), half:] = h_im
                p_re, p_im = _cmul(a_re[g], a_im[g], h_re, h_im)
                new.append((p_re + x_re, p_im + x_im))
            return tuple(new)

        ends = lax.fori_loop(0, steps, scan_step, tuple((zero, zero) for _ in range(gs)), unroll=True)

        carries = []
        for g in range(gs):
            def shift(t, h):
                return jnp.where(is_fwd, jnp.where(sl < 1, h, pltpu.roll(t, 1, 0)),
                                 jnp.where(sl >= sub - 1, h, pltpu.roll(t, sub - 1, 0)))
            e_re, e_im = ends[g]
            s_re, s_im = coef(g, 2), coef(g, 3)
            c_re, c_im = shift(zero, h0[g][0]), shift(zero, h0[g][1])
            for _ in range(sub - 1):
                p_re, p_im = _cmul(s_re, s_im, c_re, c_im)
                c_re, c_im = shift(e_re + p_re, h0[g][0]), shift(e_im + p_im, h0[g][1])
            carries.append((c_re, c_im))

        def fix_step(j, carry):
            jb = steps - 1 - j
            new = []
            for g in range(gs):
                d_re, d_im = carry[g]
                hf_scr[g, blk(j), :half] += d_re
                hf_scr[g, blk(j), half:] += d_im
                hb_scr[g, blk(jb), :half] += d_re
                hb_scr[g, blk(jb), half:] += d_im
                new.append(_cmul(a_re[g], a_im[g], d_re, d_im))
            return tuple(new)

        lax.fori_loop(0, steps, fix_step, tuple(carries), unroll=True)

        for g in range(gs):
            h_in = jnp.where(take_fwd, hf_scr[g], hb_scr[g]).T.astype(BF16)
            yt = (jnp.dot(mt_ref[g0 + g], ut_scr[g0 + g], preferred_element_type=F32)
                  + jnp.dot(vt_ref[g0 + g], h_in, preferred_element_type=F32))
            for t in range(L):
                yt_scr[t, (g0 + g) * H:(g0 + g + 1) * H, :] = yt[t * H:(t + 1) * H, :].astype(BF16)


def _ssm(u_lat, u_ctx, mt, wt, vt, a, l):
    bsz, steps, nseg, pitch, n = u_lat.shape
    L = SSM_CHUNK
    ctx_len = u_ctx.shape[0] // bsz
    ctx_chunks = ctx_len // L
    n_lat = steps * nseg
    n_all = -(-(n_lat + ctx_chunks) // LANES) * LANES
    gl = SSM_LANE_GROUPS
    f = L * SSM_GROUP
    ns = wt.shape[-2]
    parts = SSM_IO_PARTS
    n_blocks = n // LANES
    part_rows = n_lat * pitch // parts
    n_work = bsz * n_blocks
    item = lambda w: (jnp.clip(w, 0, n_work - 1) // n_blocks, jnp.clip(w, 0, n_work - 1) % n_blocks)

    def lat_in_index(w, p):
        b, j = item(jnp.where(p == 0, w - 1, w))
        return (b, jnp.where(p == 0, parts - 1, jnp.maximum(p - parts - 1, 0)), j)

    def lat_out_index(w, p):
        b, j = item(w - 1)
        return (b, jnp.where(w == 0, 0, jnp.clip(p - 1, 0, parts - 1)), j)

    def ctx_index(w, p):
        b, j = item(w - 1)
        return (b, 0, j)

    def op_index(w, p):
        return (l, item(jnp.where(p == 0, w - 1, w))[1], 0, 0)

    lat_in = pl.BlockSpec((None, part_rows, LANES), lat_in_index)
    lat_out = pl.BlockSpec((None, part_rows, LANES), lat_out_index)
    ctx_spec = pl.BlockSpec((None, ctx_len, LANES), ctx_index)
    grp = lambda shape: pl.BlockSpec((None, gl) + shape, op_index)
    y_lat, y_ctx = pl.pallas_call(
        functools.partial(_ssm_kernel, ctx_chunks, n_work),
        grid=(n_work + 1, 2 * parts + 1),
        in_specs=[lat_in, ctx_spec, grp((f, f)), grp((ns, f)), grp((f, ns)), grp((4, ns // 2))],
        out_specs=[lat_out, ctx_spec],
        out_shape=[jax.ShapeDtypeStruct((bsz, n_lat * pitch, n), F32),
                   jax.ShapeDtypeStruct((bsz, ctx_len, n), F32)],
        scratch_shapes=[pltpu.VMEM((gl, f, n_all), BF16), pltpu.VMEM((L, LANES, n_all), BF16)]
        + [pltpu.VMEM((SSM_SCAN_GROUPS, n_all, ns), F32)] * 3,
        compiler_params=pltpu.CompilerParams(
            dimension_semantics=("arbitrary", "arbitrary"), vmem_limit_bytes=VMEM_LIMIT_BYTES),
        name="ssm_chunked",
    )(u_lat.reshape(bsz, n_lat * pitch, n), u_ctx.reshape(bsz, ctx_len, n), mt, wt, vt, a)
    return y_lat.reshape(u_lat.shape), y_ctx.reshape(u_ctx.shape)


def _cpow(z_re, z_im, n):
    out = None
    while n:
        if n & 1:
            out = (z_re, z_im) if out is None else _cmul(out[0], out[1], z_re, z_im)
        n >>= 1
        if n:
            z_re, z_im = _cmul(z_re, z_im, z_re, z_im)
    return out


def _ops_kernel(seg_steps, pw_ref, pv_ref, pa_ref, bbt_ref, c_ref, csel_ref, dcol_ref,
                mt_ref, wt_ref, vt_ref, a_ref, w_scr):
    gb = mt_ref.shape[0]
    f = mt_ref.shape[-1]
    H = SSM_GROUP
    L = f // H
    half = STATE * N_DIR
    lane = lax.broadcasted_iota(jnp.int32, (H, 2 * f), 1)
    row = lax.broadcasted_iota(jnp.int32, (H, 2 * f), 0)
    centre = lane == (L - 1) * H + row
    zeros = jnp.zeros((H, f), F32)
    for g in range(gb):
        bt_re, bt_im = bbt_ref[g, 0], bbt_ref[g, 1]
        c_re, c_im = c_ref[g, 0], c_ref[g, 1]
        for t in range(L):
            rows = slice(t * H, (t + 1) * H)
            w_re, w_im = _cmul(pw_ref[g, 0, t:t + 1, :], pw_ref[g, 1, t:t + 1, :], bt_re, bt_im)
            w_scr[rows, :half] = w_re
            w_scr[rows, half:] = w_im
            g_re, g_im = _cmul(pv_ref[g, 0, t:t + 1, :], pv_ref[g, 1, t:t + 1, :], c_re, c_im)
            vt_ref[g, rows, :half] = g_re.astype(vt_ref.dtype)
            vt_ref[g, rows, half:] = (-g_im).astype(vt_ref.dtype)
        wt = w_scr[...].T
        wt_ref[g] = wt.astype(wt_ref.dtype)
        kk = jnp.dot(csel_ref[g], wt, preferred_element_type=F32, precision=lax.Precision.HIGHEST)
        kf = jnp.concatenate([kk[:H], zeros], axis=1)
        kb = pltpu.roll(jnp.concatenate([kk[H:], zeros], axis=1), (L - 1) * H, 1)
        dmat = jnp.concatenate([dcol_ref[g]] * (2 * f // LANES), axis=1)
        k = kf + kb + jnp.where(centre, dmat, 0.0)
        for t in range(L):
            off = f - (t + 1) * H
            win = k if off == 0 else pltpu.roll(k, 2 * f - off, 1)
            mt_ref[g, t * H:(t + 1) * H, :] = win[:, :f].astype(mt_ref.dtype)
        ac_re, ac_im = pa_ref[g, 0:1, :], pa_ref[g, 1:2, :]
        as_re, as_im = _cpow(ac_re, ac_im, seg_steps)
        a_ref[g, 0:1, :] = ac_re
        a_ref[g, 1:2, :] = ac_im
        a_ref[g, 2:3, :] = as_re
        a_ref[g, 3:4, :] = as_im


def _ssm_operators(lam_re, lam_im, log_dt, b_re, b_im, c_re, c_im, d_skip, seg_steps):
    L = SSM_CHUNK
    H = SSM_GROUP
    lam_re = lam_re.astype(F32)
    lam_im = lam_im.astype(F32)
    depth, _, ng, _ = lam_re.shape
    dt = jnp.exp(log_dt.astype(F32))[..., None]
    mag = jnp.exp(lam_re * dt)
    a_re = mag * jnp.cos(lam_im * dt)
    a_im = mag * jnp.sin(lam_im * dt)
    nr, ni = a_re - 1.0, a_im
    den = lam_re * lam_re + lam_im * lam_im
    f_re = (nr * lam_re + ni * lam_im) / den
    f_im = (ni * lam_re - nr * lam_im) / den
    br, bi = b_re.astype(F32), b_im.astype(F32)
    bb_re = f_re[..., None] * br - f_im[..., None] * bi
    bb_im = f_re[..., None] * bi + f_im[..., None] * br
    k = jnp.arange(L + 1, dtype=F32)[:, None]
    pmag = jnp.exp(lam_re[..., None, :] * dt[..., None, :] * k)
    parg = lam_im[..., None, :] * dt[..., None, :] * k
    p_re = pmag * jnp.cos(parg)
    p_im = pmag * jnp.sin(parg)
    both = lambda fwd, bwd: jnp.concatenate([fwd, bwd], axis=-1)
    reim = lambda re, im: jnp.stack([re, im], axis=2)
    pw = reim(both(p_re[:, 0, :, L - 1::-1][:, :, :L], p_re[:, 1, :, :L]),
              both(p_im[:, 0, :, L - 1::-1][:, :, :L], p_im[:, 1, :, :L]))
    pv = reim(both(p_re[:, 0, :, 1:], p_re[:, 1, :, :0:-1]), both(p_im[:, 0, :, 1:], p_im[:, 1, :, :0:-1]))
    pa = jnp.stack([both(p_re[:, 0, :, L], p_re[:, 1, :, L]), both(p_im[:, 0, :, L], p_im[:, 1, :, L])], axis=2)
    sw = lambda z: jnp.swapaxes(z, -1, -2)
    bbt = reim(both(sw(bb_re[:, 0]), sw(bb_re[:, 1])), both(sw(bb_im[:, 0]), sw(bb_im[:, 1])))
    cr, ci = c_re.astype(F32), c_im.astype(F32)
    cc = reim(both(cr[:, 0], cr[:, 1]), both(ci[:, 0], ci[:, 1]))
    z = jnp.zeros_like(cr[:, 0])
    csel = jnp.concatenate([jnp.concatenate([cr[:, 0], z, -ci[:, 0], z], axis=-1),
                            jnp.concatenate([z, cr[:, 1], z, -ci[:, 1]], axis=-1)], axis=-2)
    dcol = jnp.broadcast_to(d_skip.astype(F32).reshape(depth, ng, H, 1), (depth, ng, H, LANES))
    f = L * H
    ns = 2 * N_DIR * STATE
    gb = SSM_LANE_GROUPS
    blk = lambda *shape: pl.BlockSpec((None, gb) + shape, lambda l, j: (l, j) + (0,) * len(shape))
    return pl.pallas_call(
        functools.partial(_ops_kernel, seg_steps),
        grid=(depth, ng // gb),
        in_specs=[blk(2, L, ns // 2), blk(2, L, ns // 2), blk(2, ns // 2), blk(2, H, ns // 2),
                  blk(2, H, ns // 2), blk(2 * H, ns), blk(H, LANES)],
        out_specs=[blk(f, f), blk(ns, f), blk(f, ns), blk(4, ns // 2)],
        out_shape=[jax.ShapeDtypeStruct((depth, ng, f, f), BF16), jax.ShapeDtypeStruct((depth, ng, ns, f), BF16),
                   jax.ShapeDtypeStruct((depth, ng, f, ns), BF16), jax.ShapeDtypeStruct((depth, ng, 4, ns // 2), F32)],
        scratch_shapes=[pltpu.VMEM((f, ns), F32)],
        compiler_params=pltpu.CompilerParams(
            dimension_semantics=("parallel", "parallel"), vmem_limit_bytes=VMEM_LIMIT_BYTES),
        name="ssm_operators",
    )(pw, pv, pa, bbt, cc, csel, dcol)


def _gelu_tanh(x):
    return 0.5 * x * (1.0 + jnp.tanh(np.sqrt(2.0 / np.pi).astype(np.float32)
                                     * (x + np.float32(0.044715) * (x * x * x))))


def _mixer_kernel(tiles_per_seq, ctx_len, first, x_ref, *refs):
    if first:
        xc_ref, *refs = refs
    (ysl_ref, ysc_ref, mod_ref, g_ref, wa_ref, ba_ref, wg_ref, bg_ref, cw_ref,
     woa_ref, wglu_ref, bglu_ref, wo_ref, o_ref) = refs
    tm, d = x_ref.shape
    dc = woa_ref.shape[0]
    n = ysc_ref.shape[-1]
    is_ctx, row = _tile_kind(tiles_per_seq)
    shift, scale, gate = (_mod_vec(mod_ref, row, k, d) for k in range(3))
    nb = MIXER_ROW_BLOCKS
    rows = tm // nb
    assert rows % ctx_len == 0 and rows % SSM_CHUNK == 0
    t = lax.broadcasted_iota(jnp.int32, (rows, 1), 0)
    seg = jnp.where(is_ctx, ctx_len - 1, GRID_W - 1)
    pos = t & seg
    blocks = [slice(k * rows, (k + 1) * rows) for k in range(nb)]
    xs, za, zg = [], [], []
    for k, r in enumerate(blocks):
        x = jnp.where(is_ctx, xc_ref[r, :], x_ref[r, :]) if first else x_ref[r, :]
        h = _norm_mod(x, g_ref[...], shift, scale).astype(BF16)
        xs.append(x)
        za.append(jnp.dot(h, wa_ref[...], preferred_element_type=F32) + ba_ref[...])
        zg.append(jnp.dot(h, wg_ref[...], preferred_element_type=F32) + bg_ref[...])
    y_a, y_b = [], []
    for k, r in enumerate(blocks):
        g_b, g_c, x_in = za[k][:, :dc], za[k][:, dc:2 * dc], za[k][:, 2 * dc:]
        v = g_c * x_in
        v_prev = jnp.where(pos == 0, 0.0, pltpu.roll(v, 1, 0))
        v_next = jnp.where(pos == seg, 0.0, pltpu.roll(v, rows - 1, 0))
        cv = cw_ref[0:1, :] * v_prev + cw_ref[1:2, :] * v + cw_ref[2:3, :] * v_next
        y_a.append(jnp.dot((g_b * cv).astype(BF16), woa_ref[...], preferred_element_type=F32))
        ys = ysl_ref[k * rows // SSM_CHUNK:(k + 1) * rows // SSM_CHUNK, :SSM_CHUNK, :].reshape(rows, n)
        s = _gelu_tanh(jnp.where(is_ctx, ysc_ref[r, :], ys))
        gl = jnp.dot(s.astype(BF16), wglu_ref[...], preferred_element_type=F32) + bglu_ref[...]
        y_b.append(gl[:, :d] * jax.nn.sigmoid(gl[:, d:]))
    for k, r in enumerate(blocks):
        gate_a, gate_b = zg[k][:, :d], zg[k][:, d:]
        merged = jax.nn.sigmoid(gate_a) * y_a[k] + jax.nn.sigmoid(gate_b) * y_b[k]
        out = jnp.dot(merged.astype(BF16), wo_ref[...], preferred_element_type=F32)
        o_ref[r, :] = xs[k] + gate * out


def _mixer(tokens, ys_lat, ys_ctx, params, bsz, seq, ctx_len):
    first = len(tokens) == 2
    d = tokens[0].shape[-1]
    tiles_per_seq = seq // TOK_TILE
    arrays, specs = params
    return pl.pallas_call(
        functools.partial(_mixer_kernel, tiles_per_seq, ctx_len, first),
        grid=(1 + bsz * tiles_per_seq,),
        in_specs=(_input_token_specs(d, tiles_per_seq) if first else [_stream_spec(d)])
        + _chunk_specs(ys_ctx.shape[-1], tiles_per_seq) + specs,
        out_specs=_stream_spec(d),
        out_shape=_stream_shape(bsz, seq, d),
        compiler_params=pltpu.CompilerParams(
            dimension_semantics=("arbitrary",), vmem_limit_bytes=VMEM_LIMIT_BYTES),
        name="mixer",
    )(*tokens, ys_lat, ys_ctx, *arrays)


def _ffn_kernel(tiles_per_seq, final, x_ref, *refs):
    if final:
        mod_ref, g_ref, win_ref, wout_ref, fg_ref, o_ref = refs
    else:
        mod_ref, g_ref, win_ref, wout_ref, modn_ref, gn_ref, wu_ref, bu_ref, o_ref, ul_ref, uc_ref = refs
    tm, d = x_ref.shape
    dff = wout_ref.shape[0]
    is_ctx, row = _tile_kind(tiles_per_seq, first_step=1 if final else 0)
    shift, scale, gate_mod = (_mod_vec(mod_ref, row, k, d) for k in (3, 4, 5))
    rows = tm // FFN_ROW_BLOCKS
    blocks = [slice(k * rows, (k + 1) * rows) for k in range(FFN_ROW_BLOCKS)]
    xs, zs = [], []
    for r in blocks:
        x = x_ref[r, :]
        h = _norm_mod(x, g_ref[...], shift, scale)
        xs.append(x)
        zs.append(jnp.dot(h.astype(BF16), win_ref[...], preferred_element_type=F32))
    ys = []
    for x, z in zip(xs, zs):
        gate, up = z[:, :dff], z[:, dff:]
        act = gate * jax.nn.sigmoid(gate) * up
        out = jnp.dot(act.astype(BF16), wout_ref[...], preferred_element_type=F32)
        ys.append(x + gate_mod * out)
    if final:
        for r, y in zip(blocks, ys):
            ms = jnp.mean(y * y, axis=-1, keepdims=True)
            o_ref[r, :] = y * lax.rsqrt(ms + RMS_EPS) * fg_ref[...]
    else:
        for r, y in zip(blocks, ys):
            o_ref[r, :] = y
        _emit_ssm_inputs(jnp.concatenate(ys, axis=0), row, is_ctx, modn_ref, gn_ref, wu_ref, bu_ref, ul_ref, uc_ref)


def _ffn(xs, params, nxt, bsz, seq):
    d = xs.shape[-1]
    tiles_per_seq = seq // TOK_TILE
    arrays, specs = params
    return pl.pallas_call(
        functools.partial(_ffn_kernel, tiles_per_seq, False),
        grid=(1 + bsz * tiles_per_seq,),
        in_specs=[_stream_spec(d)] + specs + nxt[1],
        out_specs=[_stream_spec(d)] + _chunk_specs(nxt[2], tiles_per_seq),
        out_shape=[_stream_shape(bsz, seq, d)] + _ssm_input_shapes(bsz, seq, nxt[2]),
        compiler_params=pltpu.CompilerParams(
            dimension_semantics=("arbitrary",), vmem_limit_bytes=VMEM_LIMIT_BYTES),
        name="ffn",
    )(xs, *arrays, *nxt[0])


def _ffn_final(xs, params, final_g, bsz, seq):
    d = xs.shape[-1]
    tiles_per_seq = seq // TOK_TILE
    arrays, specs = params
    return pl.pallas_call(
        functools.partial(_ffn_kernel, tiles_per_seq, True),
        grid=(bsz * tiles_per_seq,),
        in_specs=[_stream_spec(d, first_step=1)] + specs + [_layer_spec(final_g, 0)],
        out_specs=pl.BlockSpec((None, TOK_TILE, d), lambda s: (s // tiles_per_seq, s % tiles_per_seq, 0)),
        out_shape=jax.ShapeDtypeStruct((bsz, seq, d), F32),
        compiler_params=pltpu.CompilerParams(
            dimension_semantics=("arbitrary",), vmem_limit_bytes=VMEM_LIMIT_BYTES),
        name="ffn_final",
    )(xs, *arrays, final_g)


def kernel(x, c, ctx, c_ctx, w_mod, b_mod, norm1_g, norm2_g, w_in, b_in, conv_w, w_out_a, lam_re, lam_im,
           log_dt, b_re, b_im, c_re, c_im, d_skip, w_glu, b_glu, w_o, w_ff_in, w_ff_out, final_g):
    bsz, seq, d = x.shape
    depth = w_mod.shape[0]
    ctx_len = ctx.shape[1]
    d_conv = conv_w.shape[-1]
    d_ssm = d_skip.shape[-1]
    u_lo, u_hi = 3 * d_conv, 3 * d_conv + d_ssm
    assert bsz * ctx_len == TOK_TILE and TOK_TILE % GRID_W == 0 and bsz <= 2
    assert ctx_len == SSM_CHUNK * SUBLANES and d_ssm % LANES == 0 and conv_w.shape[1] == CONV_K
    assert seq % (SUBLANES * TOK_TILE) == 0
    seg_steps = seq // (SSM_CHUNK * SUBLANES)

    cvec = jnp.zeros((8, d), F32).at[:bsz].set(c.astype(F32)).at[2].set(c_ctx.astype(F32))
    mod = _mod_all(cvec, w_mod, b_mod)

    mt_op, wt_op, vt_op, a_op = _ssm_operators(lam_re, lam_im, log_dt, b_re, b_im, c_re, c_im, d_skip,
                                               seg_steps)

    w_in, w_out_a, w_glu, w_o = (w.astype(BF16) for w in (w_in, w_out_a, w_glu, w_o))
    w_ff_in, w_ff_out = w_ff_in.astype(BF16), w_ff_out.astype(BF16)
    row = lambda v: v.reshape(v.shape[0], 1, v.shape[-1])
    b_in, b_glu, g1, g2, fg = row(b_in), row(b_glu), row(norm1_g), row(norm2_g), final_g.reshape(1, 1, d)
    gates_lo = u_hi

    def operands(l, *items):
        arrays = [arr for arr, _ in items]
        return arrays, [_layer_spec(arr, l, cols) for arr, cols in items]

    u_cols = (u_lo, d_ssm)
    ssm_in = lambda l: operands(l, (mod, None), (g1, None), (w_in, u_cols), (b_in, u_cols)) + (d_ssm,)
    mixer_in = lambda l: operands(
        l, (mod, None), (g1, None), (w_in, (0, u_lo)), (b_in, (0, u_lo)),
        (w_in, (gates_lo, 2 * d)), (b_in, (gates_lo, 2 * d)), (conv_w, None), (w_out_a, None),
        (w_glu, None), (b_glu, None), (w_o, None))
    ffn_in = lambda l: operands(l, (mod, None), (g2, None), (w_ff_in, None), (w_ff_out, None))

    tokens = (x, ctx.reshape(bsz * ctx_len, d))
    u_lat, u_ctx = _uproj(*tokens, ssm_in(0))
    for l in range(depth):
        ys_lat, ys_ctx = _ssm(u_lat, u_ctx, mt_op, wt_op, vt_op, a_op, l)
        xs = _mixer(tokens, ys_lat, ys_ctx, mixer_in(l), bsz, seq, ctx_len)
        if l + 1 < depth:
            xs, u_lat, u_ctx = _ffn(xs, ffn_in(l), ssm_in(l + 1), bsz, seq)
            tokens = (xs,)
    return _ffn_final(xs, ffn_in(depth - 1), fg, bsz, seq)
```
